```python
import math
import jax
import jax.numpy as jnp
from jax import lax
import numpy as np

D_MODEL = 1024
BATCH = 4
SEQ = 4096
DEPTH = 2

N_BRANCH = 4
BRANCH_W = 512

SSD_D_INNER = 512
SSD_HEAD_DIM = 64
SSD_HEADS = SSD_D_INNER // SSD_HEAD_DIM
SSD_GROUPS = 2
SSD_HPG = SSD_HEADS // SSD_GROUPS
SSD_STATE = 64
SSD_CONV = 4
SSD_CHUNK = 128
SSD_CONV_CH = SSD_D_INNER + 2 * SSD_GROUPS * SSD_STATE

DIL_PAIRS = ((128, 1), (512, 4), (2048, 16))
DIL_HPG = 4
DIL_HEAD_DIM = 128
DIL_HEADS = DIL_HPG * len(DIL_PAIRS)
DIL_BLOCK = 128

DIFF_HEADS = 4
DIFF_HEAD_DIM = 64
DIFF_QBLOCK = 128

HG_HEADS = 8
HG_KEY_DIM = 64
HG_VAL_DIM = BRANCH_W // HG_HEADS
HG_CHUNK = 64

N_EXPERTS = 64
N_EXPERT_GROUPS = 8
EXPERTS_PER_GROUP = N_EXPERTS // N_EXPERT_GROUPS
TOP_K = 2
D_EXPERT = 256
MOE_BLOCK = 128

NORM_EPS = 1e-6
MASK_VALUE = -1e30

IN_WIDTHS = (
    SSD_D_INNER,
    SSD_CONV_CH,
    SSD_HEADS,
    DIL_HEADS * DIL_HEAD_DIM,
    DIL_HEADS * DIL_HEAD_DIM,
    DIL_HEADS * DIL_HEAD_DIM,
    2 * DIFF_HEADS * DIFF_HEAD_DIM,
    2 * DIFF_HEADS * DIFF_HEAD_DIM,
    DIFF_HEADS * 2 * DIFF_HEAD_DIM,
    HG_HEADS * HG_KEY_DIM,
    HG_HEADS * HG_KEY_DIM,
    HG_HEADS * HG_VAL_DIM,
    HG_HEADS * HG_VAL_DIM,
    N_BRANCH * D_MODEL,
)
IN_OFFSETS = tuple(sum(IN_WIDTHS[:i + 1]) for i in range(len(IN_WIDTHS) - 1))
D_IN = sum(IN_WIDTHS)

kernel_name = "hybrid_gated_ssd_dilated_diff_hgrn2_moe"


def rms_norm(x, w):
    xf = x.astype(jnp.float32)
    y = xf * lax.rsqrt(jnp.mean(xf * xf, axis=-1, keepdims=True) + NORM_EPS)
    return (y * w.astype(jnp.float32)).astype(x.dtype)


def modulate(h, shift, scale):
    return h * (1.0 + scale[:, None, :]) + shift[:, None, :]


def causal_depthwise_conv(x, w, b):
    k = w.shape[0]
    y = lax.conv_general_dilated(x, w[:, None, :], window_strides=(1,), padding=[(k - 1, 0)],
                                 dimension_numbers=("NWC", "WIO", "NWC"),
                                 feature_group_count=x.shape[-1])
    return y + b


def segsum(x):
    t = x.shape[-1]
    cs = jnp.cumsum(x, axis=-1)
    diff = cs[..., :, None] - cs[..., None, :]
    mask = jnp.tril(jnp.ones((t, t), dtype=bool))
    return jnp.where(mask, diff, MASK_VALUE)


def ssd_chunked(xh, dt, a, bm, cm):
    b, s, g, e, p = xh.shape
    n = bm.shape[-1]
    nc = s // SSD_CHUNK
    x = (xh * dt[..., None]).reshape(b, nc, SSD_CHUNK, g, e, p)
    adt = (dt * a).reshape(b, nc, SSD_CHUNK, g, e).transpose(0, 3, 4, 1, 2)
    bm = bm.reshape(b, nc, SSD_CHUNK, g, n)
    cm = cm.reshape(b, nc, SSD_CHUNK, g, n)
    a_cs = jnp.cumsum(adt, axis=-1)
    lmat = jnp.exp(segsum(adt))
    cb = jnp.einsum("bclgn,bcsgn->bgcls", cm, bm)
    y_diag = jnp.einsum("bgecls,bcsgep->bclgep", cb[:, :, None] * lmat, x)
    decay_states = jnp.exp(a_cs[..., -1:] - a_cs)
    states = jnp.einsum("bclgn,bgecl,bclgep->bcgepn", bm, decay_states, x)
    states = jnp.concatenate([jnp.zeros_like(states[:, :1]), states], axis=1)
    chunk_tot = jnp.pad(a_cs[..., -1], ((0, 0), (0, 0), (0, 0), (1, 0)))
    decay_chunk = jnp.exp(segsum(chunk_tot))
    states = jnp.einsum("bgezc,bcgepn->bzgepn", decay_chunk, states)[:, :-1]
    y_off = jnp.einsum("bclgn,bcgepn,bgecl->bclgep", cm, states, jnp.exp(a_cs))
    return (y_diag + y_off).reshape(b, s, g, e, p)


def banded_causal_attention(q, k, v, max_dist):
    n, l, h, dh = q.shape
    blk = DIL_BLOCK
    nb = -(-l // blk)
    lp = nb * blk
    pad_end = lp - l
    qb = jnp.pad(q, ((0, 0), (0, pad_end), (0, 0), (0, 0))).reshape(n, nb, blk, h, dh)

    def key_blocks(t):
        tp = jnp.pad(t, ((0, 0), (blk, pad_end), (0, 0), (0, 0))).reshape(n, nb + 1, blk, h, dh)
        return jnp.concatenate([tp[:, :-1], tp[:, 1:]], axis=2)

    kb = key_blocks(k)
    vb = key_blocks(v)
    q_idx = jnp.arange(nb)[:, None] * blk + jnp.arange(blk)[None, :]
    k_idx = jnp.arange(nb)[:, None] * blk - blk + jnp.arange(2 * blk)[None, :]
    dist = q_idx[:, :, None] - k_idx[:, None, :]
    valid = (dist >= 0) & (dist <= max_dist) & (k_idx[:, None, :] >= 0)
    sc = jnp.einsum("nbqhd,nbkhd->nbhqk", qb, kb).astype(jnp.float32) * (dh ** -0.5)
    sc = jnp.where(valid[None, :, None], sc, MASK_VALUE)
    m = jnp.max(sc, axis=-1, keepdims=True)
    pr = jnp.exp(sc - m)
    denom = jnp.sum(pr, axis=-1, keepdims=True)
    o = jnp.einsum("nbhqk,nbkhd->nbqhd", pr / denom, vb.astype(jnp.float32))
    lse = (m + jnp.log(denom))[..., 0]
    o = o.reshape(n, lp, h, dh)[:, :l]
    lse = lse.transpose(0, 1, 3, 2).reshape(n, lp, h)[:, :l]
    return o, lse


def dilated_attention(q, k, v):
    b, s, _, dh = q.shape
    outs, lses = [], []
    for g, (window, dil) in enumerate(DIL_PAIRS):
        hs = slice(g * DIL_HPG, (g + 1) * DIL_HPG)
        sub_len = s // dil

        def to_sub(t):
            return t.reshape(b, sub_len, dil, DIL_HPG, dh).transpose(0, 2, 1, 3, 4).reshape(b * dil, sub_len, DIL_HPG, dh)

        o, lse = banded_causal_attention(to_sub(q[:, :, hs]), to_sub(k[:, :, hs]), to_sub(v[:, :, hs]), window // dil)
        outs.append(o.reshape(b, dil, sub_len, DIL_HPG, dh).transpose(0, 2, 1, 3, 4).reshape(b, s, DIL_HPG, dh))
        lses.append(lse.reshape(b, dil, sub_len, DIL_HPG).transpose(0, 2, 1, 3).reshape(b, s, DIL_HPG))
    wts = jax.nn.softmax(jnp.stack(lses, axis=0), axis=0)
    return jnp.sum(wts[..., None] * jnp.stack(outs, axis=0), axis=0)


def differential_attention(q, k, v, lam, lam_init, subln_w):
    b, s, h2, dh = q.shape
    h = h2 // 2
    nblk = s // DIFF_QBLOCK
    qb = q.reshape(b, nblk, DIFF_QBLOCK, h2, dh).transpose(1, 0, 2, 3, 4)
    k_pos = jnp.arange(s)
    vf = v.astype(jnp.float32)
    scale = dh ** -0.5

    def one_block(args):
        q_blk, blk = args
        sc = jnp.einsum("bqhd,bkhd->bhqk", q_blk, k).astype(jnp.float32) * scale
        q_pos = blk * DIFF_QBLOCK + jnp.arange(DIFF_QBLOCK)
        causal = k_pos[None, :] <= q_pos[:, None]
        pr = jax.nn.softmax(jnp.where(causal, sc, MASK_VALUE), axis=-1).reshape(b, h, 2, DIFF_QBLOCK, s)
        attn = pr[:, :, 0] - lam * pr[:, :, 1]
        return jnp.einsum("bhqk,bkhe->bqhe", attn, vf)

    o = lax.map(one_block, (qb, jnp.arange(nblk)))
    o = o.transpose(1, 0, 2, 3, 4).reshape(b, s, h, 2 * dh)
    return rms_norm(o, subln_w) * (1.0 - lam_init)


def hgrn2_chunked(q, k, v, log_f):
    b, s, h, dk = q.shape
    dv = v.shape[-1]
    nc = s // HG_CHUNK

    def chunks(t):
        return t.reshape(b, nc, HG_CHUNK, h, t.shape[-1]).transpose(1, 0, 3, 2, 4)

    causal = jnp.tril(jnp.ones((HG_CHUNK, HG_CHUNK), dtype=bool))

    def step(state, inp):
        qi, ki, vi, gi = inp
        gcum = jnp.cumsum(gi, axis=2)
        o_inter = jnp.einsum("bhik,bhkv->bhiv", qi * jnp.exp(gcum), state)
        rel = gcum[:, :, :, None, :] - gcum[:, :, None, :, :]
        decay = jnp.exp(jnp.where(causal[:, :, None], rel, MASK_VALUE))
        attn = jnp.einsum("bhik,bhjk,bhijk->bhij", qi, ki, decay)
        o_intra = jnp.einsum("bhij,bhjv->bhiv", attn, vi)
        g_last = gcum[:, :, -1:]
        new_state = jnp.exp(g_last[:, :, 0])[..., None] * state + jnp.einsum("bhjk,bhjv->bhkv", ki * jnp.exp(g_last - gcum), vi)
        return new_state, o_inter + o_intra

    state0 = jnp.zeros((b, h, dk, dv), jnp.float32)
    _, o = lax.scan(step, state0, (chunks(q), chunks(k), chunks(v), chunks(log_f)))
    return o.transpose(1, 0, 3, 2, 4).reshape(b, s, h, dv)


def token_mixer(h, layer, w_in, conv_w, conv_b, dt_bias, a_log, d_skip, ssd_norm_w, diff_lambda,
                diff_subln_w, lower_bound, hgrn_norm_w, w_merge, w_out):
    b, s, d = h.shape
    f32 = jnp.float32
    (z, xbc, dt_raw, q_b, k_b, v_b, q_c, k_c, v_c, q_d, f_d, i_d, g_d, gate_logits) = jnp.split(h @ w_in, IN_OFFSETS, axis=-1)

    xbc = jax.nn.silu(causal_depthwise_conv(xbc, conv_w, conv_b))
    xs, bm, cm = jnp.split(xbc, (SSD_D_INNER, SSD_D_INNER + SSD_GROUPS * SSD_STATE), axis=-1)
    dt = jax.nn.softplus(dt_raw.astype(f32) + dt_bias.astype(f32))
    a = -jnp.exp(a_log.astype(f32))
    xh = xs.astype(f32).reshape(b, s, SSD_GROUPS, SSD_HPG, SSD_HEAD_DIM)
    y = ssd_chunked(xh, dt.reshape(b, s, SSD_GROUPS, SSD_HPG), a.reshape(SSD_GROUPS, SSD_HPG),
                    bm.astype(f32).reshape(b, s, SSD_GROUPS, SSD_STATE), cm.astype(f32).reshape(b, s, SSD_GROUPS, SSD_STATE))
    y = y + d_skip.astype(f32).reshape(SSD_GROUPS, SSD_HPG)[..., None] * xh
    y = y.reshape(b, s, SSD_D_INNER).astype(h.dtype)
    o_a = rms_norm(y * jax.nn.silu(z), ssd_norm_w)

    dil_shape = (b, s, DIL_HEADS, DIL_HEAD_DIM)
    o_b = dilated_attention(q_b.reshape(dil_shape), k_b.reshape(dil_shape), v_b.reshape(dil_shape))
    o_b = o_b.reshape(b, s, BRANCH_W).astype(h.dtype)

    lam_q1, lam_k1, lam_q2, lam_k2 = diff_lambda.astype(f32)
    lam_init = 0.8 - 0.6 * math.exp(-0.3 * layer)
    lam = jnp.exp(jnp.sum(lam_q1 * lam_k1)) - jnp.exp(jnp.sum(lam_q2 * lam_k2)) + lam_init
    qk_shape = (b, s, 2 * DIFF_HEADS, DIFF_HEAD_DIM)
    o_c = differential_attention(q_c.reshape(qk_shape), k_c.reshape(qk_shape),
                                 v_c.reshape(b, s, DIFF_HEADS, 2 * DIFF_HEAD_DIM), lam, lam_init, diff_subln_w)
    o_c = o_c.reshape(b, s, BRANCH_W).astype(h.dtype)

    lb = lower_bound.reshape(HG_HEADS, HG_KEY_DIM)
    f_logit = f_d.astype(f32).reshape(b, s, HG_HEADS, HG_KEY_DIM)
    f_gate = lb + (1.0 - lb) * jax.nn.sigmoid(f_logit)
    log_f = jnp.log(f_gate)
    k_in = 1.0 - f_gate
    q_hg = jax.nn.silu(q_d.astype(f32).reshape(b, s, HG_HEADS, HG_KEY_DIM))
    v_hg = i_d.astype(f32).reshape(b, s, HG_HEADS, HG_VAL_DIM)
    o = hgrn2_chunked(q_hg, k_in, v_hg, log_f)
    o_d = rms_norm(o, hgrn_norm_w) * jax.nn.silu(g_d.astype(f32).reshape(b, s, HG_HEADS, HG_VAL_DIM))
    o_d = o_d.reshape(b, s, BRANCH_W).astype(h.dtype)

    branches = jnp.stack([o_a, o_b, o_c, o_d], axis=2)
    proj = jnp.einsum("bsnw,nwd->bsnd", branches, w_merge)
    gates = jax.nn.sigmoid(gate_logits.reshape(b, s, N_BRANCH, d))
    return jnp.sum(gates * proj, axis=2) @ w_out


def moe_ffn(h, w_router, router_bias, w_gate, w_up, w_down):
    b, s, d = h.shape
    n_tok = b * s
    t = h.reshape(n_tok, d)
    scores = jax.nn.sigmoid((t @ w_router).astype(jnp.float32))
    sel = scores + router_bias.astype(jnp.float32)
    group_score = jnp.sum(lax.top_k(sel.reshape(n_tok, N_EXPERT_GROUPS, EXPERTS_PER_GROUP), 2)[0], axis=-1)
    best_group = jnp.argmax(group_score, axis=-1)
    in_group = (jnp.arange(N_EXPERTS) // EXPERTS_PER_GROUP)[None, :] == best_group[:, None]
    _, idx = lax.top_k(jnp.where(in_group, sel, MASK_VALUE), TOP_K)
    wts = jnp.take_along_axis(scores, idx, axis=-1)
    wts = wts / jnp.sum(wts, axis=-1, keepdims=True)

    flat_e = idx.reshape(-1).astype(jnp.int32)
    flat_tok = jnp.repeat(jnp.arange(n_tok, dtype=jnp.int32), TOP_K)
    flat_w = wts.reshape(-1)
    order = jnp.argsort(flat_e)
    e_s, tok_s, w_s = flat_e[order], flat_tok[order], flat_w[order]
    counts = jnp.bincount(flat_e, length=N_EXPERTS)
    padded = ((counts + MOE_BLOCK - 1) // MOE_BLOCK) * MOE_BLOCK
    start = jnp.cumsum(counts) - counts
    pend = jnp.cumsum(padded)
    pstart = pend - padded
    dest = pstart[e_s] + (jnp.arange(n_tok * TOP_K) - start[e_s])
    n_rows = n_tok * TOP_K + N_EXPERTS * MOE_BLOCK
    n_blocks = n_rows // MOE_BLOCK
    row_tok = jnp.zeros((n_rows,), jnp.int32).at[dest].set(tok_s)
    row_w = jnp.zeros((n_rows,), jnp.float32).at[dest].set(w_s)
    blk_start = jnp.arange(n_blocks) * MOE_BLOCK
    blk_e = jnp.minimum(jnp.sum(pend[None, :] <= blk_start[:, None], axis=1), N_EXPERTS - 1)
    xs = t[row_tok].reshape(n_blocks, MOE_BLOCK, d)

    def expert_block(args):
        xb, e = args
        hid = jax.nn.silu(xb @ w_gate[e]) * (xb @ w_up[e])
        return hid @ w_down[e]

    ys = lax.map(expert_block, (xs, blk_e)).reshape(n_rows, d)
    out = jnp.zeros((n_tok, d), h.dtype).at[row_tok].add(ys * row_w[:, None].astype(h.dtype))
    return out.reshape(b, s, d)


def setup_inputs(seed: int = 0) -> dict:
    key = jax.random.key(seed)
    ks = jax.random.split(key, 26)
    f32 = jnp.float32

    def nrm(k, shape, scale):
        return jax.random.normal(k, shape, f32) * scale

    dt0 = jnp.exp(jax.random.uniform(ks[9], (DEPTH, SSD_HEADS), f32) * (math.log(0.1) - math.log(1e-3)) + math.log(1e-3))
    return {
        "x": nrm(ks[0], (BATCH, SEQ, D_MODEL), 1.0),
        "c": nrm(ks[1], (BATCH, D_MODEL), 1.0),
        "w_ada": nrm(ks[2], (DEPTH, D_MODEL, 6 * D_MODEL), 0.5 * D_MODEL ** -0.5),
        "b_ada": nrm(ks[3], (DEPTH, 6 * D_MODEL), 0.01),
        "norm_mix_w": 1.0 + nrm(ks[4], (DEPTH, D_MODEL), 0.02),
        "norm_ffn_w": 1.0 + nrm(ks[5], (DEPTH, D_MODEL), 0.02),
        "w_in": nrm(ks[6], (DEPTH, D_MODEL, D_IN), D_MODEL ** -0.5),
        "conv_w": nrm(ks[7], (DEPTH, SSD_CONV, SSD_CONV_CH), SSD_CONV ** -0.5),
        "conv_b": nrm(ks[8], (DEPTH, SSD_CONV_CH), 0.01),
        "ssd_dt_bias": dt0 + jnp.log(-jnp.expm1(-dt0)),
        "ssd_a_log": jnp.log(jax.random.uniform(ks[10], (DEPTH, SSD_HEADS), f32, 1.0, 16.0)),
        "ssd_d": 1.0 + nrm(ks[11], (DEPTH, SSD_HEADS), 0.1),
        "ssd_norm_w": 1.0 + nrm(ks[12], (DEPTH, SSD_D_INNER), 0.02),
        "diff_lambda": nrm(ks[13], (DEPTH, 4, DIFF_HEAD_DIM), 0.1),
        "diff_subln_w": 1.0 + nrm(ks[14], (DEPTH, 2 * DIFF_HEAD_DIM), 0.02),
        "hgrn_lb_logits": nrm(ks[15], (DEPTH, HG_HEADS * HG_KEY_DIM), 1.0),
        "hgrn_norm_w": 1.0 + nrm(ks[16], (DEPTH, HG_VAL_DIM), 0.02),
        "w_merge": nrm(ks[17], (DEPTH, N_BRANCH, BRANCH_W, D_MODEL), BRANCH_W ** -0.5),
        "w_out": nrm(ks[18], (DEPTH, D_MODEL, D_MODEL), D_MODEL ** -0.5),
        "w_router": nrm(ks[19], (D_MODEL, N_EXPERTS), D_MODEL ** -0.5),
        "router_bias": nrm(ks[20], (N_EXPERTS,), 0.01),
        "w_expert_gate": nrm(ks[21], (DEPTH, N_EXPERTS, D_MODEL, D_EXPERT), D_MODEL ** -0.5),
        "w_expert_up": nrm(ks[22], (DEPTH, N_EXPERTS, D_MODEL, D_EXPERT), D_MODEL ** -0.5),
        "w_expert_down": nrm(ks[23], (DEPTH, N_EXPERTS, D_EXPERT, D_MODEL), D_EXPERT ** -0.5),
        "final_norm_w": 1.0 + nrm(ks[24], (D_MODEL,), 0.02),
    }


def reference(x, c, w_ada, b_ada, norm_mix_w, norm_ffn_w, w_in, conv_w, conv_b, ssd_dt_bias, ssd_a_log,
              ssd_d, ssd_norm_w, diff_lambda, diff_subln_w, hgrn_lb_logits, hgrn_norm_w, w_merge, w_out,
              w_router, router_bias, w_expert_gate, w_expert_up, w_expert_down, final_norm_w):
    lb_p = jax.nn.softmax(hgrn_lb_logits.astype(jnp.float32), axis=0)
    lower_bounds = jnp.cumsum(lb_p, axis=0) - lb_p[0]
    c_act = jax.nn.silu(c)
    for l in range(DEPTH):
        mod = c_act @ w_ada[l] + b_ada[l]
        sh1, sc1, g1, sh2, sc2, g2 = jnp.split(mod, 6, axis=-1)
        h = modulate(rms_norm(x, norm_mix_w[l]), sh1, sc1)
        mix = token_mixer(h, l, w_in[l], conv_w[l], conv_b[l], ssd_dt_bias[l], ssd_a_log[l], ssd_d[l],
                          ssd_norm_w[l], diff_lambda[l], diff_subln_w[l], lower_bounds[l], hgrn_norm_w[l],
                          w_merge[l], w_out[l])
        x = x + g1[:, None, :] * mix
        h = modulate(rms_norm(x, norm_ffn_w[l]), sh2, sc2)
        x = x + g2[:, None, :] * moe_ffn(h, w_router, router_bias, w_expert_gate[l], w_expert_up[l], w_expert_down[l])
    return rms_norm(x, final_norm_w)
```

```python
import functools
import math

import jax
import jax.numpy as jnp
from jax import lax
from jax.experimental import pallas as pl
from jax.experimental.pallas import tpu as pltpu

F32 = jnp.float32
BF16 = jnp.bfloat16

D_MODEL = 1024
DEPTH = 2
N_BRANCH = 4
BRANCH_W = 512

SSD_D_INNER = 512
SSD_HEAD_DIM = 64
SSD_HEADS = 8
SSD_GROUPS = 2
SSD_HPG = 4
SSD_STATE = 64
SSD_CONV = 4
SSD_CHUNK = 128
SSD_CONV_CH = 768

DIL_PAIRS = ((128, 1), (512, 4), (2048, 16))
DIL_HPG = 4
DIL_HEAD_DIM = 128
DIL_HEADS = 12
DIL_BLOCK = 128

DIFF_HEADS = 4
DIFF_HEAD_DIM = 64

HG_HEADS = 8
HG_KEY_DIM = 64
HG_VAL_DIM = 64
HG_CHUNK = 64

N_EXPERTS = 64
N_EXPERT_GROUPS = 8
EXPERTS_PER_GROUP = 8
TOP_K = 2
D_EXPERT = 256

NORM_EPS = 1e-6
MASK_VALUE = -1e30

LANE = 128
VMEM_LIMIT = 48 * 1024 * 1024

C_GATE = 0
C_Z = 32
C_XBC = 36
C_DT = 42
C_QB = 44
C_KB = 56
C_VB = 68
C_QC = 80
C_KC = 84
C_VC = 88
C_QD = 92
C_FD = 96
C_ID = 100
C_GD = 104
N_COLB = 108
D_IN_PAD = N_COLB * LANE
DT_PAD = 2 * LANE


def _cparams(sem):
    return pltpu.CompilerParams(dimension_semantics=sem, vmem_limit_bytes=VMEM_LIMIT)


def _silu(v):
    return v * jax.nn.sigmoid(v)


def _bdot(a, b):
    return jnp.dot(a.astype(BF16), b.astype(BF16), preferred_element_type=F32)


def _bdot_nt(a, b):
    return lax.dot_general(a.astype(BF16), b.astype(BF16), (((1,), (1,)), ((), ())),
                           preferred_element_type=F32)


def _bdot_tn(a, b):
    return lax.dot_general(a.astype(BF16), b.astype(BF16), (((0,), (0,)), ((), ())),
                           preferred_element_type=F32)


def _exact_rows_dot(m01, v):
    hi = v.astype(BF16)
    r1 = v - hi.astype(F32)
    mid = r1.astype(BF16)
    lo = (r1 - mid.astype(F32)).astype(BF16)
    m = m01.astype(BF16)
    return (jnp.dot(m, hi, preferred_element_type=F32) + jnp.dot(m, mid, preferred_element_type=F32)
            + jnp.dot(m, lo, preferred_element_type=F32))


def _ada_kernel(c_ref, w_ref, b_ref, o_ref):
    o_ref[...] = _bdot(_silu(c_ref[...]), w_ref[...]) + b_ref[...]


def ada_modulation(c, w_ada, b_ada):
    depth, d, n = w_ada.shape
    b = c.shape[0]
    bp = 8
    c_pad = jnp.zeros((bp, d), F32).at[:b].set(c)
    tn = 1536
    out = pl.pallas_call(
        _ada_kernel,
        grid=(depth, n // tn),
        in_specs=[
            pl.BlockSpec((bp, d), lambda l, j: (0, 0)),
            pl.BlockSpec((None, d, tn), lambda l, j: (l, 0, j)),
            pl.BlockSpec((None, 1, tn), lambda l, j: (l, 0, j)),
        ],
        out_specs=pl.BlockSpec((None, bp, tn), lambda l, j: (l, 0, j)),
        out_shape=jax.ShapeDtypeStruct((depth, bp, n), F32),
        compiler_params=_cparams(("arbitrary", "arbitrary")),
        name="ada_mod",
    )(c_pad, w_ada, b_ada.reshape(depth, 1, n))
    return out[:, :b]


def _inproj_kernel(x_ref, nw_ref, sc_ref, sh_ref, w_ref, o_ref, h_ref):
    @pl.when(pl.program_id(1) == 0)
    def _():
        x = x_ref[...]
        y = x * lax.rsqrt(jnp.mean(x * x, axis=-1, keepdims=True) + NORM_EPS) * nw_ref[...]
        h_ref[...] = (y * (1.0 + sc_ref[...]) + sh_ref[...]).astype(BF16)

    o_ref[...] = jnp.dot(h_ref[...], w_ref[...], preferred_element_type=F32)


def in_projection(x2, norm_w, scale, shift, w_pad, seq):
    t, d = x2.shape
    n = w_pad.shape[1]
    tm = min(1024, seq)
    tn = 512
    per_b = seq // tm
    return pl.pallas_call(
        _inproj_kernel,
        grid=(t // tm, n // tn),
        in_specs=[
            pl.BlockSpec((tm, d), lambda i, j: (i, 0)),
            pl.BlockSpec((1, d), lambda i, j: (0, 0)),
            pl.BlockSpec((None, 1, d), lambda i, j: (i // per_b, 0, 0)),
            pl.BlockSpec((None, 1, d), lambda i, j: (i // per_b, 0, 0)),
            pl.BlockSpec((d, tn), lambda i, j: (0, j)),
        ],
        out_specs=pl.BlockSpec((tm, tn), lambda i, j: (i, j)),
        out_shape=jax.ShapeDtypeStruct((t, n), F32),
        scratch_shapes=[pltpu.VMEM((tm, d), BF16)],
        compiler_params=_cparams(("arbitrary", "arbitrary")),
        name="in_proj",
    )(x2, norm_w.reshape(1, d), scale, shift, w_pad)


def pad_w_in(w_in_l):
    d = w_in_l.shape[0]
    o_dt = SSD_D_INNER + SSD_CONV_CH
    o_gate = w_in_l.shape[1] - N_BRANCH * D_MODEL
    return jnp.concatenate(
        [w_in_l[:, o_gate:], w_in_l[:, :o_dt + SSD_HEADS], jnp.zeros((d, DT_PAD - SSD_HEADS), w_in_l.dtype),
         w_in_l[:, o_dt + SSD_HEADS:o_gate]], axis=1).astype(BF16)


def _ssd_kernel(z_ref, xbc_ref, dt_ref, cw_ref, cb_ref, dtb_ref, alog_ref, dsk_ref, nw_ref, o_ref,
                xpad_ref, state_ref, y_ref):
    c = pl.program_id(1)
    L = SSD_CHUNK
    halo = 8

    @pl.when(c == 0)
    def _():
        xpad_ref[0:halo, :] = jnp.zeros((halo, SSD_CONV_CH), F32)
        state_ref[...] = jnp.zeros_like(state_ref)

    @pl.when(c > 0)
    def _():
        xpad_ref[0:halo, :] = xpad_ref[L:L + halo, :]

    z = z_ref[...]
    xpad_ref[halo:halo + L, :] = xbc_ref[...]

    conv = cb_ref[...]
    for j in range(SSD_CONV):
        off = halo - (SSD_CONV - 1) + j
        conv = conv + cw_ref[j:j + 1, :] * xpad_ref[off:off + L, :]
    xc = _silu(conv)
    xs = xc[:, 0:SSD_D_INNER]
    ns = SSD_GROUPS * SSD_STATE
    bm = xc[:, SSD_D_INNER:SSD_D_INNER + ns]
    cm = xc[:, SSD_D_INNER + ns:SSD_D_INNER + 2 * ns]

    dt = jax.nn.softplus(dt_ref[:, 0:LANE] + dtb_ref[...])
    a = -jnp.exp(alog_ref[...])
    adt = dt * a
    row = lax.broadcasted_iota(jnp.int32, (L, L), 0)
    col = lax.broadcasted_iota(jnp.int32, (L, L), 1)
    tril = row >= col
    a_cs = _exact_rows_dot(jnp.where(tril, 1.0, 0.0), adt)
    a_cs_t = a_cs.T
    tot = a_cs[L - 1:L, :]

    cbs = [_bdot_nt(cm[:, g * SSD_STATE:(g + 1) * SSD_STATE], bm[:, g * SSD_STATE:(g + 1) * SSD_STATE])
           for g in range(SSD_GROUPS)]
    for e in range(SSD_HEADS):
        g = e // SSD_HPG
        cs_col = a_cs[:, e:e + 1]
        cs_row = a_cs_t[e:e + 1, :]
        lmat = jnp.exp(jnp.where(tril, cs_col - cs_row, MASK_VALUE))
        xe = xs[:, e * SSD_HEAD_DIM:(e + 1) * SSD_HEAD_DIM]
        xdt = xe * dt[:, e:e + 1]
        bg = bm[:, g * SSD_STATE:(g + 1) * SSD_STATE]
        cg = cm[:, g * SSD_STATE:(g + 1) * SSD_STATE]
        st = state_ref[e]
        y = _bdot(cbs[g] * lmat, xdt)
        y = y + _bdot(cg, st) * jnp.exp(cs_col)
        tot_e = tot[:, e:e + 1]
        state_ref[e] = jnp.exp(tot_e) * st + _bdot_tn(bg * jnp.exp(tot_e - cs_col), xdt)
        y_ref[:, e * SSD_HEAD_DIM:(e + 1) * SSD_HEAD_DIM] = y + dsk_ref[:, e:e + 1] * xe

    yz = y_ref[...] * _silu(z)
    o_ref[...] = yz * lax.rsqrt(jnp.mean(yz * yz, axis=-1, keepdims=True) + NORM_EPS) * nw_ref[...]


def _pad_lanes(v, n=LANE):
    return jnp.zeros((1, n), F32).at[0, :v.shape[0]].set(v.astype(F32))


def ssd_branch(p3, conv_w, conv_b, dt_bias, a_log, d_skip, norm_w):
    b, s, _ = p3.shape
    L = SSD_CHUNK
    const = lambda shape: pl.BlockSpec(shape, lambda i, c: (0, 0))
    return pl.pallas_call(
        _ssd_kernel,
        grid=(b, s // L),
        in_specs=[
            pl.BlockSpec((None, L, SSD_D_INNER), lambda i, c: (i, c, C_Z * LANE // SSD_D_INNER)),
            pl.BlockSpec((None, L, SSD_CONV_CH), lambda i, c: (i, c, C_XBC * LANE // SSD_CONV_CH)),
            pl.BlockSpec((None, L, DT_PAD), lambda i, c: (i, c, C_DT * LANE // DT_PAD)),
            const((SSD_CONV, SSD_CONV_CH)),
            const((1, SSD_CONV_CH)),
            const((1, LANE)),
            const((1, LANE)),
            const((1, LANE)),
            const((1, SSD_D_INNER)),
        ],
        out_specs=pl.BlockSpec((None, L, SSD_D_INNER), lambda i, c: (i, c, 0)),
        out_shape=jax.ShapeDtypeStruct((b, s, SSD_D_INNER), F32),
        scratch_shapes=[
            pltpu.VMEM((L + 8, SSD_CONV_CH), F32),
            pltpu.VMEM((SSD_HEADS, SSD_STATE, SSD_HEAD_DIM), F32),
            pltpu.VMEM((L, SSD_D_INNER), F32),
        ],
        compiler_params=_cparams(("arbitrary", "arbitrary")),
        name="ssd",
    )(p3, p3, p3, conv_w, conv_b.reshape(1, -1), _pad_lanes(dt_bias), _pad_lanes(a_log), _pad_lanes(d_skip),
      norm_w.reshape(1, -1))


def _dil_kernel(q_ref, k_ref, v_ref, o_ref, m_ref, l_ref):
    g = pl.program_id(2)
    s_len = q_ref.shape[0]
    blk = DIL_BLOCK
    scale = DIL_HEAD_DIM ** -0.5

    @pl.when(g == 0)
    def _():
        m_ref[...] = jnp.full_like(m_ref, MASK_VALUE)
        l_ref[...] = jnp.zeros_like(l_ref)
        o_ref[...] = jnp.zeros_like(o_ref)

    row = lax.broadcasted_iota(jnp.int32, (blk, blk), 0)
    col = lax.broadcasted_iota(jnp.int32, (blk, blk), 1)
    cur_ok = col <= row
    prev_ok = col >= row

    def run_group(dil):
        nblk = s_len // (blk * dil)

        def rows(start):
            if dil == 1:
                return pl.ds(pl.multiple_of(start, blk), blk)
            return pl.ds(start, blk, stride=dil)

        def unit(u, carry):
            r = u // nblk
            b = u % nblk
            base = r + dil * blk * b
            has_prev = b > 0
            pbase = jnp.where(has_prev, base - dil * blk, base)
            qs = rows(base)
            ps = rows(pbase)
            q = q_ref[qs, :] * scale
            s_cur = jnp.where(cur_ok, _bdot_nt(q, k_ref[qs, :]), MASK_VALUE)
            s_prev = jnp.where(jnp.logical_and(prev_ok, has_prev), _bdot_nt(q, k_ref[ps, :]), MASK_VALUE)
            m_old = m_ref[qs, :][:, 0:1]
            l_old = l_ref[qs, :][:, 0:1]
            m_new = jnp.maximum(m_old, jnp.maximum(jnp.max(s_cur, axis=-1, keepdims=True),
                                                   jnp.max(s_prev, axis=-1, keepdims=True)))
            alpha = jnp.exp(m_old - m_new)
            p_cur = jnp.exp(s_cur - m_new)
            p_prev = jnp.exp(s_prev - m_new)
            l_new = alpha * l_old + jnp.sum(p_cur, axis=-1, keepdims=True) + jnp.sum(p_prev, axis=-1, keepdims=True)
            acc = alpha * o_ref[qs, :] + _bdot(p_cur, v_ref[qs, :]) + _bdot(p_prev, v_ref[ps, :])
            o_ref[qs, :] = acc
            m_ref[qs, :] = jnp.broadcast_to(m_new, (blk, LANE))
            l_ref[qs, :] = jnp.broadcast_to(l_new, (blk, LANE))
            return carry

        lax.fori_loop(0, s_len // blk, unit, 0)

    for gi, (window, dil) in enumerate(DIL_PAIRS):
        assert window // dil == blk

        @pl.when(g == gi)
        def _(dil=dil):
            run_group(dil)

    @pl.when(g == len(DIL_PAIRS) - 1)
    def _():
        o_ref[...] = o_ref[...] / l_ref[...]


def dilated_branch(p3):
    b, s, _ = p3.shape
    ng = len(DIL_PAIRS)
    spec = lambda c0: pl.BlockSpec((None, s, DIL_HEAD_DIM), lambda i, j, g: (i, 0, c0 + g * DIL_HPG + j))
    return pl.pallas_call(
        _dil_kernel,
        grid=(b, DIL_HPG, ng),
        in_specs=[spec(C_QB), spec(C_KB), spec(C_VB)],
        out_specs=pl.BlockSpec((None, s, DIL_HEAD_DIM), lambda i, j, g: (i, 0, j)),
        out_shape=jax.ShapeDtypeStruct((b, s, BRANCH_W), F32),
        scratch_shapes=[pltpu.VMEM((s, LANE), F32), pltpu.VMEM((s, LANE), F32)],
        compiler_params=_cparams(("arbitrary", "arbitrary", "arbitrary")),
        name="dilated",
    )(p3, p3, p3)


DIFF_TQ = 256
DIFF_TK = 256


def _diff_kernel(lam_ref, q_ref, k_ref, v_ref, nw_ref, o_ref, *, lam_init):
    qi = pl.program_id(2)
    tq, tk, dh = DIFF_TQ, DIFF_TK, DIFF_HEAD_DIM
    scale = dh ** -0.5
    q = q_ref[...] * scale
    row = lax.broadcasted_iota(jnp.int32, (tq, tk), 0)
    col = lax.broadcasted_iota(jnp.int32, (tq, tk), 1)

    def kv_step(j, carry):
        ks = pl.ds(pl.multiple_of(j * tk, tk), tk)
        kblk = k_ref[ks, :]
        vblk = v_ref[ks, :].astype(BF16)
        causal = (j * tk + col) <= (qi * tq + row)
        out = []
        for t in range(2):
            m_old, l_old, acc = carry[t]
            sc = _bdot_nt(q[:, t * dh:(t + 1) * dh], kblk[:, t * dh:(t + 1) * dh])
            sc = jnp.where(causal, sc, MASK_VALUE)
            m_new = jnp.maximum(m_old, jnp.max(sc, axis=-1, keepdims=True))
            alpha = jnp.exp(m_old - m_new)
            p = jnp.exp(sc - m_new)
            l_new = alpha * l_old + jnp.sum(p, axis=-1, keepdims=True)
            acc = alpha * acc + jnp.dot(p.astype(BF16), vblk, preferred_element_type=F32)
            out.append((m_new, l_new, acc))
        return tuple(out)

    init = tuple((jnp.full((tq, 1), MASK_VALUE, F32), jnp.zeros((tq, 1), F32), jnp.zeros((tq, 2 * dh), F32))
                 for _ in range(2))
    n_kv = ((qi + 1) * tq + tk - 1) // tk
    (m0, l0, a0), (m1, l1, a1) = lax.fori_loop(0, n_kv, kv_step, init)

    lam_p = lam_ref[...]
    lam = (jnp.exp(jnp.sum(lam_p[0:1] * lam_p[1:2], axis=-1, keepdims=True))
           - jnp.exp(jnp.sum(lam_p[2:3] * lam_p[3:4], axis=-1, keepdims=True)) + lam_init)
    o = a0 / l0 - lam * (a1 / l1)
    o = o * lax.rsqrt(jnp.mean(o * o, axis=-1, keepdims=True) + NORM_EPS) * nw_ref[...]
    o_ref[...] = o * (1.0 - lam_init)


def diff_branch(p3, diff_lambda, subln_w, layer):
    b, s, _ = p3.shape
    lam_init = 0.8 - 0.6 * math.exp(-0.3 * layer)
    w = 2 * DIFF_HEAD_DIM
    return pl.pallas_call(
        functools.partial(_diff_kernel, lam_init=lam_init),
        grid=(b, DIFF_HEADS, s // DIFF_TQ),
        in_specs=[
            pl.BlockSpec((4, DIFF_HEAD_DIM), lambda i, h, t: (0, 0)),
            pl.BlockSpec((None, DIFF_TQ, w), lambda i, h, t: (i, t, C_QC + h)),
            pl.BlockSpec((None, s, w), lambda i, h, t: (i, 0, C_KC + h)),
            pl.BlockSpec((None, s, w), lambda i, h, t: (i, 0, C_VC + h)),
            pl.BlockSpec((1, w), lambda i, h, t: (0, 0)),
        ],
        out_specs=pl.BlockSpec((None, DIFF_TQ, w), lambda i, h, t: (i, t, h)),
        out_shape=jax.ShapeDtypeStruct((b, s, BRANCH_W), F32),
        compiler_params=_cparams(("arbitrary", "arbitrary", "arbitrary")),
        name="diff_attn",
    )(diff_lambda, p3, p3, p3, subln_w.reshape(1, w))


HG_LEVELS = 6


def _hgrn_tables():
    import numpy as np
    c = HG_CHUNK
    i = np.arange(c)[:, None]
    m = np.arange(c)[None, :]
    e_rows, f_rows, masks = [], [], []
    for lv in range(HG_LEVELS + 1):
        sz = 1 << lv
        bs = (i // sz) * sz
        be = bs + sz - 1
        e_rows.append(((m >= bs) & (m <= i)).astype(np.float32))
        f_rows.append(((m > i) & (m <= be)).astype(np.float32))
    j = m
    for lv in range(HG_LEVELS):
        sz = 1 << lv
        masks.append(((i // (2 * sz) == j // (2 * sz)) & (i // sz == j // sz + 1)).astype(np.float32))
    masks.append((i == j).astype(np.float32))
    sel = np.concatenate(e_rows + f_rows[1:], axis=0)
    return sel, np.stack(masks, axis=0)


def _hgrn_kernel(q_ref, f_ref, i_ref, g_ref, lb_ref, sel_ref, msk_ref, nw_ref, o_ref, state_ref, y_ref):
    c = pl.program_id(1)
    cs = HG_CHUNK
    dk, dv = HG_KEY_DIM, HG_VAL_DIM
    nl = HG_LEVELS

    @pl.when(c == 0)
    def _():
        state_ref[...] = jnp.zeros_like(state_ref)

    lb = lb_ref[...]
    f_gate = lb + (1.0 - lb) * jax.nn.sigmoid(f_ref[...])
    log_f = jnp.log(f_gate)
    k_in = 1.0 - f_gate
    q = _silu(q_ref[...])
    v = i_ref[...]

    hi = log_f.astype(BF16)
    lo = (log_f - hi.astype(F32)).astype(BF16)
    sel = sel_ref[...]
    ef = jnp.dot(sel, hi, preferred_element_type=F32) + jnp.dot(sel, lo, preferred_element_type=F32)
    dec = jnp.exp(ef)

    def e_fac(lv):
        return dec[lv * cs:(lv + 1) * cs]

    def f_fac(lv):
        r0 = (nl + lv) * cs
        return dec[r0:r0 + cs]

    for h in range(HG_HEADS):
        ks = slice(h * dk, (h + 1) * dk)
        vs = slice(h * dv, (h + 1) * dv)
        qh, kh, vh = q[:, ks], k_in[:, ks], v[:, vs]
        attn = msk_ref[nl] * _bdot_nt(qh, kh)
        for lv in range(nl):
            kf = kh if lv == 0 else kh * f_fac(lv)[:, ks]
            attn = attn + msk_ref[lv] * _bdot_nt(qh * e_fac(lv)[:, ks], kf)
        e_full = e_fac(nl)[:, ks]
        st = state_ref[h]
        o = _bdot(attn, vh) + _bdot_nt(qh * e_full, st)
        state_ref[h] = e_full[cs - 1:cs, :] * st + _bdot_tn(vh, kh * f_fac(nl)[:, ks])
        o = o * lax.rsqrt(jnp.mean(o * o, axis=-1, keepdims=True) + NORM_EPS) * nw_ref[...]
        y_ref[:, vs] = o
    o_ref[...] = y_ref[...] * _silu(g_ref[...])


def hgrn_branch(p3, lower_bound, norm_w):
    b, s, _ = p3.shape
    cs = HG_CHUNK
    w = HG_HEADS * HG_KEY_DIM
    sel, masks = _hgrn_tables()
    seg = lambda c0: pl.BlockSpec((None, cs, w), lambda i, c: (i, c, c0 * LANE // w))
    return pl.pallas_call(
        _hgrn_kernel,
        grid=(b, s // cs),
        in_specs=[
            seg(C_QD), seg(C_FD), seg(C_ID), seg(C_GD),
            pl.BlockSpec((1, w), lambda i, c: (0, 0)),
            pl.BlockSpec(sel.shape, lambda i, c: (0, 0)),
            pl.BlockSpec(masks.shape, lambda i, c: (0, 0, 0)),
            pl.BlockSpec((1, HG_VAL_DIM), lambda i, c: (0, 0)),
        ],
        out_specs=pl.BlockSpec((None, cs, w), lambda i, c: (i, c, 0)),
        out_shape=jax.ShapeDtypeStruct((b, s, BRANCH_W), F32),
        scratch_shapes=[pltpu.VMEM((HG_HEADS, HG_VAL_DIM, HG_KEY_DIM), F32), pltpu.VMEM((cs, w), F32)],
        compiler_params=_cparams(("arbitrary", "arbitrary")),
        name="hgrn2",
    )(p3, p3, p3, p3, lower_bound.reshape(1, w).astype(F32), jnp.asarray(sel, BF16), jnp.asarray(masks, F32),
      norm_w.reshape(1, HG_VAL_DIM))


MERGE_TM = 512


def _split_bf16(v):
    hi = v.astype(BF16)
    return hi, (v - hi.astype(F32)).astype(BF16)


def _first_index_of_max(vals, iota, n):
    top = jnp.max(vals, axis=0, keepdims=True)
    idx = jnp.min(jnp.where(vals == top, iota, n), axis=0, keepdims=True)
    return top, idx


def _merge_kernel(oa_ref, ob_ref, oc_ref, od_ref, gl_ref, x_ref, g1_ref, wm_ref, wo_ref,
                  nw_ref, sc_ref, sh_ref, wr_ref, rb_ref,
                  xo_ref, h2_ref, idx_ref, pos_ref, wt_ref, cnt_ref, carry_ref):
    step = pl.program_id(0)
    tm = x_ref.shape[0]
    d = D_MODEL

    acc = jnp.zeros((tm, d), F32)
    for n, o_ref in enumerate((oa_ref, ob_ref, oc_ref, od_ref)):
        proj = jnp.dot(o_ref[...].astype(BF16), wm_ref[n], preferred_element_type=F32)
        acc = acc + jax.nn.sigmoid(gl_ref[:, n * d:(n + 1) * d]) * proj
    mix = jnp.dot(acc.astype(BF16), wo_ref[...], preferred_element_type=F32)
    x_new = x_ref[...] + g1_ref[...] * mix
    xo_ref[...] = x_new

    y = x_new * lax.rsqrt(jnp.mean(x_new * x_new, axis=-1, keepdims=True) + NORM_EPS) * nw_ref[...]
    h2 = y * (1.0 + sc_ref[...]) + sh_ref[...]
    h2_ref[...] = h2

    h_hi, h_lo = _split_bf16(h2)
    w_hi, w_lo = _split_bf16(wr_ref[...])
    nt = lambda a, b: lax.dot_general(a, b, (((1,), (1,)), ((), ())), preferred_element_type=F32)
    logits = nt(w_hi, h_hi) + nt(w_hi, h_lo) + nt(w_lo, h_hi)
    scores = jax.nn.sigmoid(logits)
    sel = scores + rb_ref[...]

    ne, ng, pg = N_EXPERTS, N_EXPERT_GROUPS, EXPERTS_PER_GROUP
    iota_g = lax.broadcasted_iota(jnp.int32, (pg, tm), 0)
    best_score = None
    best_group = None
    for g in range(ng):
        xg = sel[g * pg:(g + 1) * pg, :]
        top1, i1 = _first_index_of_max(xg, iota_g, pg)
        top2 = jnp.max(jnp.where(iota_g == i1, -jnp.inf, xg), axis=0, keepdims=True)
        gs = top1 + top2
        if g == 0:
            best_score, best_group = gs, jnp.zeros((1, tm), jnp.int32)
        else:
            better = gs > best_score
            best_score = jnp.where(better, gs, best_score)
            best_group = jnp.where(better, g, best_group)

    iota_e = lax.broadcasted_iota(jnp.int32, (ne, tm), 0)
    masked = jnp.where(iota_e // pg == best_group, sel, MASK_VALUE)
    _, e1 = _first_index_of_max(masked, iota_e, ne)
    oh1 = iota_e == e1
    _, e2 = _first_index_of_max(jnp.where(oh1, MASK_VALUE, masked), iota_e, ne)
    oh2 = iota_e == e2
    w1 = jnp.sum(jnp.where(oh1, scores, 0.0), axis=0, keepdims=True)
    w2 = jnp.sum(jnp.where(oh2, scores, 0.0), axis=0, keepdims=True)
    wsum = w1 + w2
    w1 = w1 / wsum
    w2 = w2 / wsum

    @pl.when(step == 0)
    def _():
        carry_ref[...] = jnp.zeros_like(carry_ref)

    f1 = jnp.where(oh1, 1.0, 0.0)
    f2 = jnp.where(oh2, 1.0, 0.0)
    both = f1 + f2
    r_i = lax.broadcasted_iota(jnp.int32, (tm, tm), 0)
    c_i = lax.broadcasted_iota(jnp.int32, (tm, tm), 1)
    before = jnp.where(r_i < c_i, 1.0, 0.0).astype(BF16)
    rank = jnp.dot(both.astype(BF16), before, preferred_element_type=F32) + carry_ref[:, 0:1]
    p1 = jnp.sum(f1 * rank, axis=0, keepdims=True)
    p2 = jnp.sum(f2 * rank, axis=0, keepdims=True)
    carry_new = carry_ref[...] + jnp.sum(both, axis=1, keepdims=True)
    carry_ref[...] = carry_new
    cnt_ref[...] = carry_new.astype(jnp.int32)

    idx_ref[...] = jnp.concatenate([e1, e2], axis=0)
    pos_ref[...] = jnp.concatenate([p1, p2], axis=0).astype(jnp.int32)
    wt_rows = jnp.concatenate([w1, w2, jnp.zeros((LANE - 2, tm), F32)], axis=0)
    wt_ref[...] = wt_rows.T


def merge_and_route(o_a, o_b, o_c, o_d, p2, x2, g1, w_merge_bf, w_out_bf, norm_w, sc2, sh2, w_router_t, router_bias, seq):
    t, d = x2.shape
    tm = min(MERGE_TM, seq)
    per_b = seq // tm
    gw = N_BRANCH * d
    tok = lambda w: pl.BlockSpec((tm, w), lambda i: (i, 0))
    bat = pl.BlockSpec((None, 1, d), lambda i: (i // per_b, 0, 0))
    ne = N_EXPERTS
    return pl.pallas_call(
        _merge_kernel,
        grid=(t // tm,),
        in_specs=[
            tok(BRANCH_W), tok(BRANCH_W), tok(BRANCH_W), tok(BRANCH_W),
            pl.BlockSpec((tm, gw), lambda i: (i, C_GATE * LANE // gw)),
            tok(d), bat,
            pl.BlockSpec((N_BRANCH, BRANCH_W, d), lambda i: (0, 0, 0)),
            pl.BlockSpec((d, d), lambda i: (0, 0)),
            pl.BlockSpec((1, d), lambda i: (0, 0)),
            bat, bat,
            pl.BlockSpec((ne, d), lambda i: (0, 0)),
            pl.BlockSpec((ne, 1), lambda i: (0, 0)),
        ],
        out_specs=[
            tok(d), tok(d),
            pl.BlockSpec((TOP_K, tm), lambda i: (0, i)),
            pl.BlockSpec((TOP_K, tm), lambda i: (0, i)),
            tok(LANE),
            pl.BlockSpec((ne, LANE), lambda i: (0, 0)),
        ],
        out_shape=[
            jax.ShapeDtypeStruct((t, d), F32),
            jax.ShapeDtypeStruct((t, d), F32),
            jax.ShapeDtypeStruct((TOP_K, t), jnp.int32),
            jax.ShapeDtypeStruct((TOP_K, t), jnp.int32),
            jax.ShapeDtypeStruct((t, LANE), F32),
            jax.ShapeDtypeStruct((ne, LANE), jnp.int32),
        ],
        scratch_shapes=[pltpu.VMEM((ne, LANE), F32)],
        compiler_params=_cparams(("arbitrary",)),
        name="merge_route",
    )(o_a, o_b, o_c, o_d, p2, x2, g1, w_merge_bf, w_out_bf, norm_w.reshape(1, d), sc2, sh2,
      w_router_t, router_bias.reshape(ne, 1))


MOE_BM = 128


def _dispatch_kernel(size_ref, idx_ref, pos_ref, cnt_ref, rowtok_ref, dest_ref, blke_ref, nused_ref, pstart_ref):
    bm = MOE_BM
    n_tok = size_ref[0]
    n_rows = size_ref[1]
    n_blocks = size_ref[2]

    def expert_body(e, acc):
        pstart_ref[e] = acc
        nb = (cnt_ref[e] + bm - 1) // bm
        b0 = acc // bm

        def blk_body(i, carry):
            blke_ref[b0 + i] = e
            return carry

        lax.fori_loop(0, nb, blk_body, 0)
        return acc + nb * bm

    total = lax.fori_loop(0, N_EXPERTS, expert_body, jnp.int32(0))
    used = total // bm
    nused_ref[0] = used

    def tail_body(i, carry):
        blke_ref[i] = N_EXPERTS - 1
        return carry

    lax.fori_loop(used, n_blocks, tail_body, 0)

    def zero_body(i, carry):
        rowtok_ref[i] = 0
        return carry

    lax.fori_loop(0, n_rows, zero_body, 0)

    def assign_body(i, carry):
        dst = pstart_ref[idx_ref[i]] + pos_ref[i]
        dest_ref[i] = dst
        rowtok_ref[dst] = i % n_tok
        return carry

    lax.fori_loop(0, TOP_K * n_tok, assign_body, 0)


def moe_dispatch(idx, pos, counts):
    n_tok = idx.shape[1]
    n_rows = TOP_K * n_tok + N_EXPERTS * MOE_BM
    n_blocks = n_rows // MOE_BM
    smem = pl.BlockSpec(memory_space=pltpu.SMEM)
    sizes = jnp.array([n_tok, n_rows, n_blocks], jnp.int32)
    return pl.pallas_call(
        _dispatch_kernel,
        in_specs=[smem, smem, smem, smem],
        out_specs=[smem, smem, smem, smem],
        out_shape=[
            jax.ShapeDtypeStruct((n_rows,), jnp.int32),
            jax.ShapeDtypeStruct((TOP_K * n_tok,), jnp.int32),
            jax.ShapeDtypeStruct((n_blocks,), jnp.int32),
            jax.ShapeDtypeStruct((1,), jnp.int32),
        ],
        scratch_shapes=[pltpu.SMEM((N_EXPERTS,), jnp.int32)],
        name="moe_dispatch",
    )(sizes, idx.reshape(-1), pos.reshape(-1), counts[:, 0])


def _row_gather(src_hbm, tok_of_row, dst, sem, n):
    def body(r, carry):
        pltpu.make_async_copy(src_hbm.at[pl.ds(tok_of_row(r), 1), :], dst.at[pl.ds(r, 1), :], sem).start()
        return carry
    lax.fori_loop(0, n, body, 0, unroll=8)


def _row_gather_wait(src_hbm, dst, sem, n):
    def body(r, carry):
        pltpu.make_async_copy(src_hbm.at[pl.ds(0, 1), :], dst.at[pl.ds(r, 1), :], sem).wait()
        return carry
    lax.fori_loop(0, n, body, 0, unroll=8)


def _expert_kernel(blke_ref, nused_ref, rowtok_ref, h_hbm, wg_ref, wu_ref, wd_ref, y_ref,
                   buf_ref, sem_ref, wgb_ref, wub_ref, wdb_ref):
    b = pl.program_id(0)
    bm = MOE_BM
    used = nused_ref[0]
    slot = b % 2

    def start(blk, sl):
        _row_gather(h_hbm, lambda r: rowtok_ref[blk * bm + r], buf_ref.at[sl], sem_ref.at[sl], bm)

    @pl.when(b == 0)
    def _():
        start(0, 0)

    @pl.when(b + 1 < used)
    def _():
        start(b + 1, 1 - slot)

    changed = jnp.logical_or(b == 0, blke_ref[b] != blke_ref[jnp.maximum(b - 1, 0)])

    @pl.when(jnp.logical_and(b < used, changed))
    def _():
        wgb_ref[...] = wg_ref[...].astype(BF16)
        wub_ref[...] = wu_ref[...].astype(BF16)
        wdb_ref[...] = wd_ref[...].astype(BF16)

    @pl.when(b < used)
    def _():
        _row_gather_wait(h_hbm, buf_ref.at[slot], sem_ref.at[slot], bm)
        xb = buf_ref[slot].astype(BF16)
        hid = _silu(jnp.dot(xb, wgb_ref[...], preferred_element_type=F32)) * jnp.dot(xb, wub_ref[...], preferred_element_type=F32)
        y_ref[...] = jnp.dot(hid.astype(BF16), wdb_ref[...], preferred_element_type=F32)

    @pl.when(b >= used)
    def _():
        y_ref[...] = jnp.zeros_like(y_ref)


def moe_experts(h2, row_tok, blk_e, n_used, w_gate, w_up, w_down):
    t, d = h2.shape
    n_rows = row_tok.shape[0]
    bm = MOE_BM
    de = w_gate.shape[-1]
    grid_spec = pltpu.PrefetchScalarGridSpec(
        num_scalar_prefetch=3,
        grid=(n_rows // bm,),
        in_specs=[
            pl.BlockSpec(memory_space=pl.ANY),
            pl.BlockSpec((None, d, de), lambda b, be, nu, rt: (be[b], 0, 0)),
            pl.BlockSpec((None, d, de), lambda b, be, nu, rt: (be[b], 0, 0)),
            pl.BlockSpec((None, de, d), lambda b, be, nu, rt: (be[b], 0, 0)),
        ],
        out_specs=pl.BlockSpec((bm, d), lambda b, be, nu, rt: (b, 0)),
        scratch_shapes=[
            pltpu.VMEM((2, bm, d), F32),
            pltpu.SemaphoreType.DMA((2,)),
            pltpu.VMEM((d, de), BF16),
            pltpu.VMEM((d, de), BF16),
            pltpu.VMEM((de, d), BF16),
        ],
    )
    return pl.pallas_call(
        _expert_kernel,
        grid_spec=grid_spec,
        out_shape=jax.ShapeDtypeStruct((n_rows, d), F32),
        compiler_params=_cparams(("arbitrary",)),
        name="moe_experts",
    )(blk_e, n_used, row_tok, h2, w_gate, w_up, w_down)


COMB_TM = 256


def _combine_kernel(dest_ref, ys_hbm, x_ref, wt_ref, g2_ref, fw_ref, o_ref, buf_ref, sem_ref, *, n_tok, final_norm):
    i = pl.program_id(0)
    n = pl.num_programs(0)
    tm = x_ref.shape[0]
    slot = i % 2

    def start(tile, sl):
        for k in range(TOP_K):
            _row_gather(ys_hbm, lambda r: dest_ref[k * n_tok + tile * tm + r], buf_ref.at[sl, k], sem_ref.at[sl, k], tm)

    @pl.when(i == 0)
    def _():
        start(0, 0)

    @pl.when(i + 1 < n)
    def _():
        start(i + 1, 1 - slot)

    for k in range(TOP_K):
        _row_gather_wait(ys_hbm, buf_ref.at[slot, k], sem_ref.at[slot, k], tm)
    wt = wt_ref[...]
    moe = wt[:, 0:1] * buf_ref[slot, 0] + wt[:, 1:2] * buf_ref[slot, 1]
    out = x_ref[...] + g2_ref[...] * moe
    if final_norm:
        out = out * lax.rsqrt(jnp.mean(out * out, axis=-1, keepdims=True) + NORM_EPS) * fw_ref[...]
    o_ref[...] = out


def moe_combine(ys, dest, x2, wts, g2, final_w, seq, final_norm):
    t, d = x2.shape
    tm = min(COMB_TM, seq)
    per_b = seq // tm
    grid_spec = pltpu.PrefetchScalarGridSpec(
        num_scalar_prefetch=1,
        grid=(t // tm,),
        in_specs=[
            pl.BlockSpec(memory_space=pl.ANY),
            pl.BlockSpec((tm, d), lambda i, ds: (i, 0)),
            pl.BlockSpec((tm, LANE), lambda i, ds: (i, 0)),
            pl.BlockSpec((None, 1, d), lambda i, ds: (i // per_b, 0, 0)),
            pl.BlockSpec((1, d), lambda i, ds: (0, 0)),
        ],
        out_specs=pl.BlockSpec((tm, d), lambda i, ds: (i, 0)),
        scratch_shapes=[pltpu.VMEM((2, TOP_K, tm, d), F32), pltpu.SemaphoreType.DMA((2, TOP_K))],
    )
    return pl.pallas_call(
        functools.partial(_combine_kernel, n_tok=t, final_norm=final_norm),
        grid_spec=grid_spec,
        out_shape=jax.ShapeDtypeStruct((t, d), F32),
        compiler_params=_cparams(("arbitrary",)),
        name="moe_combine",
    )(dest, ys, x2, wts, g2, final_w.reshape(1, d))


def kernel(x, c, w_ada, b_ada, norm_mix_w, norm_ffn_w, w_in, conv_w, conv_b, ssd_dt_bias, ssd_a_log, ssd_d, ssd_norm_w, diff_lambda, diff_subln_w, hgrn_lb_logits, hgrn_norm_w, w_merge, w_out, w_router, router_bias, w_expert_gate, w_expert_up, w_expert_down, final_norm_w):
    b, s, d = x.shape
    t = b * s
    depth = w_in.shape[0]
    mod = ada_modulation(c, w_ada, b_ada)
    lb_p = jax.nn.softmax(hgrn_lb_logits.astype(F32), axis=0)
    lower_bounds = jnp.cumsum(lb_p, axis=0) - lb_p[0]
    w_router_t = w_router.T
    x2 = x.reshape(t, d)
    for l in range(depth):
        sh1, sc1, g1, sh2, sc2, g2 = [mod[l, :, i * d:(i + 1) * d].reshape(b, 1, d) for i in range(6)]
        p2 = in_projection(x2, norm_mix_w[l], sc1, sh1, pad_w_in(w_in[l]), s)
        p3 = p2.reshape(b, s, -1)
        o_a = ssd_branch(p3, conv_w[l], conv_b[l], ssd_dt_bias[l], ssd_a_log[l], ssd_d[l], ssd_norm_w[l])
        o_b = dilated_branch(p3)
        o_c = diff_branch(p3, diff_lambda[l], diff_subln_w[l], l)
        o_d = hgrn_branch(p3, lower_bounds[l], hgrn_norm_w[l])
        flat = lambda o: o.reshape(t, BRANCH_W)
        x_mid, h2, idx, pos, wts, counts = merge_and_route(
            flat(o_a), flat(o_b), flat(o_c), flat(o_d), p2, x2, g1, w_merge[l].astype(BF16), w_out[l].astype(BF16),
            norm_ffn_w[l], sc2, sh2, w_router_t, router_bias, s)
        row_tok, dest, blk_e, n_used = moe_dispatch(idx, pos, counts)
        ys = moe_experts(h2, row_tok, blk_e, n_used, w_expert_gate[l], w_expert_up[l], w_expert_down[l])
        x2 = moe_combine(ys, dest, x_mid, wts, g2, final_norm_w, s, final_norm=(l == depth - 1))
    return x2.reshape(b, s, d)
```

```python
import functools
import math

import jax
import jax.numpy as jnp
from jax import lax
from jax.experimental import pallas as pl
from jax.experimental.pallas import tpu as pltpu

F32 = jnp.float32
BF16 = jnp.bfloat16

D_MODEL = 1024
DEPTH = 2
N_BRANCH = 4
BRANCH_W = 512

SSD_D_INNER = 512
SSD_HEAD_DIM = 64
SSD_HEADS = 8
SSD_GROUPS = 2
SSD_HPG = 4
SSD_STATE = 64
SSD_CONV = 4
SSD_CHUNK = 128
SSD_CONV_CH = 768

DIL_PAIRS = ((128, 1), (512, 4), (2048, 16))
DIL_HPG = 4
DIL_HEAD_DIM = 128
DIL_HEADS = 12
DIL_BLOCK = 128

DIFF_HEADS = 4
DIFF_HEAD_DIM = 64

HG_HEADS = 8
HG_KEY_DIM = 64
HG_VAL_DIM = 64
HG_CHUNK = 64

N_EXPERTS = 64
N_EXPERT_GROUPS = 8
EXPERTS_PER_GROUP = 8
TOP_K = 2
D_EXPERT = 256

NORM_EPS = 1e-6
MASK_VALUE = -1e30

LANE = 128
VMEM_LIMIT = 48 * 1024 * 1024

C_GATE = 0
C_Z = 32
C_XBC = 36
C_DT = 42
C_QB = 44
C_KB = 56
C_VB = 68
C_QC = 80
C_KC = 84
C_VC = 88
C_QD = 92
C_FD = 96
C_ID = 100
C_GD = 104
N_COLB = 108
D_IN_PAD = N_COLB * LANE
DT_PAD = 2 * LANE


def _cparams(sem):
    return pltpu.CompilerParams(dimension_semantics=sem, vmem_limit_bytes=VMEM_LIMIT)


def _silu(v):
    return v * jax.nn.sigmoid(v)


def _bdot(a, b):
    return jnp.dot(a.astype(BF16), b.astype(BF16), preferred_element_type=F32)


def _bdot_nt(a, b):
    return lax.dot_general(a.astype(BF16), b.astype(BF16), (((1,), (1,)), ((), ())),
                           preferred_element_type=F32)


def _bdot_tn(a, b):
    return lax.dot_general(a.astype(BF16), b.astype(BF16), (((0,), (0,)), ((), ())),
                           preferred_element_type=F32)


def _exact_rows_dot(m01, v):
    hi = v.astype(BF16)
    r1 = v - hi.astype(F32)
    mid = r1.astype(BF16)
    lo = (r1 - mid.astype(F32)).astype(BF16)
    m = m01.astype(BF16)
    return (jnp.dot(m, hi, preferred_element_type=F32) + jnp.dot(m, mid, preferred_element_type=F32)
            + jnp.dot(m, lo, preferred_element_type=F32))


def _ada_kernel(c_ref, w_ref, b_ref, o_ref):
    o_ref[...] = _bdot(_silu(c_ref[...]), w_ref[...]) + b_ref[...]


def ada_modulation(c, w_ada, b_ada):
    depth, d, n = w_ada.shape
    b = c.shape[0]
    bp = 8
    c_pad = jnp.zeros((bp, d), F32).at[:b].set(c)
    tn = 1536
    out = pl.pallas_call(
        _ada_kernel,
        grid=(depth, n // tn),
        in_specs=[
            pl.BlockSpec((bp, d), lambda l, j: (0, 0)),
            pl.BlockSpec((None, d, tn), lambda l, j: (l, 0, j)),
            pl.BlockSpec((None, 1, tn), lambda l, j: (l, 0, j)),
        ],
        out_specs=pl.BlockSpec((None, bp, tn), lambda l, j: (l, 0, j)),
        out_shape=jax.ShapeDtypeStruct((depth, bp, n), F32),
        compiler_params=_cparams(("arbitrary", "arbitrary")),
        name="ada_mod",
    )(c_pad, w_ada, b_ada.reshape(depth, 1, n))
    return out[:, :b]


INPROJ_TM = 2048
INPROJ_TN = 512


def _inproj_kernel(x_ref, nw_ref, sc_ref, sh_ref, w_ref, o_ref, h_ref):
    @pl.when(pl.program_id(1) == 0)
    def _():
        x = x_ref[...]
        y = x * lax.rsqrt(jnp.mean(x * x, axis=-1, keepdims=True) + NORM_EPS) * nw_ref[...]
        h_ref[...] = (y * (1.0 + sc_ref[...]) + sh_ref[...]).astype(BF16)

    o_ref[...] = jnp.dot(h_ref[...], w_ref[...], preferred_element_type=F32).astype(o_ref.dtype)


def in_projection(x2, norm_w, scale, shift, w_pad, seq):
    t, d = x2.shape
    n = w_pad.shape[1]
    tm = min(INPROJ_TM, seq)
    tn = INPROJ_TN
    per_b = seq // tm
    return pl.pallas_call(
        _inproj_kernel,
        grid=(t // tm, n // tn),
        in_specs=[
            pl.BlockSpec((tm, d), lambda i, j: (i, 0)),
            pl.BlockSpec((1, d), lambda i, j: (0, 0)),
            pl.BlockSpec((None, 1, d), lambda i, j: (i // per_b, 0, 0)),
            pl.BlockSpec((None, 1, d), lambda i, j: (i // per_b, 0, 0)),
            pl.BlockSpec((d, tn), lambda i, j: (0, j)),
        ],
        out_specs=pl.BlockSpec((tm, tn), lambda i, j: (i, j)),
        out_shape=jax.ShapeDtypeStruct((t, n), BF16),
        scratch_shapes=[pltpu.VMEM((tm, d), BF16)],
        compiler_params=_cparams(("arbitrary", "arbitrary")),
        name="in_proj",
    )(x2, norm_w.reshape(1, d), scale, shift, w_pad)


def pad_w_in(w_in_l):
    d = w_in_l.shape[0]
    o_dt = SSD_D_INNER + SSD_CONV_CH
    o_gate = w_in_l.shape[1] - N_BRANCH * D_MODEL
    return jnp.concatenate(
        [w_in_l[:, o_gate:], w_in_l[:, :o_dt + SSD_HEADS], jnp.zeros((d, DT_PAD - SSD_HEADS), w_in_l.dtype),
         w_in_l[:, o_dt + SSD_HEADS:o_gate]], axis=1).astype(BF16)


def _ssd_kernel(z_ref, xbc_ref, dt_ref, cw_ref, cb_ref, dtb_ref, alog_ref, dsk_ref, nw_ref, o_ref,
                xpad_ref, state_ref, y_ref):
    c = pl.program_id(1)
    L = SSD_CHUNK
    halo = 8

    @pl.when(c == 0)
    def _():
        xpad_ref[0:halo, :] = jnp.zeros((halo, SSD_CONV_CH), F32)
        state_ref[...] = jnp.zeros_like(state_ref)

    @pl.when(c > 0)
    def _():
        xpad_ref[0:halo, :] = xpad_ref[L:L + halo, :]

    z = z_ref[...].astype(F32)
    xpad_ref[halo:halo + L, :] = xbc_ref[...].astype(F32)

    conv = cb_ref[...]
    for j in range(SSD_CONV):
        off = halo - (SSD_CONV - 1) + j
        conv = conv + cw_ref[j:j + 1, :] * xpad_ref[off:off + L, :]
    xc = _silu(conv)
    xs = xc[:, 0:SSD_D_INNER]
    ns = SSD_GROUPS * SSD_STATE
    bm = xc[:, SSD_D_INNER:SSD_D_INNER + ns]
    cm = xc[:, SSD_D_INNER + ns:SSD_D_INNER + 2 * ns]

    dt = jax.nn.softplus(dt_ref[:, 0:LANE].astype(F32) + dtb_ref[...])
    a = -jnp.exp(alog_ref[...])
    adt = dt * a
    row = lax.broadcasted_iota(jnp.int32, (L, L), 0)
    col = lax.broadcasted_iota(jnp.int32, (L, L), 1)
    tril = row >= col
    a_cs = _exact_rows_dot(jnp.where(tril, 1.0, 0.0), adt)
    a_cs_t = a_cs.T
    tot = a_cs[L - 1:L, :]

    cbs = [_bdot_nt(cm[:, g * SSD_STATE:(g + 1) * SSD_STATE], bm[:, g * SSD_STATE:(g + 1) * SSD_STATE])
           for g in range(SSD_GROUPS)]
    for e in range(SSD_HEADS):
        g = e // SSD_HPG
        cs_col = a_cs[:, e:e + 1]
        cs_row = a_cs_t[e:e + 1, :]
        lmat = jnp.exp(jnp.where(tril, cs_col - cs_row, MASK_VALUE))
        xe = xs[:, e * SSD_HEAD_DIM:(e + 1) * SSD_HEAD_DIM]
        xdt = xe * dt[:, e:e + 1]
        bg = bm[:, g * SSD_STATE:(g + 1) * SSD_STATE]
        cg = cm[:, g * SSD_STATE:(g + 1) * SSD_STATE]
        st = state_ref[e]
        y = _bdot(cbs[g] * lmat, xdt)
        y = y + _bdot(cg, st) * jnp.exp(cs_col)
        tot_e = tot[:, e:e + 1]
        state_ref[e] = jnp.exp(tot_e) * st + _bdot_tn(bg * jnp.exp(tot_e - cs_col), xdt)
        y_ref[:, e * SSD_HEAD_DIM:(e + 1) * SSD_HEAD_DIM] = y + dsk_ref[:, e:e + 1] * xe

    yz = y_ref[...] * _silu(z)
    o_ref[...] = yz * lax.rsqrt(jnp.mean(yz * yz, axis=-1, keepdims=True) + NORM_EPS) * nw_ref[...]


def _pad_lanes(v, n=LANE):
    return jnp.zeros((1, n), F32).at[0, :v.shape[0]].set(v.astype(F32))


def ssd_branch(p3, conv_w, conv_b, dt_bias, a_log, d_skip, norm_w):
    b, s, _ = p3.shape
    L = SSD_CHUNK
    const = lambda shape: pl.BlockSpec(shape, lambda i, c: (0, 0))
    return pl.pallas_call(
        _ssd_kernel,
        grid=(b, s // L),
        in_specs=[
            pl.BlockSpec((None, L, SSD_D_INNER), lambda i, c: (i, c, C_Z * LANE // SSD_D_INNER)),
            pl.BlockSpec((None, L, SSD_CONV_CH), lambda i, c: (i, c, C_XBC * LANE // SSD_CONV_CH)),
            pl.BlockSpec((None, L, DT_PAD), lambda i, c: (i, c, C_DT * LANE // DT_PAD)),
            const((SSD_CONV, SSD_CONV_CH)),
            const((1, SSD_CONV_CH)),
            const((1, LANE)),
            const((1, LANE)),
            const((1, LANE)),
            const((1, SSD_D_INNER)),
        ],
        out_specs=pl.BlockSpec((None, L, SSD_D_INNER), lambda i, c: (i, c, 0)),
        out_shape=jax.ShapeDtypeStruct((b, s, SSD_D_INNER), F32),
        scratch_shapes=[
            pltpu.VMEM((L + 8, SSD_CONV_CH), F32),
            pltpu.VMEM((SSD_HEADS, SSD_STATE, SSD_HEAD_DIM), F32),
            pltpu.VMEM((L, SSD_D_INNER), F32),
        ],
        compiler_params=_cparams(("arbitrary", "arbitrary")),
        name="ssd",
    )(p3, p3, p3, conv_w, conv_b.reshape(1, -1), _pad_lanes(dt_bias), _pad_lanes(a_log), _pad_lanes(d_skip),
      norm_w.reshape(1, -1))


DIL_UNROLL = 4


def _dil_kernel(q_ref, k_ref, v_ref, o_ref, m_ref, l_ref, tmp_ref, qd_ref, kd_ref, vd_ref):
    g = pl.program_id(2)
    s_len = q_ref.shape[0]
    blk = DIL_BLOCK
    scale = DIL_HEAD_DIM ** -0.5

    @pl.when(g == 0)
    def _():
        m_ref[...] = jnp.full_like(m_ref, MASK_VALUE)
        l_ref[...] = jnp.zeros_like(l_ref)
        o_ref[...] = jnp.zeros_like(o_ref)

    def run_group(dil):
        sub = s_len // dil
        nblk = sub // blk

        if dil == 1:
            qd_ref[...] = (q_ref[...].astype(F32) * scale).astype(BF16)
            kd, vd = k_ref, v_ref
        else:
            for src, dst, mul in ((q_ref, qd_ref, scale), (k_ref, kd_ref, None), (v_ref, vd_ref, None)):
                x = src[...].astype(F32)
                tmp_ref[...] = x if mul is None else x * mul

                def gather(r, carry, dst=dst):
                    dst[pl.ds(pl.multiple_of(r * sub, blk), sub), :] = tmp_ref[pl.ds(r, sub, stride=dil), :].astype(BF16)
                    return carry

                lax.fori_loop(0, dil, gather, 0)
            kd, vd = kd_ref, vd_ref

        def unit(r, b, nkey):
            row0 = pl.multiple_of(r * sub + b * blk, blk)
            ks = pl.ds(pl.multiple_of(row0 - (nkey - blk), blk), nkey)
            base = r + dil * blk * b
            acc_rows = pl.ds(pl.multiple_of(base, blk), blk) if dil == 1 else pl.ds(base, blk, stride=dil)
            i = lax.broadcasted_iota(jnp.int32, (blk, nkey), 0) + (nkey - blk)
            j = lax.broadcasted_iota(jnp.int32, (blk, nkey), 1)
            ok = jnp.logical_and(j <= i, j >= i - blk)
            sc = lax.dot_general(qd_ref[pl.ds(row0, blk), :], kd[ks, :], (((1,), (1,)), ((), ())),
                                 preferred_element_type=F32)
            sc = jnp.where(ok, sc, MASK_VALUE)
            m_old = m_ref[acc_rows, :][:, 0:1]
            m_new = jnp.maximum(m_old, jnp.max(sc, axis=-1, keepdims=True))
            alpha = jnp.exp(m_old - m_new)
            p = jnp.exp(sc - m_new).astype(BF16)
            v_ext = jnp.concatenate([vd[ks, :], jnp.ones((nkey, LANE), BF16)], axis=1)
            pv = jnp.dot(p, v_ext, preferred_element_type=F32)
            o_ref[acc_rows, :] = alpha * o_ref[acc_rows, :] + pv[:, :LANE]
            l_ref[acc_rows, :] = alpha * l_ref[acc_rows, :] + pv[:, LANE:]
            m_ref[acc_rows, :] = jnp.broadcast_to(m_new, (blk, LANE))

        def first_unit(r, carry):
            unit(r, 0, blk)
            return carry

        def later_unit(u, carry):
            unit(u // (nblk - 1), 1 + u % (nblk - 1), 2 * blk)
            return carry

        lax.fori_loop(0, dil, first_unit, 0, unroll=min(dil, DIL_UNROLL))
        if nblk > 1:
            lax.fori_loop(0, dil * (nblk - 1), later_unit, 0, unroll=DIL_UNROLL)

    for gi, (window, dil) in enumerate(DIL_PAIRS):
        assert window // dil == blk

        @pl.when(g == gi)
        def _(dil=dil):
            run_group(dil)

    @pl.when(g == len(DIL_PAIRS) - 1)
    def _():
        o_ref[...] = o_ref[...] / l_ref[...]


def dilated_branch(p3):
    b, s, _ = p3.shape
    ng = len(DIL_PAIRS)
    spec = lambda c0: pl.BlockSpec((None, s, DIL_HEAD_DIM), lambda i, j, g: (i, 0, c0 + g * DIL_HPG + j))
    return pl.pallas_call(
        _dil_kernel,
        grid=(b, DIL_HPG, ng),
        in_specs=[spec(C_QB), spec(C_KB), spec(C_VB)],
        out_specs=pl.BlockSpec((None, s, DIL_HEAD_DIM), lambda i, j, g: (i, 0, j)),
        out_shape=jax.ShapeDtypeStruct((b, s, BRANCH_W), F32),
        scratch_shapes=[pltpu.VMEM((s, LANE), F32), pltpu.VMEM((s, LANE), F32), pltpu.VMEM((s, LANE), F32),
                        pltpu.VMEM((s, LANE), BF16), pltpu.VMEM((s, LANE), BF16), pltpu.VMEM((s, LANE), BF16)],
        compiler_params=_cparams(("arbitrary", "arbitrary", "arbitrary")),
        name="dilated",
    )(p3, p3, p3)


DIFF_TQ = 512
DIFF_TK = 1024
LOG2E = 1.4426950408889634


DIFF_ONES = 16


def _diff_kernel(lam_ref, q_ref, k_ref, v_ref, nw_ref, o_ref, vt_ref, m_ref, acc_ref, *, lam_init):
    qi = pl.program_id(2)
    tq, tk, dh = DIFF_TQ, DIFF_TK, DIFF_HEAD_DIM
    w = 2 * dh
    s_len = k_ref.shape[0]

    @pl.when(qi == 0)
    def _():
        for c0 in range(0, s_len, tk):
            vt_ref[0:w, c0:c0 + tk] = v_ref[c0:c0 + tk, :].astype(F32).T.astype(BF16)
        vt_ref[w:w + DIFF_ONES, :] = jnp.ones((DIFF_ONES, s_len), BF16)

    q_t = (q_ref[...].astype(F32) * (dh ** -0.5 * LOG2E)).T
    half = lax.broadcasted_iota(jnp.int32, (w, tq), 0) < dh
    q_sel = [jnp.where(half, q_t, 0.0).astype(BF16), jnp.where(half, 0.0, q_t).astype(BF16)]
    for t in range(2):
        m_ref[t] = jnp.full((8, tq), MASK_VALUE, F32)
        acc_ref[t] = jnp.zeros((w + DIFF_ONES, tq), F32)

    def scores(kstart, width, masked):
        kb = k_ref[pl.ds(kstart, width), :]
        out = [jnp.dot(kb, q_sel[t], preferred_element_type=F32) for t in range(2)]
        if masked:
            row = lax.broadcasted_iota(jnp.int32, (width, tq), 0)
            col = lax.broadcasted_iota(jnp.int32, (width, tq), 1)
            out = [jnp.where(row <= col, sc, MASK_VALUE) for sc in out]
        return out

    def over_causal_blocks(block):
        def main_step(j, carry):
            block(pl.multiple_of(j * tk, tk), tk, False)
            return carry

        per = tk // tq
        lax.fori_loop(0, qi // per, main_step, 0)
        for r in range(1, per):
            @pl.when(qi % per >= r)
            def _(r=r):
                block(pl.multiple_of((qi // per) * tk + (r - 1) * tq, tq), tq, False)
        block(pl.multiple_of(qi * tq, tq), tq, True)

    def max_block(kstart, width, masked):
        for t, sc in enumerate(scores(kstart, width, masked)):
            m_ref[t] = jnp.maximum(m_ref[t], jnp.max(sc.reshape(width // 8, 8, tq), axis=0))

    over_causal_blocks(max_block)
    m_fin = [jnp.max(m_ref[t], axis=0, keepdims=True) for t in range(2)]

    def acc_block(kstart, width, masked):
        vb = vt_ref[:, pl.ds(kstart, width)]
        for t, sc in enumerate(scores(kstart, width, masked)):
            p = jnp.exp2(sc - m_fin[t]).astype(BF16)
            acc_ref[t] += jnp.dot(vb, p, preferred_element_type=F32)

    over_causal_blocks(acc_block)

    lam_p = lam_ref[...]
    lam = (jnp.exp(jnp.sum(lam_p[0:1] * lam_p[1:2], axis=-1, keepdims=True))
           - jnp.exp(jnp.sum(lam_p[2:3] * lam_p[3:4], axis=-1, keepdims=True)) + lam_init)
    a0, a1 = acc_ref[0], acc_ref[1]
    o_t = a0[0:w] / a0[w:w + 1] - lam * (a1[0:w] / a1[w:w + 1])
    o = o_t.T
    o = o * lax.rsqrt(jnp.mean(o * o, axis=-1, keepdims=True) + NORM_EPS) * nw_ref[...]
    o_ref[...] = o * (1.0 - lam_init)


def diff_branch(p3, diff_lambda, subln_w, layer):
    b, s, _ = p3.shape
    lam_init = 0.8 - 0.6 * math.exp(-0.3 * layer)
    w = 2 * DIFF_HEAD_DIM
    return pl.pallas_call(
        functools.partial(_diff_kernel, lam_init=lam_init),
        grid=(b, DIFF_HEADS, s // DIFF_TQ),
        in_specs=[
            pl.BlockSpec((4, DIFF_HEAD_DIM), lambda i, h, t: (0, 0)),
            pl.BlockSpec((None, DIFF_TQ, w), lambda i, h, t: (i, t, C_QC + h)),
            pl.BlockSpec((None, s, w), lambda i, h, t: (i, 0, C_KC + h)),
            pl.BlockSpec((None, s, w), lambda i, h, t: (i, 0, C_VC + h)),
            pl.BlockSpec((1, w), lambda i, h, t: (0, 0)),
        ],
        out_specs=pl.BlockSpec((None, DIFF_TQ, w), lambda i, h, t: (i, t, h)),
        out_shape=jax.ShapeDtypeStruct((b, s, BRANCH_W), F32),
        scratch_shapes=[pltpu.VMEM((w + DIFF_ONES, s), BF16), pltpu.VMEM((2, 8, DIFF_TQ), F32),
                        pltpu.VMEM((2, w + DIFF_ONES, DIFF_TQ), F32)],
        compiler_params=_cparams(("arbitrary", "arbitrary", "arbitrary")),
        name="diff_attn",
    )(diff_lambda, p3, p3, p3, subln_w.reshape(1, w))


HG_LEVELS = 6


def _hgrn_tables():
    import numpy as np
    c = HG_CHUNK
    i = np.arange(c)[:, None]
    m = np.arange(c)[None, :]
    e_rows, f_rows, masks = [], [], []
    for lv in range(HG_LEVELS + 1):
        sz = 1 << lv
        bs = (i // sz) * sz
        be = bs + sz - 1
        e_rows.append(((m >= bs) & (m <= i)).astype(np.float32))
        f_rows.append(((m > i) & (m <= be)).astype(np.float32))
    j = m
    for lv in range(HG_LEVELS):
        sz = 1 << lv
        masks.append(((i // (2 * sz) == j // (2 * sz)) & (i // sz == j // sz + 1)).astype(np.float32))
    masks.append((i == j).astype(np.float32))
    sel = np.concatenate(e_rows + f_rows[1:], axis=0)
    return sel, np.stack(masks, axis=0)


def _hgrn_kernel(q_ref, f_ref, i_ref, g_ref, lb_ref, sel_ref, msk_ref, nw_ref, o_ref, state_ref, y_ref):
    c = pl.program_id(1)
    cs = HG_CHUNK
    dk, dv = HG_KEY_DIM, HG_VAL_DIM
    nl = HG_LEVELS

    @pl.when(c == 0)
    def _():
        state_ref[...] = jnp.zeros_like(state_ref)

    lb = lb_ref[...]
    f_gate = lb + (1.0 - lb) * jax.nn.sigmoid(f_ref[...].astype(F32))
    log_f = jnp.log(f_gate)
    k_in = 1.0 - f_gate
    q = _silu(q_ref[...].astype(F32))
    v = i_ref[...].astype(F32)

    hi = log_f.astype(BF16)
    lo = (log_f - hi.astype(F32)).astype(BF16)
    sel = sel_ref[...]
    ef = jnp.dot(sel, hi, preferred_element_type=F32) + jnp.dot(sel, lo, preferred_element_type=F32)
    dec = jnp.exp(ef)

    def e_fac(lv):
        return dec[lv * cs:(lv + 1) * cs]

    def f_fac(lv):
        r0 = (nl + lv) * cs
        return dec[r0:r0 + cs]

    for h in range(HG_HEADS):
        ks = slice(h * dk, (h + 1) * dk)
        vs = slice(h * dv, (h + 1) * dv)
        qh, kh, vh = q[:, ks], k_in[:, ks], v[:, vs]
        attn = msk_ref[nl] * _bdot_nt(qh, kh)
        for lv in range(nl):
            kf = kh if lv == 0 else kh * f_fac(lv)[:, ks]
            attn = attn + msk_ref[lv] * _bdot_nt(qh * e_fac(lv)[:, ks], kf)
        e_full = e_fac(nl)[:, ks]
        st = state_ref[h]
        o = _bdot(attn, vh) + _bdot_nt(qh * e_full, st)
        state_ref[h] = e_full[cs - 1:cs, :] * st + _bdot_tn(vh, kh * f_fac(nl)[:, ks])
        o = o * lax.rsqrt(jnp.mean(o * o, axis=-1, keepdims=True) + NORM_EPS) * nw_ref[...]
        y_ref[:, vs] = o
    o_ref[...] = y_ref[...] * _silu(g_ref[...].astype(F32))


def hgrn_branch(p3, lower_bound, norm_w):
    b, s, _ = p3.shape
    cs = HG_CHUNK
    w = HG_HEADS * HG_KEY_DIM
    sel, masks = _hgrn_tables()
    seg = lambda c0: pl.BlockSpec((None, cs, w), lambda i, c: (i, c, c0 * LANE // w))
    return pl.pallas_call(
        _hgrn_kernel,
        grid=(b, s // cs),
        in_specs=[
            seg(C_QD), seg(C_FD), seg(C_ID), seg(C_GD),
            pl.BlockSpec((1, w), lambda i, c: (0, 0)),
            pl.BlockSpec(sel.shape, lambda i, c: (0, 0)),
            pl.BlockSpec(masks.shape, lambda i, c: (0, 0, 0)),
            pl.BlockSpec((1, HG_VAL_DIM), lambda i, c: (0, 0)),
        ],
        out_specs=pl.BlockSpec((None, cs, w), lambda i, c: (i, c, 0)),
        out_shape=jax.ShapeDtypeStruct((b, s, BRANCH_W), F32),
        scratch_shapes=[pltpu.VMEM((HG_HEADS, HG_VAL_DIM, HG_KEY_DIM), F32), pltpu.VMEM((cs, w), F32)],
        compiler_params=_cparams(("arbitrary", "arbitrary")),
        name="hgrn2",
    )(p3, p3, p3, p3, lower_bound.reshape(1, w).astype(F32), jnp.asarray(sel, BF16), jnp.asarray(masks, F32),
      norm_w.reshape(1, HG_VAL_DIM))


MERGE_TM = 512


def _split_bf16(v):
    hi = v.astype(BF16)
    return hi, (v - hi.astype(F32)).astype(BF16)


def _first_index_of_max(vals, iota, n):
    top = jnp.max(vals, axis=0, keepdims=True)
    idx = jnp.min(jnp.where(vals == top, iota, n), axis=0, keepdims=True)
    return top, idx


def _merge_kernel(oa_ref, ob_ref, oc_ref, od_ref, gl_ref, x_ref, g1_ref, wm_ref, wo_ref,
                  nw_ref, sc_ref, sh_ref, wr_ref, rb_ref,
                  xo_ref, h2_ref, idx_ref, pos_ref, wt_ref, cnt_ref, carry_ref):
    step = pl.program_id(0)
    tm = x_ref.shape[0]
    d = D_MODEL

    acc = jnp.zeros((tm, d), F32)
    for n, o_ref in enumerate((oa_ref, ob_ref, oc_ref, od_ref)):
        proj = jnp.dot(o_ref[...].astype(BF16), wm_ref[n], preferred_element_type=F32)
        acc = acc + jax.nn.sigmoid(gl_ref[:, n * d:(n + 1) * d].astype(F32)) * proj
    mix = jnp.dot(acc.astype(BF16), wo_ref[...], preferred_element_type=F32)
    x_new = x_ref[...] + g1_ref[...] * mix
    xo_ref[...] = x_new

    y = x_new * lax.rsqrt(jnp.mean(x_new * x_new, axis=-1, keepdims=True) + NORM_EPS) * nw_ref[...]
    h2 = y * (1.0 + sc_ref[...]) + sh_ref[...]
    h2_ref[...] = h2

    h_hi, h_lo = _split_bf16(h2)
    w_hi, w_lo = _split_bf16(wr_ref[...])
    nt = lambda a, b: lax.dot_general(a, b, (((1,), (1,)), ((), ())), preferred_element_type=F32)
    logits = nt(w_hi, h_hi) + nt(w_hi, h_lo) + nt(w_lo, h_hi)
    scores = jax.nn.sigmoid(logits)
    sel = scores + rb_ref[...]

    ne, ng, pg = N_EXPERTS, N_EXPERT_GROUPS, EXPERTS_PER_GROUP
    iota_g = lax.broadcasted_iota(jnp.int32, (pg, tm), 0)
    best_score = None
    best_group = None
    for g in range(ng):
        xg = sel[g * pg:(g + 1) * pg, :]
        top1, i1 = _first_index_of_max(xg, iota_g, pg)
        top2 = jnp.max(jnp.where(iota_g == i1, -jnp.inf, xg), axis=0, keepdims=True)
        gs = top1 + top2
        if g == 0:
            best_score, best_group = gs, jnp.zeros((1, tm), jnp.int32)
        else:
            better = gs > best_score
            best_score = jnp.where(better, gs, best_score)
            best_group = jnp.where(better, g, best_group)

    iota_e = lax.broadcasted_iota(jnp.int32, (ne, tm), 0)
    masked = jnp.where(iota_e // pg == best_group, sel, MASK_VALUE)
    _, e1 = _first_index_of_max(masked, iota_e, ne)
    oh1 = iota_e == e1
    _, e2 = _first_index_of_max(jnp.where(oh1, MASK_VALUE, masked), iota_e, ne)
    oh2 = iota_e == e2
    w1 = jnp.sum(jnp.where(oh1, scores, 0.0), axis=0, keepdims=True)
    w2 = jnp.sum(jnp.where(oh2, scores, 0.0), axis=0, keepdims=True)
    wsum = w1 + w2
    w1 = w1 / wsum
    w2 = w2 / wsum

    @pl.when(step == 0)
    def _():
        carry_ref[...] = jnp.zeros_like(carry_ref)

    f1 = jnp.where(oh1, 1.0, 0.0)
    f2 = jnp.where(oh2, 1.0, 0.0)
    both = f1 + f2
    r_i = lax.broadcasted_iota(jnp.int32, (tm, tm), 0)
    c_i = lax.broadcasted_iota(jnp.int32, (tm, tm), 1)
    before = jnp.where(r_i < c_i, 1.0, 0.0).astype(BF16)
    rank = jnp.dot(both.astype(BF16), before, preferred_element_type=F32) + carry_ref[:, 0:1]
    p1 = jnp.sum(f1 * rank, axis=0, keepdims=True)
    p2 = jnp.sum(f2 * rank, axis=0, keepdims=True)
    carry_new = carry_ref[...] + jnp.sum(both, axis=1, keepdims=True)
    carry_ref[...] = carry_new
    cnt_ref[...] = carry_new.astype(jnp.int32)

    idx_ref[...] = jnp.concatenate([e1, e2], axis=0)
    pos_ref[...] = jnp.concatenate([p1, p2], axis=0).astype(jnp.int32)
    wt_rows = jnp.concatenate([w1, w2, jnp.zeros((LANE - 2, tm), F32)], axis=0)
    wt_ref[...] = wt_rows.T


def merge_and_route(o_a, o_b, o_c, o_d, p2, x2, g1, w_merge_bf, w_out_bf, norm_w, sc2, sh2, w_router_t, router_bias, seq):
    t, d = x2.shape
    tm = min(MERGE_TM, seq)
    per_b = seq // tm
    gw = N_BRANCH * d
    tok = lambda w: pl.BlockSpec((tm, w), lambda i: (i, 0))
    bat = pl.BlockSpec((None, 1, d), lambda i: (i // per_b, 0, 0))
    ne = N_EXPERTS
    return pl.pallas_call(
        _merge_kernel,
        grid=(t // tm,),
        in_specs=[
            tok(BRANCH_W), tok(BRANCH_W), tok(BRANCH_W), tok(BRANCH_W),
            pl.BlockSpec((tm, gw), lambda i: (i, C_GATE * LANE // gw)),
            tok(d), bat,
            pl.BlockSpec((N_BRANCH, BRANCH_W, d), lambda i: (0, 0, 0)),
            pl.BlockSpec((d, d), lambda i: (0, 0)),
            pl.BlockSpec((1, d), lambda i: (0, 0)),
            bat, bat,
            pl.BlockSpec((ne, d), lambda i: (0, 0)),
            pl.BlockSpec((ne, 1), lambda i: (0, 0)),
        ],
        out_specs=[
            tok(d), tok(d),
            pl.BlockSpec((TOP_K, tm), lambda i: (0, i)),
            pl.BlockSpec((TOP_K, tm), lambda i: (0, i)),
            tok(LANE),
            pl.BlockSpec((ne, LANE), lambda i: (0, 0)),
        ],
        out_shape=[
            jax.ShapeDtypeStruct((t, d), F32),
            jax.ShapeDtypeStruct((t, d), F32),
            jax.ShapeDtypeStruct((TOP_K, t), jnp.int32),
            jax.ShapeDtypeStruct((TOP_K, t), jnp.int32),
            jax.ShapeDtypeStruct((t, LANE), F32),
            jax.ShapeDtypeStruct((ne, LANE), jnp.int32),
        ],
        scratch_shapes=[pltpu.VMEM((ne, LANE), F32)],
        compiler_params=_cparams(("arbitrary",)),
        name="merge_route",
    )(o_a, o_b, o_c, o_d, p2, x2, g1, w_merge_bf, w_out_bf, norm_w.reshape(1, d), sc2, sh2,
      w_router_t, router_bias.reshape(ne, 1))


MOE_BM = 128


SCATTER_UNROLL = 8


def _plan_kernel(idx_ref, pos_ref, cnt_ref, dest_ref, blke_ref, nused_ref):
    bm = MOE_BM
    ne = N_EXPERTS
    cnt = cnt_ref[...].astype(F32)
    padded = jnp.floor((cnt + (bm - 1)) * (1.0 / bm)) * bm
    r = lax.broadcasted_iota(jnp.int32, (ne, ne), 0)
    c = lax.broadcasted_iota(jnp.int32, (ne, ne), 1)
    pstart = _exact_rows_dot(jnp.where(c < r, 1.0, 0.0), padded)
    pend = pstart + padded

    idx = idx_ref[...]
    base = jnp.zeros(idx.shape, F32)
    for e in range(ne):
        base = jnp.where(idx == e, pstart[e:e + 1, 0:1], base)
    dest_ref[...] = base.astype(jnp.int32) + pos_ref[...]

    nbp = blke_ref.shape[1]
    blk_start = (lax.broadcasted_iota(jnp.int32, (ne, nbp), 1) * bm).astype(F32)
    done = jnp.sum(jnp.where(pend[:, 0:1] <= blk_start, 1.0, 0.0), axis=0, keepdims=True)
    blke_ref[...] = jnp.minimum(done, ne - 1.0).astype(jnp.int32)
    nused_ref[...] = (pend[ne - 1:ne, :] * (1.0 / bm)).astype(jnp.int32)


def _scatter_kernel(size_ref, dest_ref, rowtok_ref):
    n_tok = size_ref[0]
    n_rows = size_ref[1]
    u = SCATTER_UNROLL

    def zero_body(i, carry):
        for j in range(u):
            rowtok_ref[i * u + j] = 0
        return carry

    lax.fori_loop(0, n_rows // u, zero_body, 0)

    for k in range(TOP_K):
        def assign_body(i, carry, k=k):
            for j in range(u):
                t = i * u + j
                rowtok_ref[dest_ref[k * n_tok + t]] = t
            return carry

        lax.fori_loop(0, n_tok // u, assign_body, 0)


def moe_dispatch(idx, pos, counts):
    n_tok = idx.shape[1]
    n_rows = TOP_K * n_tok + N_EXPERTS * MOE_BM
    n_blocks = n_rows // MOE_BM
    nbp = -(-n_blocks // LANE) * LANE
    assert n_tok % SCATTER_UNROLL == 0 and n_rows % SCATTER_UNROLL == 0
    dest, blk_e, n_used = pl.pallas_call(
        _plan_kernel,
        out_shape=[
            jax.ShapeDtypeStruct((TOP_K, n_tok), jnp.int32),
            jax.ShapeDtypeStruct((1, nbp), jnp.int32),
            jax.ShapeDtypeStruct((1, LANE), jnp.int32),
        ],
        compiler_params=pltpu.CompilerParams(vmem_limit_bytes=VMEM_LIMIT),
        name="moe_plan",
    )(idx, pos, counts)
    dest = dest.reshape(-1)
    smem = pl.BlockSpec(memory_space=pltpu.SMEM)
    sizes = jnp.array([n_tok, n_rows], jnp.int32)
    row_tok = pl.pallas_call(
        _scatter_kernel,
        in_specs=[smem, smem],
        out_specs=smem,
        out_shape=jax.ShapeDtypeStruct((n_rows,), jnp.int32),
        name="moe_scatter",
    )(sizes, dest)
    return row_tok, dest, blk_e[0, :n_blocks], n_used[0, :1]


def _row_gather(src_hbm, tok_of_row, dst, sem, n):
    def body(r, carry):
        pltpu.make_async_copy(src_hbm.at[pl.ds(tok_of_row(r), 1), :], dst.at[pl.ds(r, 1), :], sem).start()
        return carry
    lax.fori_loop(0, n, body, 0, unroll=8)


def _row_gather_wait(src_hbm, dst, sem, n):
    def body(r, carry):
        pltpu.make_async_copy(src_hbm.at[pl.ds(0, 1), :], dst.at[pl.ds(r, 1), :], sem).wait()
        return carry
    lax.fori_loop(0, n, body, 0, unroll=8)


def _expert_kernel(blke_ref, nused_ref, rowtok_ref, h_hbm, wg_ref, wu_ref, wd_ref, y_ref,
                   buf_ref, sem_ref, wgb_ref, wub_ref, wdb_ref):
    b = pl.program_id(0)
    bm = MOE_BM
    used = nused_ref[0]
    slot = b % 2

    def start(blk, sl):
        _row_gather(h_hbm, lambda r: rowtok_ref[blk * bm + r], buf_ref.at[sl], sem_ref.at[sl], bm)

    @pl.when(b == 0)
    def _():
        start(0, 0)

    @pl.when(b + 1 < used)
    def _():
        start(b + 1, 1 - slot)

    changed = jnp.logical_or(b == 0, blke_ref[b] != blke_ref[jnp.maximum(b - 1, 0)])

    @pl.when(jnp.logical_and(b < used, changed))
    def _():
        wgb_ref[...] = wg_ref[...].astype(BF16)
        wub_ref[...] = wu_ref[...].astype(BF16)
        wdb_ref[...] = wd_ref[...].astype(BF16)

    @pl.when(b < used)
    def _():
        _row_gather_wait(h_hbm, buf_ref.at[slot], sem_ref.at[slot], bm)
        xb = buf_ref[slot].astype(BF16)
        hid = _silu(jnp.dot(xb, wgb_ref[...], preferred_element_type=F32)) * jnp.dot(xb, wub_ref[...], preferred_element_type=F32)
        y_ref[...] = jnp.dot(hid.astype(BF16), wdb_ref[...], preferred_element_type=F32)

    @pl.when(b >= used)
    def _():
        y_ref[...] = jnp.zeros_like(y_ref)


def moe_experts(h2, row_tok, blk_e, n_used, w_gate, w_up, w_down):
    t, d = h2.shape
    n_rows = row_tok.shape[0]
    bm = MOE_BM
    de = w_gate.shape[-1]
    grid_spec = pltpu.PrefetchScalarGridSpec(
        num_scalar_prefetch=3,
        grid=(n_rows // bm,),
        in_specs=[
            pl.BlockSpec(memory_space=pl.ANY),
            pl.BlockSpec((None, d, de), lambda b, be, nu, rt: (be[b], 0, 0)),
            pl.BlockSpec((None, d, de), lambda b, be, nu, rt: (be[b], 0, 0)),
            pl.BlockSpec((None, de, d), lambda b, be, nu, rt: (be[b], 0, 0)),
        ],
        out_specs=pl.BlockSpec((bm, d), lambda b, be, nu, rt: (b, 0)),
        scratch_shapes=[
            pltpu.VMEM((2, bm, d), F32),
            pltpu.SemaphoreType.DMA((2,)),
            pltpu.VMEM((d, de), BF16),
            pltpu.VMEM((d, de), BF16),
            pltpu.VMEM((de, d), BF16),
        ],
    )
    return pl.pallas_call(
        _expert_kernel,
        grid_spec=grid_spec,
        out_shape=jax.ShapeDtypeStruct((n_rows, d), F32),
        compiler_params=_cparams(("arbitrary",)),
        name="moe_experts",
    )(blk_e, n_used, row_tok, h2, w_gate, w_up, w_down)


COMB_TM = 256


def _combine_kernel(dest_ref, ys_hbm, x_ref, wt_ref, g2_ref, fw_ref, o_ref, buf_ref, sem_ref, *, n_tok, final_norm):
    i = pl.program_id(0)
    n = pl.num_programs(0)
    tm = x_ref.shape[0]
    slot = i % 2

    def start(tile, sl):
        for k in range(TOP_K):
            _row_gather(ys_hbm, lambda r: dest_ref[k * n_tok + tile * tm + r], buf_ref.at[sl, k], sem_ref.at[sl, k], tm)

    @pl.when(i == 0)
    def _():
        start(0, 0)

    @pl.when(i + 1 < n)
    def _():
        start(i + 1, 1 - slot)

    for k in range(TOP_K):
        _row_gather_wait(ys_hbm, buf_ref.at[slot, k], sem_ref.at[slot, k], tm)
    wt = wt_ref[...]
    moe = wt[:, 0:1] * buf_ref[slot, 0] + wt[:, 1:2] * buf_ref[slot, 1]
    out = x_ref[...] + g2_ref[...] * moe
    if final_norm:
        out = out * lax.rsqrt(jnp.mean(out * out, axis=-1, keepdims=True) + NORM_EPS) * fw_ref[...]
    o_ref[...] = out


def moe_combine(ys, dest, x2, wts, g2, final_w, seq, final_norm):
    t, d = x2.shape
    tm = min(COMB_TM, seq)
    per_b = seq // tm
    grid_spec = pltpu.PrefetchScalarGridSpec(
        num_scalar_prefetch=1,
        grid=(t // tm,),
        in_specs=[
            pl.BlockSpec(memory_space=pl.ANY),
            pl.BlockSpec((tm, d), lambda i, ds: (i, 0)),
            pl.BlockSpec((tm, LANE), lambda i, ds: (i, 0)),
            pl.BlockSpec((None, 1, d), lambda i, ds: (i // per_b, 0, 0)),
            pl.BlockSpec((1, d), lambda i, ds: (0, 0)),
        ],
        out_specs=pl.BlockSpec((tm, d), lambda i, ds: (i, 0)),
        scratch_shapes=[pltpu.VMEM((2, TOP_K, tm, d), F32), pltpu.SemaphoreType.DMA((2, TOP_K))],
    )
    return pl.pallas_call(
        functools.partial(_combine_kernel, n_tok=t, final_norm=final_norm),
        grid_spec=grid_spec,
        out_shape=jax.ShapeDtypeStruct((t, d), F32),
        compiler_params=_cparams(("arbitrary",)),
        name="moe_combine",
    )(dest, ys, x2, wts, g2, final_w.reshape(1, d))


def kernel(x, c, w_ada, b_ada, norm_mix_w, norm_ffn_w, w_in, conv_w, conv_b, ssd_dt_bias, ssd_a_log, ssd_d, ssd_norm_w, diff_lambda, diff_subln_w, hgrn_lb_logits, hgrn_norm_w, w_merge, w_out, w_router, router_bias, w_expert_gate, w_expert_up, w_expert_down, final_norm_w):
    b, s, d = x.shape
    t = b * s
    depth = w_in.shape[0]
    mod = ada_modulation(c, w_ada, b_ada)
    lb_p = jax.nn.softmax(hgrn_lb_logits.astype(F32), axis=0)
    lower_bounds = jnp.cumsum(lb_p, axis=0) - lb_p[0]
    w_router_t = w_router.T
    x2 = x.reshape(t, d)
    for l in range(depth):
        sh1, sc1, g1, sh2, sc2, g2 = [mod[l, :, i * d:(i + 1) * d].reshape(b, 1, d) for i in range(6)]
        p2 = in_projection(x2, norm_mix_w[l], sc1, sh1, pad_w_in(w_in[l]), s)
        p3 = p2.reshape(b, s, -1)
        o_a = ssd_branch(p3, conv_w[l], conv_b[l], ssd_dt_bias[l], ssd_a_log[l], ssd_d[l], ssd_norm_w[l])
        o_b = dilated_branch(p3)
        o_c = diff_branch(p3, diff_lambda[l], diff_subln_w[l], l)
        o_d = hgrn_branch(p3, lower_bounds[l], hgrn_norm_w[l])
        flat = lambda o: o.reshape(t, BRANCH_W)
        x_mid, h2, idx, pos, wts, counts = merge_and_route(
            flat(o_a), flat(o_b), flat(o_c), flat(o_d), p2, x2, g1, w_merge[l].astype(BF16), w_out[l].astype(BF16),
            norm_ffn_w[l], sc2, sh2, w_router_t, router_bias, s)
        row_tok, dest, blk_e, n_used = moe_dispatch(idx, pos, counts)
        ys = moe_experts(h2, row_tok, blk_e, n_used, w_expert_gate[l], w_expert_up[l], w_expert_down[l])
        x2 = moe_combine(ys, dest, x_mid, wts, g2, final_norm_w, s, final_norm=(l == depth - 1))
    return x2.reshape(b, s, d)
```

```python
import functools
import math

import jax
import jax.numpy as jnp
from jax import lax
from jax.experimental import pallas as pl
from jax.experimental.pallas import tpu as pltpu

F32 = jnp.float32
BF16 = jnp.bfloat16

D_MODEL = 1024
DEPTH = 2
N_BRANCH = 4
BRANCH_W = 512

SSD_D_INNER = 512
SSD_HEAD_DIM = 64
SSD_HEADS = 8
SSD_GROUPS = 2
SSD_HPG = 4
SSD_STATE = 64
SSD_CONV = 4
SSD_CHUNK = 128
SSD_CONV_CH = 768

DIL_PAIRS = ((128, 1), (512, 4), (2048, 16))
DIL_HPG = 4
DIL_HEAD_DIM = 128
DIL_HEADS = 12
DIL_BLOCK = 128

DIFF_HEADS = 4
DIFF_HEAD_DIM = 64

HG_HEADS = 8
HG_KEY_DIM = 64
HG_VAL_DIM = 64
HG_CHUNK = 64

N_EXPERTS = 64
N_EXPERT_GROUPS = 8
EXPERTS_PER_GROUP = 8
TOP_K = 2
D_EXPERT = 256

NORM_EPS = 1e-6
MASK_VALUE = -1e30

LANE = 128
VMEM_LIMIT = 48 * 1024 * 1024

C_GATE = 0
C_Z = 32
C_XBC = 36
C_DT = 42
C_QB = 44
C_KB = 56
C_VB = 68
C_QC = 80
C_KC = 84
C_VC = 88
C_QD = 92
C_FD = 96
C_ID = 100
C_GD = 104
N_COLB = 108
D_IN_PAD = N_COLB * LANE
DT_PAD = 2 * LANE


def _cparams(sem):
    return pltpu.CompilerParams(dimension_semantics=sem, vmem_limit_bytes=VMEM_LIMIT)


def _silu(v):
    return v * jax.nn.sigmoid(v)


def _bdot(a, b):
    return jnp.dot(a.astype(BF16), b.astype(BF16), preferred_element_type=F32)


def _bdot_nt(a, b):
    return lax.dot_general(a.astype(BF16), b.astype(BF16), (((1,), (1,)), ((), ())),
                           preferred_element_type=F32)


def _bdot_tn(a, b):
    return lax.dot_general(a.astype(BF16), b.astype(BF16), (((0,), (0,)), ((), ())),
                           preferred_element_type=F32)


def _exact_rows_dot(m01, v):
    hi = v.astype(BF16)
    r1 = v - hi.astype(F32)
    mid = r1.astype(BF16)
    lo = (r1 - mid.astype(F32)).astype(BF16)
    m = m01.astype(BF16)
    return (jnp.dot(m, hi, preferred_element_type=F32) + jnp.dot(m, mid, preferred_element_type=F32)
            + jnp.dot(m, lo, preferred_element_type=F32))


def _ada_kernel(c_ref, w_ref, b_ref, o_ref):
    o_ref[...] = _bdot(_silu(c_ref[...]), w_ref[...]) + b_ref[...]


def ada_modulation(c, w_ada, b_ada):
    depth, d, n = w_ada.shape
    b = c.shape[0]
    bp = 8
    c_pad = jnp.zeros((bp, d), F32).at[:b].set(c)
    tn = 1536
    out = pl.pallas_call(
        _ada_kernel,
        grid=(depth, n // tn),
        in_specs=[
            pl.BlockSpec((bp, d), lambda l, j: (0, 0)),
            pl.BlockSpec((None, d, tn), lambda l, j: (l, 0, j)),
            pl.BlockSpec((None, 1, tn), lambda l, j: (l, 0, j)),
        ],
        out_specs=pl.BlockSpec((None, bp, tn), lambda l, j: (l, 0, j)),
        out_shape=jax.ShapeDtypeStruct((depth, bp, n), F32),
        compiler_params=_cparams(("arbitrary", "arbitrary")),
        name="ada_mod",
    )(c_pad, w_ada, b_ada.reshape(depth, 1, n))
    return out[:, :b]


INPROJ_TM = 2048
INPROJ_TN = 512


def _inproj_kernel(x_ref, nw_ref, sc_ref, sh_ref, w_ref, o_ref, h_ref):
    @pl.when(pl.program_id(1) == 0)
    def _():
        x = x_ref[...]
        y = x * lax.rsqrt(jnp.mean(x * x, axis=-1, keepdims=True) + NORM_EPS) * nw_ref[...]
        h_ref[...] = (y * (1.0 + sc_ref[...]) + sh_ref[...]).astype(BF16)

    o_ref[...] = jnp.dot(h_ref[...], w_ref[...], preferred_element_type=F32).astype(o_ref.dtype)


def in_projection(x2, norm_w, scale, shift, w_pad, seq):
    t, d = x2.shape
    n = w_pad.shape[1]
    tm = min(INPROJ_TM, seq)
    tn = INPROJ_TN
    per_b = seq // tm
    return pl.pallas_call(
        _inproj_kernel,
        grid=(t // tm, n // tn),
        in_specs=[
            pl.BlockSpec((tm, d), lambda i, j: (i, 0)),
            pl.BlockSpec((1, d), lambda i, j: (0, 0)),
            pl.BlockSpec((None, 1, d), lambda i, j: (i // per_b, 0, 0)),
            pl.BlockSpec((None, 1, d), lambda i, j: (i // per_b, 0, 0)),
            pl.BlockSpec((d, tn), lambda i, j: (0, j)),
        ],
        out_specs=pl.BlockSpec((tm, tn), lambda i, j: (i, j)),
        out_shape=jax.ShapeDtypeStruct((t, n), BF16),
        scratch_shapes=[pltpu.VMEM((tm, d), BF16)],
        compiler_params=_cparams(("arbitrary", "arbitrary")),
        name="in_proj",
    )(x2, norm_w.reshape(1, d), scale, shift, w_pad)


def pad_w_in(w_in_l):
    d = w_in_l.shape[0]
    o_dt = SSD_D_INNER + SSD_CONV_CH
    o_gate = w_in_l.shape[1] - N_BRANCH * D_MODEL
    return jnp.concatenate(
        [w_in_l[:, o_gate:], w_in_l[:, :o_dt + SSD_HEADS], jnp.zeros((d, DT_PAD - SSD_HEADS), w_in_l.dtype),
         w_in_l[:, o_dt + SSD_HEADS:o_gate]], axis=1).astype(BF16)


def _ssd_kernel(z_ref, xbc_ref, dt_ref, cw_ref, cb_ref, dtb_ref, alog_ref, dsk_ref, nw_ref, o_ref,
                xpad_ref, state_ref, y_ref):
    c = pl.program_id(1)
    L = SSD_CHUNK
    halo = 8

    @pl.when(c == 0)
    def _():
        xpad_ref[0:halo, :] = jnp.zeros((halo, SSD_CONV_CH), F32)
        state_ref[...] = jnp.zeros_like(state_ref)

    @pl.when(c > 0)
    def _():
        xpad_ref[0:halo, :] = xpad_ref[L:L + halo, :]

    z = z_ref[...].astype(F32)
    xpad_ref[halo:halo + L, :] = xbc_ref[...].astype(F32)

    conv = cb_ref[...]
    for j in range(SSD_CONV):
        off = halo - (SSD_CONV - 1) + j
        conv = conv + cw_ref[j:j + 1, :] * xpad_ref[off:off + L, :]
    xc = _silu(conv)
    xs = xc[:, 0:SSD_D_INNER]
    ns = SSD_GROUPS * SSD_STATE
    bm = xc[:, SSD_D_INNER:SSD_D_INNER + ns]
    cm = xc[:, SSD_D_INNER + ns:SSD_D_INNER + 2 * ns]

    dt = jax.nn.softplus(dt_ref[:, 0:LANE].astype(F32) + dtb_ref[...])
    a = -jnp.exp(alog_ref[...])
    adt = dt * a
    row = lax.broadcasted_iota(jnp.int32, (L, L), 0)
    col = lax.broadcasted_iota(jnp.int32, (L, L), 1)
    tril = row >= col
    a_cs = _exact_rows_dot(jnp.where(tril, 1.0, 0.0), adt)
    a_cs_t = a_cs.T
    tot = a_cs[L - 1:L, :]

    cbs = [_bdot_nt(cm[:, g * SSD_STATE:(g + 1) * SSD_STATE], bm[:, g * SSD_STATE:(g + 1) * SSD_STATE])
           for g in range(SSD_GROUPS)]
    for e in range(SSD_HEADS):
        g = e // SSD_HPG
        cs_col = a_cs[:, e:e + 1]
        cs_row = a_cs_t[e:e + 1, :]
        lmat = jnp.exp(jnp.where(tril, cs_col - cs_row, MASK_VALUE))
        xe = xs[:, e * SSD_HEAD_DIM:(e + 1) * SSD_HEAD_DIM]
        xdt = xe * dt[:, e:e + 1]
        bg = bm[:, g * SSD_STATE:(g + 1) * SSD_STATE]
        cg = cm[:, g * SSD_STATE:(g + 1) * SSD_STATE]
        st = state_ref[e]
        y = _bdot(cbs[g] * lmat, xdt)
        y = y + _bdot(cg, st) * jnp.exp(cs_col)
        tot_e = tot[:, e:e + 1]
        state_ref[e] = jnp.exp(tot_e) * st + _bdot_tn(bg * jnp.exp(tot_e - cs_col), xdt)
        y_ref[:, e * SSD_HEAD_DIM:(e + 1) * SSD_HEAD_DIM] = y + dsk_ref[:, e:e + 1] * xe

    yz = y_ref[...] * _silu(z)
    o_ref[...] = yz * lax.rsqrt(jnp.mean(yz * yz, axis=-1, keepdims=True) + NORM_EPS) * nw_ref[...]


def _pad_lanes(v, n=LANE):
    return jnp.zeros((1, n), F32).at[0, :v.shape[0]].set(v.astype(F32))


def ssd_branch(p3, conv_w, conv_b, dt_bias, a_log, d_skip, norm_w):
    b, s, _ = p3.shape
    L = SSD_CHUNK
    const = lambda shape: pl.BlockSpec(shape, lambda i, c: (0, 0))
    return pl.pallas_call(
        _ssd_kernel,
        grid=(b, s // L),
        in_specs=[
            pl.BlockSpec((None, L, SSD_D_INNER), lambda i, c: (i, c, C_Z * LANE // SSD_D_INNER)),
            pl.BlockSpec((None, L, SSD_CONV_CH), lambda i, c: (i, c, C_XBC * LANE // SSD_CONV_CH)),
            pl.BlockSpec((None, L, DT_PAD), lambda i, c: (i, c, C_DT * LANE // DT_PAD)),
            const((SSD_CONV, SSD_CONV_CH)),
            const((1, SSD_CONV_CH)),
            const((1, LANE)),
            const((1, LANE)),
            const((1, LANE)),
            const((1, SSD_D_INNER)),
        ],
        out_specs=pl.BlockSpec((None, L, SSD_D_INNER), lambda i, c: (i, c, 0)),
        out_shape=jax.ShapeDtypeStruct((b, s, SSD_D_INNER), F32),
        scratch_shapes=[
            pltpu.VMEM((L + 8, SSD_CONV_CH), F32),
            pltpu.VMEM((SSD_HEADS, SSD_STATE, SSD_HEAD_DIM), F32),
            pltpu.VMEM((L, SSD_D_INNER), F32),
        ],
        compiler_params=_cparams(("arbitrary", "arbitrary")),
        name="ssd",
    )(p3, p3, p3, conv_w, conv_b.reshape(1, -1), _pad_lanes(dt_bias), _pad_lanes(a_log), _pad_lanes(d_skip),
      norm_w.reshape(1, -1))


DIL_UNROLL = 4


def _dil_kernel(q_ref, k_ref, v_ref, o_ref, m_ref, l_ref, tmp_ref, qd_ref, kd_ref, vd_ref):
    g = pl.program_id(2)
    s_len = q_ref.shape[0]
    blk = DIL_BLOCK
    scale = DIL_HEAD_DIM ** -0.5

    @pl.when(g == 0)
    def _():
        m_ref[...] = jnp.full_like(m_ref, MASK_VALUE)
        l_ref[...] = jnp.zeros_like(l_ref)
        o_ref[...] = jnp.zeros_like(o_ref)

    def run_group(dil):
        sub = s_len // dil
        nblk = sub // blk

        if dil == 1:
            qd_ref[...] = (q_ref[...].astype(F32) * scale).astype(BF16)
            kd, vd = k_ref, v_ref
        else:
            for src, dst, mul in ((q_ref, qd_ref, scale), (k_ref, kd_ref, None), (v_ref, vd_ref, None)):
                x = src[...].astype(F32)
                tmp_ref[...] = x if mul is None else x * mul

                def gather(r, carry, dst=dst):
                    dst[pl.ds(pl.multiple_of(r * sub, blk), sub), :] = tmp_ref[pl.ds(r, sub, stride=dil), :].astype(BF16)
                    return carry

                lax.fori_loop(0, dil, gather, 0)
            kd, vd = kd_ref, vd_ref

        def unit(r, b, nkey):
            row0 = pl.multiple_of(r * sub + b * blk, blk)
            ks = pl.ds(pl.multiple_of(row0 - (nkey - blk), blk), nkey)
            base = r + dil * blk * b
            acc_rows = pl.ds(pl.multiple_of(base, blk), blk) if dil == 1 else pl.ds(base, blk, stride=dil)
            i = lax.broadcasted_iota(jnp.int32, (blk, nkey), 0) + (nkey - blk)
            j = lax.broadcasted_iota(jnp.int32, (blk, nkey), 1)
            ok = jnp.logical_and(j <= i, j >= i - blk)
            sc = lax.dot_general(qd_ref[pl.ds(row0, blk), :], kd[ks, :], (((1,), (1,)), ((), ())),
                                 preferred_element_type=F32)
            sc = jnp.where(ok, sc, MASK_VALUE)
            m_old = m_ref[acc_rows, :][:, 0:1]
            m_new = jnp.maximum(m_old, jnp.max(sc, axis=-1, keepdims=True))
            alpha = jnp.exp(m_old - m_new)
            p = jnp.exp(sc - m_new).astype(BF16)
            v_ext = jnp.concatenate([vd[ks, :], jnp.ones((nkey, LANE), BF16)], axis=1)
            pv = jnp.dot(p, v_ext, preferred_element_type=F32)
            o_ref[acc_rows, :] = alpha * o_ref[acc_rows, :] + pv[:, :LANE]
            l_ref[acc_rows, :] = alpha * l_ref[acc_rows, :] + pv[:, LANE:]
            m_ref[acc_rows, :] = jnp.broadcast_to(m_new, (blk, LANE))

        def first_unit(r, carry):
            unit(r, 0, blk)
            return carry

        def later_unit(u, carry):
            unit(u // (nblk - 1), 1 + u % (nblk - 1), 2 * blk)
            return carry

        lax.fori_loop(0, dil, first_unit, 0, unroll=min(dil, DIL_UNROLL))
        if nblk > 1:
            lax.fori_loop(0, dil * (nblk - 1), later_unit, 0, unroll=DIL_UNROLL)

    for gi, (window, dil) in enumerate(DIL_PAIRS):
        assert window // dil == blk

        @pl.when(g == gi)
        def _(dil=dil):
            run_group(dil)

    @pl.when(g == len(DIL_PAIRS) - 1)
    def _():
        o_ref[...] = o_ref[...] / l_ref[...]


def dilated_branch(p3):
    b, s, _ = p3.shape
    ng = len(DIL_PAIRS)
    spec = lambda c0: pl.BlockSpec((None, s, DIL_HEAD_DIM), lambda i, j, g: (i, 0, c0 + g * DIL_HPG + j))
    return pl.pallas_call(
        _dil_kernel,
        grid=(b, DIL_HPG, ng),
        in_specs=[spec(C_QB), spec(C_KB), spec(C_VB)],
        out_specs=pl.BlockSpec((None, s, DIL_HEAD_DIM), lambda i, j, g: (i, 0, j)),
        out_shape=jax.ShapeDtypeStruct((b, s, BRANCH_W), F32),
        scratch_shapes=[pltpu.VMEM((s, LANE), F32), pltpu.VMEM((s, LANE), F32), pltpu.VMEM((s, LANE), F32),
                        pltpu.VMEM((s, LANE), BF16), pltpu.VMEM((s, LANE), BF16), pltpu.VMEM((s, LANE), BF16)],
        compiler_params=_cparams(("arbitrary", "arbitrary", "arbitrary")),
        name="dilated",
    )(p3, p3, p3)


DIFF_TQ = 512
DIFF_TK = 1024
LOG2E = 1.4426950408889634


DIFF_ONES = 16


def _diff_kernel(lam_ref, q_ref, k_ref, v_ref, nw_ref, o_ref, vt_ref, m_ref, acc_ref, *, lam_init):
    qi = pl.program_id(2)
    tq, tk, dh = DIFF_TQ, DIFF_TK, DIFF_HEAD_DIM
    w = 2 * dh
    s_len = k_ref.shape[0]

    @pl.when(qi == 0)
    def _():
        for c0 in range(0, s_len, tk):
            vt_ref[0:w, c0:c0 + tk] = v_ref[c0:c0 + tk, :].astype(F32).T.astype(BF16)
        vt_ref[w:w + DIFF_ONES, :] = jnp.ones((DIFF_ONES, s_len), BF16)

    q_t = (q_ref[...].astype(F32) * (dh ** -0.5 * LOG2E)).T
    half = lax.broadcasted_iota(jnp.int32, (w, tq), 0) < dh
    q_sel = [jnp.where(half, q_t, 0.0).astype(BF16), jnp.where(half, 0.0, q_t).astype(BF16)]
    for t in range(2):
        m_ref[t] = jnp.full((8, tq), MASK_VALUE, F32)
        acc_ref[t] = jnp.zeros((w + DIFF_ONES, tq), F32)

    def scores(kstart, width, masked):
        kb = k_ref[pl.ds(kstart, width), :]
        out = [jnp.dot(kb, q_sel[t], preferred_element_type=F32) for t in range(2)]
        if masked:
            row = lax.broadcasted_iota(jnp.int32, (width, tq), 0)
            col = lax.broadcasted_iota(jnp.int32, (width, tq), 1)
            out = [jnp.where(row <= col, sc, MASK_VALUE) for sc in out]
        return out

    def over_causal_blocks(block):
        def main_step(j, carry):
            block(pl.multiple_of(j * tk, tk), tk, False)
            return carry

        per = tk // tq
        lax.fori_loop(0, qi // per, main_step, 0)
        for r in range(1, per):
            @pl.when(qi % per >= r)
            def _(r=r):
                block(pl.multiple_of((qi // per) * tk + (r - 1) * tq, tq), tq, False)
        block(pl.multiple_of(qi * tq, tq), tq, True)

    def max_block(kstart, width, masked):
        for t, sc in enumerate(scores(kstart, width, masked)):
            m_ref[t] = jnp.maximum(m_ref[t], jnp.max(sc.reshape(width // 8, 8, tq), axis=0))

    over_causal_blocks(max_block)
    m_fin = [jnp.max(m_ref[t], axis=0, keepdims=True) for t in range(2)]

    def acc_block(kstart, width, masked):
        vb = vt_ref[:, pl.ds(kstart, width)]
        for t, sc in enumerate(scores(kstart, width, masked)):
            p = jnp.exp2(sc - m_fin[t]).astype(BF16)
            acc_ref[t] += jnp.dot(vb, p, preferred_element_type=F32)

    over_causal_blocks(acc_block)

    lam_p = lam_ref[...]
    lam = (jnp.exp(jnp.sum(lam_p[0:1] * lam_p[1:2], axis=-1, keepdims=True))
           - jnp.exp(jnp.sum(lam_p[2:3] * lam_p[3:4], axis=-1, keepdims=True)) + lam_init)
    a0, a1 = acc_ref[0], acc_ref[1]
    o_t = a0[0:w] / a0[w:w + 1] - lam * (a1[0:w] / a1[w:w + 1])
    o = o_t.T
    o = o * lax.rsqrt(jnp.mean(o * o, axis=-1, keepdims=True) + NORM_EPS) * nw_ref[...]
    o_ref[...] = o * (1.0 - lam_init)


def diff_branch(p3, diff_lambda, subln_w, layer):
    b, s, _ = p3.shape
    lam_init = 0.8 - 0.6 * math.exp(-0.3 * layer)
    w = 2 * DIFF_HEAD_DIM
    return pl.pallas_call(
        functools.partial(_diff_kernel, lam_init=lam_init),
        grid=(b, DIFF_HEADS, s // DIFF_TQ),
        in_specs=[
            pl.BlockSpec((4, DIFF_HEAD_DIM), lambda i, h, t: (0, 0)),
            pl.BlockSpec((None, DIFF_TQ, w), lambda i, h, t: (i, t, C_QC + h)),
            pl.BlockSpec((None, s, w), lambda i, h, t: (i, 0, C_KC + h)),
            pl.BlockSpec((None, s, w), lambda i, h, t: (i, 0, C_VC + h)),
            pl.BlockSpec((1, w), lambda i, h, t: (0, 0)),
        ],
        out_specs=pl.BlockSpec((None, DIFF_TQ, w), lambda i, h, t: (i, t, h)),
        out_shape=jax.ShapeDtypeStruct((b, s, BRANCH_W), F32),
        scratch_shapes=[pltpu.VMEM((w + DIFF_ONES, s), BF16), pltpu.VMEM((2, 8, DIFF_TQ), F32),
                        pltpu.VMEM((2, w + DIFF_ONES, DIFF_TQ), F32)],
        compiler_params=_cparams(("arbitrary", "arbitrary", "arbitrary")),
        name="diff_attn",
    )(diff_lambda, p3, p3, p3, subln_w.reshape(1, w))


HG_LEVELS = 6
HG_STEP_CHUNKS = 2


def _hgrn_tables():
    import numpy as np
    c = HG_CHUNK
    i = np.arange(c)[:, None]
    j = np.arange(c)[None, :]
    tril = (j <= i).astype(np.float32)
    masks = []
    for lv in range(HG_LEVELS):
        sz = 1 << lv
        m = ((i // (2 * sz) == j // (2 * sz)) & (i // sz == j // sz + 1)).astype(np.float32)
        masks.append(np.concatenate([m, m], axis=0))
    lane = np.arange(LANE)
    same_head = (lane[:, None] // HG_KEY_DIM == lane[None, :] // HG_KEY_DIM).astype(np.float32)
    return tril, np.stack(masks, axis=0), same_head


def _hgrn_kernel(q_ref, f_ref, i_ref, g_ref, lb_ref, tri_ref, msk_ref, sh_ref, nw_ref, o_ref, state_ref, gpad_ref):
    @pl.when(pl.program_id(1) == 0)
    def _():
        state_ref[...] = jnp.zeros_like(state_ref)

    for ci in range(HG_STEP_CHUNKS):
        rows = slice(ci * HG_CHUNK, (ci + 1) * HG_CHUNK)
        _hgrn_chunk(q_ref[rows, :], f_ref[rows, :], i_ref[rows, :], g_ref, lb_ref, tri_ref, msk_ref, sh_ref, nw_ref,
                    o_ref, state_ref, gpad_ref, rows)


def _hgrn_chunk(q_in, f_in, i_in, g_ref, lb_ref, tri_ref, msk_ref, sh_ref, nw_ref, o_ref, state_ref, gpad_ref, rows):
    cs = HG_CHUNK
    nl = HG_LEVELS
    w = q_in.shape[1]
    sub_rows = 8

    lb = lb_ref[...]
    f_gate = lb + (1.0 - lb) * jax.nn.sigmoid(f_in.astype(F32))
    log_f = jnp.log(f_gate)
    k_in = 1.0 - f_gate
    q = _silu(q_in.astype(F32))
    v = i_in.astype(F32)

    hi, lo = _split_bf16(log_f)
    tri = tri_ref[...]
    g = jnp.dot(tri, hi, preferred_element_type=F32) + jnp.dot(tri, lo, preferred_element_type=F32)
    gpad_ref[0:sub_rows, :] = jnp.zeros((sub_rows, w), F32)
    gpad_ref[sub_rows:sub_rows + cs, :] = g
    g_last = g[cs - 1:cs, :]

    def g_row(r):
        return jnp.broadcast_to(gpad_ref[sub_rows + r:sub_rows + r + 1, :], (sub_rows, w))

    sub = lax.broadcasted_iota(jnp.int32, (sub_rows, w), 0)
    tiles = [[] for _ in range(nl)]
    for t in range(cs // sub_rows):
        r0 = t * sub_rows
        gt = g[r0:r0 + sub_rows]
        prev = gpad_ref[r0 + sub_rows - 1:r0 + 2 * sub_rows - 1, :]
        tiles[0].append(jnp.where(sub % 2 == 1, gt - prev, 0.0))
        c1 = jnp.where(sub < 4, g_row(r0 + 1), g_row(r0 + 5))
        tiles[1].append(jnp.where((sub // 2) % 2 == 1, gt - c1, c1 - gt))
        c2 = g_row(r0 + 3)
        tiles[2].append(jnp.where(sub >= 4, gt - c2, c2 - gt))
        for lv in range(3, nl):
            span = (1 << lv) // sub_rows
            mid = (t // (2 * span)) * 2 * span + span
            cm = g_row(mid * sub_rows - 1)
            tiles[lv].append(gt - cm if (t // span) % 2 == 1 else cm - gt)
    dec = [jnp.exp(jnp.concatenate(tl, axis=0)) for tl in tiles]
    q_lv = [q * d for d in dec]
    k_lv = [k_in * d for d in dec]
    q_full = q * jnp.exp(g)
    k_full = k_in * jnp.exp(g_last - g)
    decay_row = jnp.exp(g_last)

    same_head = sh_ref[...]
    same_head_b = same_head.astype(BF16)
    first = lax.broadcasted_iota(jnp.int32, (cs, LANE), 1) < HG_KEY_DIM
    for p in range(w // LANE):
        sl = slice(p * LANE, (p + 1) * LANE)
        attn = jnp.zeros((2 * cs, cs), F32)
        for lv in range(nl):
            qp = q_lv[lv][:, sl]
            lhs = jnp.concatenate([jnp.where(first, qp, 0.0), jnp.where(first, 0.0, qp)], axis=0)
            attn = attn + msk_ref[lv] * _bdot_nt(lhs, k_lv[lv][:, sl])
        vp = v[:, sl]
        vpb = vp.astype(BF16)
        r = jnp.dot(attn.astype(BF16), vpb, preferred_element_type=F32)
        o = jnp.where(first, r[0:cs], r[cs:2 * cs])
        o = o + jnp.dot((q[:, sl] * k_in[:, sl]).astype(BF16), same_head_b, preferred_element_type=F32) * vp
        st = state_ref[p]
        o = o + _bdot_nt(q_full[:, sl], st)
        state_ref[p] = decay_row[:, sl] * st + same_head * _bdot_tn(vpb, k_full[:, sl])
        ms = jnp.dot((o * o).astype(BF16), same_head_b, preferred_element_type=F32) * (1.0 / HG_VAL_DIM)
        o = o * lax.rsqrt(ms + NORM_EPS) * nw_ref[...]
        o_ref[rows, sl] = o * _silu(g_ref[rows, sl].astype(F32))


def hgrn_branch(p3, lower_bound, norm_w):
    b, s, _ = p3.shape
    cs = HG_CHUNK * HG_STEP_CHUNKS
    w = HG_HEADS * HG_KEY_DIM
    assert HG_KEY_DIM == HG_VAL_DIM and LANE == 2 * HG_KEY_DIM
    tril, masks, same_head = _hgrn_tables()
    seg = lambda c0: pl.BlockSpec((None, cs, w), lambda i, c: (i, c, c0 * LANE // w))
    return pl.pallas_call(
        _hgrn_kernel,
        grid=(b, s // cs),
        in_specs=[
            seg(C_QD), seg(C_FD), seg(C_ID), seg(C_GD),
            pl.BlockSpec((1, w), lambda i, c: (0, 0)),
            pl.BlockSpec(tril.shape, lambda i, c: (0, 0)),
            pl.BlockSpec(masks.shape, lambda i, c: (0, 0, 0)),
            pl.BlockSpec(same_head.shape, lambda i, c: (0, 0)),
            pl.BlockSpec((1, LANE), lambda i, c: (0, 0)),
        ],
        out_specs=pl.BlockSpec((None, cs, w), lambda i, c: (i, c, 0)),
        out_shape=jax.ShapeDtypeStruct((b, s, BRANCH_W), F32),
        scratch_shapes=[pltpu.VMEM((w // LANE, LANE, LANE), F32), pltpu.VMEM((HG_CHUNK + 8, w), F32)],
        compiler_params=_cparams(("arbitrary", "arbitrary")),
        name="hgrn2",
    )(p3, p3, p3, p3, lower_bound.reshape(1, w).astype(F32), jnp.asarray(tril, BF16), jnp.asarray(masks, F32),
      jnp.asarray(same_head, F32), jnp.tile(norm_w.astype(F32), LANE // HG_VAL_DIM).reshape(1, LANE))


MERGE_TM = 512


def _split_bf16(v):
    hi = v.astype(BF16)
    return hi, (v - hi.astype(F32)).astype(BF16)


def _first_index_of_max(vals, iota, n):
    top = jnp.max(vals, axis=0, keepdims=True)
    idx = jnp.min(jnp.where(vals == top, iota, n), axis=0, keepdims=True)
    return top, idx


def _merge_kernel(oa_ref, ob_ref, oc_ref, od_ref, gl_ref, x_ref, g1_ref, wm_ref, wo_ref,
                  nw_ref, sc_ref, sh_ref, wr_ref, rb_ref,
                  xo_ref, h2_ref, idx_ref, pos_ref, wt_ref, cnt_ref, carry_ref):
    step = pl.program_id(0)
    tm = x_ref.shape[0]
    d = D_MODEL

    acc = jnp.zeros((tm, d), F32)
    for n, o_ref in enumerate((oa_ref, ob_ref, oc_ref, od_ref)):
        proj = jnp.dot(o_ref[...].astype(BF16), wm_ref[n], preferred_element_type=F32)
        acc = acc + jax.nn.sigmoid(gl_ref[:, n * d:(n + 1) * d].astype(F32)) * proj
    mix = jnp.dot(acc.astype(BF16), wo_ref[...], preferred_element_type=F32)
    x_new = x_ref[...] + g1_ref[...] * mix
    xo_ref[...] = x_new

    y = x_new * lax.rsqrt(jnp.mean(x_new * x_new, axis=-1, keepdims=True) + NORM_EPS) * nw_ref[...]
    h2 = y * (1.0 + sc_ref[...]) + sh_ref[...]
    _to_row_tiles(h2_ref, h2)

    h_hi, h_lo = _split_bf16(h2)
    w_hi, w_lo = _split_bf16(wr_ref[...])
    nt = lambda a, b: lax.dot_general(a, b, (((1,), (1,)), ((), ())), preferred_element_type=F32)
    logits = nt(w_hi, h_hi) + nt(w_hi, h_lo) + nt(w_lo, h_hi)
    scores = jax.nn.sigmoid(logits)
    sel = scores + rb_ref[...]

    ne, ng, pg = N_EXPERTS, N_EXPERT_GROUPS, EXPERTS_PER_GROUP
    iota_g = lax.broadcasted_iota(jnp.int32, (pg, tm), 0)
    best_score = None
    best_group = None
    for g in range(ng):
        xg = sel[g * pg:(g + 1) * pg, :]
        top1, i1 = _first_index_of_max(xg, iota_g, pg)
        top2 = jnp.max(jnp.where(iota_g == i1, -jnp.inf, xg), axis=0, keepdims=True)
        gs = top1 + top2
        if g == 0:
            best_score, best_group = gs, jnp.zeros((1, tm), jnp.int32)
        else:
            better = gs > best_score
            best_score = jnp.where(better, gs, best_score)
            best_group = jnp.where(better, g, best_group)

    iota_e = lax.broadcasted_iota(jnp.int32, (ne, tm), 0)
    masked = jnp.where(iota_e // pg == best_group, sel, MASK_VALUE)
    _, e1 = _first_index_of_max(masked, iota_e, ne)
    oh1 = iota_e == e1
    _, e2 = _first_index_of_max(jnp.where(oh1, MASK_VALUE, masked), iota_e, ne)
    oh2 = iota_e == e2
    w1 = jnp.sum(jnp.where(oh1, scores, 0.0), axis=0, keepdims=True)
    w2 = jnp.sum(jnp.where(oh2, scores, 0.0), axis=0, keepdims=True)
    wsum = w1 + w2
    w1 = w1 / wsum
    w2 = w2 / wsum

    @pl.when(step == 0)
    def _():
        carry_ref[...] = jnp.zeros_like(carry_ref)

    f1 = jnp.where(oh1, 1.0, 0.0)
    f2 = jnp.where(oh2, 1.0, 0.0)
    both = f1 + f2
    r_i = lax.broadcasted_iota(jnp.int32, (tm, tm), 0)
    c_i = lax.broadcasted_iota(jnp.int32, (tm, tm), 1)
    before = jnp.where(r_i < c_i, 1.0, 0.0).astype(BF16)
    rank = jnp.dot(both.astype(BF16), before, preferred_element_type=F32) + carry_ref[:, 0:1]
    p1 = jnp.sum(f1 * rank, axis=0, keepdims=True)
    p2 = jnp.sum(f2 * rank, axis=0, keepdims=True)
    carry_new = carry_ref[...] + jnp.sum(both, axis=1, keepdims=True)
    carry_ref[...] = carry_new
    cnt_ref[...] = carry_new.astype(jnp.int32)

    idx_ref[...] = jnp.concatenate([e1, e2], axis=0)
    pos_ref[...] = jnp.concatenate([p1, p2], axis=0).astype(jnp.int32)
    wt_rows = jnp.concatenate([w1, w2, jnp.zeros((LANE - 2, tm), F32)], axis=0)
    wt_ref[...] = wt_rows.T


def merge_and_route(o_a, o_b, o_c, o_d, p2, x2, g1, w_merge_bf, w_out_bf, norm_w, sc2, sh2, w_router_t, router_bias, seq):
    t, d = x2.shape
    tm = min(MERGE_TM, seq)
    per_b = seq // tm
    gw = N_BRANCH * d
    tok = lambda w: pl.BlockSpec((tm, w), lambda i: (i, 0))
    bat = pl.BlockSpec((None, 1, d), lambda i: (i // per_b, 0, 0))
    ne = N_EXPERTS
    return pl.pallas_call(
        _merge_kernel,
        grid=(t // tm,),
        in_specs=[
            tok(BRANCH_W), tok(BRANCH_W), tok(BRANCH_W), tok(BRANCH_W),
            pl.BlockSpec((tm, gw), lambda i: (i, C_GATE * LANE // gw)),
            tok(d), bat,
            pl.BlockSpec((N_BRANCH, BRANCH_W, d), lambda i: (0, 0, 0)),
            pl.BlockSpec((d, d), lambda i: (0, 0)),
            pl.BlockSpec((1, d), lambda i: (0, 0)),
            bat, bat,
            pl.BlockSpec((ne, d), lambda i: (0, 0)),
            pl.BlockSpec((ne, 1), lambda i: (0, 0)),
        ],
        out_specs=[
            tok(d), pl.BlockSpec((tm * ROW_TILE, LANE), lambda i: (i, 0)),
            pl.BlockSpec((TOP_K, tm), lambda i: (0, i)),
            pl.BlockSpec((TOP_K, tm), lambda i: (0, i)),
            tok(LANE),
            pl.BlockSpec((ne, LANE), lambda i: (0, 0)),
        ],
        out_shape=[
            jax.ShapeDtypeStruct((t, d), F32),
            jax.ShapeDtypeStruct((t * ROW_TILE, LANE), F32),
            jax.ShapeDtypeStruct((TOP_K, t), jnp.int32),
            jax.ShapeDtypeStruct((TOP_K, t), jnp.int32),
            jax.ShapeDtypeStruct((t, LANE), F32),
            jax.ShapeDtypeStruct((ne, LANE), jnp.int32),
        ],
        scratch_shapes=[pltpu.VMEM((ne, LANE), F32)],
        compiler_params=_cparams(("arbitrary",)),
        name="merge_route",
    )(o_a, o_b, o_c, o_d, p2, x2, g1, w_merge_bf, w_out_bf, norm_w.reshape(1, d), sc2, sh2,
      w_router_t, router_bias.reshape(ne, 1))


MOE_BM = 256


SCATTER_UNROLL = 8


def _plan_kernel(idx_ref, pos_ref, cnt_ref, dest_ref, blke_ref, nused_ref):
    bm = MOE_BM
    ne = N_EXPERTS
    cnt = cnt_ref[...].astype(F32)
    padded = jnp.floor((cnt + (bm - 1)) * (1.0 / bm)) * bm
    r = lax.broadcasted_iota(jnp.int32, (ne, ne), 0)
    c = lax.broadcasted_iota(jnp.int32, (ne, ne), 1)
    pstart = _exact_rows_dot(jnp.where(c < r, 1.0, 0.0), padded)
    pend = pstart + padded

    idx = idx_ref[...]
    base = jnp.zeros(idx.shape, F32)
    for e in range(ne):
        base = jnp.where(idx == e, pstart[e:e + 1, 0:1], base)
    dest_ref[...] = base.astype(jnp.int32) + pos_ref[...]

    nbp = blke_ref.shape[1]
    blk_start = (lax.broadcasted_iota(jnp.int32, (ne, nbp), 1) * bm).astype(F32)
    done = jnp.sum(jnp.where(pend[:, 0:1] <= blk_start, 1.0, 0.0), axis=0, keepdims=True)
    blke_ref[...] = jnp.minimum(done, ne - 1.0).astype(jnp.int32)
    nused_ref[...] = (pend[ne - 1:ne, :] * (1.0 / bm)).astype(jnp.int32)


def _scatter_kernel(size_ref, dest_ref, rowtok_ref):
    n_tok = size_ref[0]
    n_rows = size_ref[1]
    u = SCATTER_UNROLL

    def zero_body(i, carry):
        for j in range(u):
            rowtok_ref[i * u + j] = 0
        return carry

    lax.fori_loop(0, n_rows // u, zero_body, 0)

    for k in range(TOP_K):
        def assign_body(i, carry, k=k):
            for j in range(u):
                t = i * u + j
                rowtok_ref[dest_ref[k * n_tok + t]] = t
            return carry

        lax.fori_loop(0, n_tok // u, assign_body, 0)


def moe_dispatch(idx, pos, counts):
    n_tok = idx.shape[1]
    n_rows = TOP_K * n_tok + N_EXPERTS * MOE_BM
    n_blocks = n_rows // MOE_BM
    nbp = -(-n_blocks // LANE) * LANE
    assert n_tok % SCATTER_UNROLL == 0 and n_rows % SCATTER_UNROLL == 0
    dest, blk_e, n_used = pl.pallas_call(
        _plan_kernel,
        out_shape=[
            jax.ShapeDtypeStruct((TOP_K, n_tok), jnp.int32),
            jax.ShapeDtypeStruct((1, nbp), jnp.int32),
            jax.ShapeDtypeStruct((1, LANE), jnp.int32),
        ],
        compiler_params=pltpu.CompilerParams(vmem_limit_bytes=VMEM_LIMIT),
        name="moe_plan",
    )(idx, pos, counts)
    dest = dest.reshape(-1)
    smem = pl.BlockSpec(memory_space=pltpu.SMEM)
    sizes = jnp.array([n_tok, n_rows], jnp.int32)
    row_tok = pl.pallas_call(
        _scatter_kernel,
        in_specs=[smem, smem],
        out_specs=smem,
        out_shape=jax.ShapeDtypeStruct((n_rows,), jnp.int32),
        name="moe_scatter",
    )(sizes, dest)
    return row_tok, dest, blk_e[0, :n_blocks], n_used[0, :1]


ROW_TILE = 8


def _to_row_tiles(ref, x):
    rows = x.shape[0]
    for j in range(ROW_TILE):
        ref[pl.ds(j, rows, stride=ROW_TILE), :] = x[:, j * LANE:(j + 1) * LANE]


def _from_row_tiles(ref, rows):
    return jnp.concatenate([ref[pl.ds(j, rows, stride=ROW_TILE), :] for j in range(ROW_TILE)], axis=1)


def _tile_copy(src_hbm, row, dst, r, sem):
    return pltpu.make_async_copy(src_hbm.at[pl.ds(pl.multiple_of(row * ROW_TILE, ROW_TILE), ROW_TILE), :],
                                 dst.at[pl.ds(r * ROW_TILE, ROW_TILE), :], sem)


def _gather_start(src_hbm, row_of, dst, sem, n):
    for r in range(n):
        _tile_copy(src_hbm, row_of(r), dst, r, sem).start(priority=r % 2)


def _gather_wait(src_hbm, dst, sem, n):
    for r in range(n):
        _tile_copy(src_hbm, 0, dst, r, sem).wait()


def _expert_kernel(blke_ref, nused_ref, rowtok_ref, h_hbm, wg_ref, wu_ref, wd_ref, y_ref,
                   buf_ref, sem_ref, wgb_ref, wub_ref, wdb_ref):
    b = pl.program_id(0)
    bm = MOE_BM
    used = nused_ref[0]
    slot = b % 2

    def start(blk, sl):
        _gather_start(h_hbm, lambda r: rowtok_ref[blk * bm + r], buf_ref.at[sl], sem_ref.at[sl], bm)

    @pl.when(b == 0)
    def _():
        start(0, 0)

    changed = jnp.logical_or(b == 0, blke_ref[b] != blke_ref[jnp.maximum(b - 1, 0)])

    @pl.when(jnp.logical_and(b < used, changed))
    def _():
        wgb_ref[...] = wg_ref[...].astype(BF16)
        wub_ref[...] = wu_ref[...].astype(BF16)
        wdb_ref[...] = wd_ref[...].astype(BF16)

    @pl.when(b < used)
    def _():
        start(jnp.minimum(b + 1, used - 1), 1 - slot)
        _gather_wait(h_hbm, buf_ref.at[slot], sem_ref.at[slot], bm)
        xb = _from_row_tiles(buf_ref.at[slot], bm).astype(BF16)
        hid = _silu(jnp.dot(xb, wgb_ref[...], preferred_element_type=F32)) * jnp.dot(xb, wub_ref[...], preferred_element_type=F32)
        _to_row_tiles(y_ref, jnp.dot(hid.astype(BF16), wdb_ref[...], preferred_element_type=F32))

    @pl.when(b == used)
    def _():
        _gather_wait(h_hbm, buf_ref.at[slot], sem_ref.at[slot], bm)

    @pl.when(b >= used)
    def _():
        y_ref[...] = jnp.zeros_like(y_ref)


def moe_experts(h2_tiles, row_tok, blk_e, n_used, w_gate, w_up, w_down, layer):
    n_rows = row_tok.shape[0]
    bm = MOE_BM
    d, de = w_gate.shape[-2:]
    assert d == ROW_TILE * LANE
    wspec = lambda r, c: pl.BlockSpec((None, None, r, c), lambda b, be, nu, rt: (layer, be[b], 0, 0))
    grid_spec = pltpu.PrefetchScalarGridSpec(
        num_scalar_prefetch=3,
        grid=(n_rows // bm,),
        in_specs=[pl.BlockSpec(memory_space=pl.ANY), wspec(d, de), wspec(d, de), wspec(de, d)],
        out_specs=pl.BlockSpec((bm * ROW_TILE, LANE), lambda b, be, nu, rt: (b, 0)),
        scratch_shapes=[
            pltpu.VMEM((2, bm * ROW_TILE, LANE), F32),
            pltpu.SemaphoreType.DMA((2,)),
            pltpu.VMEM((d, de), BF16),
            pltpu.VMEM((d, de), BF16),
            pltpu.VMEM((de, d), BF16),
        ],
    )
    return pl.pallas_call(
        _expert_kernel,
        grid_spec=grid_spec,
        out_shape=jax.ShapeDtypeStruct((n_rows * ROW_TILE, LANE), F32),
        compiler_params=_cparams(("arbitrary",)),
        name="moe_experts",
    )(blk_e, n_used, row_tok, h2_tiles, w_gate, w_up, w_down)


COMB_TM = 256


def _combine_kernel(dest_ref, ys_hbm, x_ref, wt_ref, g2_ref, fw_ref, o_ref, buf_ref, sem_ref, *, n_tok, final_norm):
    i = pl.program_id(0)
    n = pl.num_programs(0)
    tm = x_ref.shape[0]
    slot = i % 2

    def start(tile, sl):
        for k in range(TOP_K):
            _gather_start(ys_hbm, lambda r: dest_ref[k * n_tok + tile * tm + r], buf_ref.at[sl, k], sem_ref.at[sl, k], tm)

    def wait(sl):
        for k in range(TOP_K):
            _gather_wait(ys_hbm, buf_ref.at[sl, k], sem_ref.at[sl, k], tm)

    @pl.when(i == 0)
    def _():
        start(0, 0)

    start(jnp.minimum(i + 1, n - 1), 1 - slot)
    wait(slot)
    wt = wt_ref[...]
    moe = (wt[:, 0:1] * _from_row_tiles(buf_ref.at[slot, 0], tm)
           + wt[:, 1:2] * _from_row_tiles(buf_ref.at[slot, 1], tm))
    out = x_ref[...] + g2_ref[...] * moe
    if final_norm:
        out = out * lax.rsqrt(jnp.mean(out * out, axis=-1, keepdims=True) + NORM_EPS) * fw_ref[...]
    o_ref[...] = out

    @pl.when(i == n - 1)
    def _():
        wait(1 - slot)


def moe_combine(ys, dest, x2, wts, g2, final_w, seq, final_norm):
    t, d = x2.shape
    tm = min(COMB_TM, seq)
    per_b = seq // tm
    grid_spec = pltpu.PrefetchScalarGridSpec(
        num_scalar_prefetch=1,
        grid=(t // tm,),
        in_specs=[
            pl.BlockSpec(memory_space=pl.ANY),
            pl.BlockSpec((tm, d), lambda i, ds: (i, 0)),
            pl.BlockSpec((tm, LANE), lambda i, ds: (i, 0)),
            pl.BlockSpec((None, 1, d), lambda i, ds: (i // per_b, 0, 0)),
            pl.BlockSpec((1, d), lambda i, ds: (0, 0)),
        ],
        out_specs=pl.BlockSpec((tm, d), lambda i, ds: (i, 0)),
        scratch_shapes=[pltpu.VMEM((2, TOP_K, tm * ROW_TILE, LANE), F32), pltpu.SemaphoreType.DMA((2, TOP_K))],
    )
    return pl.pallas_call(
        functools.partial(_combine_kernel, n_tok=t, final_norm=final_norm),
        grid_spec=grid_spec,
        out_shape=jax.ShapeDtypeStruct((t, d), F32),
        compiler_params=_cparams(("arbitrary",)),
        name="moe_combine",
    )(dest, ys, x2, wts, g2, final_w.reshape(1, d))


def kernel(x, c, w_ada, b_ada, norm_mix_w, norm_ffn_w, w_in, conv_w, conv_b, ssd_dt_bias, ssd_a_log, ssd_d, ssd_norm_w, diff_lambda, diff_subln_w, hgrn_lb_logits, hgrn_norm_w, w_merge, w_out, w_router, router_bias, w_expert_gate, w_expert_up, w_expert_down, final_norm_w):
    b, s, d = x.shape
    t = b * s
    depth = w_in.shape[0]
    mod = ada_modulation(c, w_ada, b_ada)
    lb_p = jax.nn.softmax(hgrn_lb_logits.astype(F32), axis=0)
    lower_bounds = jnp.cumsum(lb_p, axis=0) - lb_p[0]
    w_router_t = w_router.T
    x2 = x.reshape(t, d)
    for l in range(depth):
        sh1, sc1, g1, sh2, sc2, g2 = [mod[l, :, i * d:(i + 1) * d].reshape(b, 1, d) for i in range(6)]
        p2 = in_projection(x2, norm_mix_w[l], sc1, sh1, pad_w_in(w_in[l]), s)
        p3 = p2.reshape(b, s, -1)
        o_a = ssd_branch(p3, conv_w[l], conv_b[l], ssd_dt_bias[l], ssd_a_log[l], ssd_d[l], ssd_norm_w[l])
        o_b = dilated_branch(p3)
        o_c = diff_branch(p3, diff_lambda[l], diff_subln_w[l], l)
        o_d = hgrn_branch(p3, lower_bounds[l], hgrn_norm_w[l])
        flat = lambda o: o.reshape(t, BRANCH_W)
        x_mid, h2, idx, pos, wts, counts = merge_and_route(
            flat(o_a), flat(o_b), flat(o_c), flat(o_d), p2, x2, g1, w_merge[l].astype(BF16), w_out[l].astype(BF16),
            norm_ffn_w[l], sc2, sh2, w_router_t, router_bias, s)
        row_tok, dest, blk_e, n_used = moe_dispatch(idx, pos, counts)
        ys = moe_experts(h2, row_tok, blk_e, n_used, w_expert_gate, w_expert_up, w_expert_down, l)
        x2 = moe_combine(ys, dest, x_mid, wts, g2, final_norm_w, s, final_norm=(l == depth - 1))
    return x2.reshape(b, s, d)
```

```python
import functools
import math

import jax
import jax.numpy as jnp
from jax import lax
from jax.experimental import pallas as pl
from jax.experimental.pallas import tpu as pltpu

F32 = jnp.float32
BF16 = jnp.bfloat16

D_MODEL = 1024
DEPTH = 2
N_BRANCH = 4
BRANCH_W = 512

SSD_D_INNER = 512
SSD_HEAD_DIM = 64
SSD_HEADS = 8
SSD_GROUPS = 2
SSD_HPG = 4
SSD_STATE = 64
SSD_CONV = 4
SSD_CHUNK = 128
SSD_CONV_CH = 768

DIL_PAIRS = ((128, 1), (512, 4), (2048, 16))
DIL_HPG = 4
DIL_HEAD_DIM = 128
DIL_HEADS = 12
DIL_BLOCK = 128

DIFF_HEADS = 4
DIFF_HEAD_DIM = 64

HG_HEADS = 8
HG_KEY_DIM = 64
HG_VAL_DIM = 64
HG_CHUNK = 64

N_EXPERTS = 64
N_EXPERT_GROUPS = 8
EXPERTS_PER_GROUP = 8
TOP_K = 2
D_EXPERT = 256

NORM_EPS = 1e-6
MASK_VALUE = -1e30

LANE = 128
VMEM_LIMIT = 48 * 1024 * 1024

C_GATE = 0
C_Z = 32
C_XBC = 36
C_DT = 42
C_QB = 44
C_KB = 56
C_VB = 68
C_QC = 80
C_KC = 84
C_VC = 88
C_QD = 92
C_FD = 96
C_ID = 100
C_GD = 104
N_COLB = 108
D_IN_PAD = N_COLB * LANE
DT_PAD = 2 * LANE


def _cparams(sem):
    return pltpu.CompilerParams(dimension_semantics=sem, vmem_limit_bytes=VMEM_LIMIT)


def _silu(v):
    return v * jax.nn.sigmoid(v)


def _bdot(a, b):
    return jnp.dot(a.astype(BF16), b.astype(BF16), preferred_element_type=F32)


def _bdot_nt(a, b):
    return lax.dot_general(a.astype(BF16), b.astype(BF16), (((1,), (1,)), ((), ())),
                           preferred_element_type=F32)


def _bdot_tn(a, b):
    return lax.dot_general(a.astype(BF16), b.astype(BF16), (((0,), (0,)), ((), ())),
                           preferred_element_type=F32)


def _exact_rows_dot(m01, v):
    hi = v.astype(BF16)
    r1 = v - hi.astype(F32)
    mid = r1.astype(BF16)
    lo = (r1 - mid.astype(F32)).astype(BF16)
    m = m01.astype(BF16)
    return (jnp.dot(m, hi, preferred_element_type=F32) + jnp.dot(m, mid, preferred_element_type=F32)
            + jnp.dot(m, lo, preferred_element_type=F32))


def _ada_kernel(c_ref, w_ref, b_ref, o_ref):
    o_ref[...] = _bdot(_silu(c_ref[...]), w_ref[...]) + b_ref[...]


def ada_modulation(c, w_ada, b_ada):
    depth, d, n = w_ada.shape
    b = c.shape[0]
    bp = 8
    c_pad = jnp.zeros((bp, d), F32).at[:b].set(c)
    tn = 1536
    out = pl.pallas_call(
        _ada_kernel,
        grid=(depth, n // tn),
        in_specs=[
            pl.BlockSpec((bp, d), lambda l, j: (0, 0)),
            pl.BlockSpec((None, d, tn), lambda l, j: (l, 0, j)),
            pl.BlockSpec((None, 1, tn), lambda l, j: (l, 0, j)),
        ],
        out_specs=pl.BlockSpec((None, bp, tn), lambda l, j: (l, 0, j)),
        out_shape=jax.ShapeDtypeStruct((depth, bp, n), F32),
        compiler_params=_cparams(("arbitrary", "arbitrary")),
        name="ada_mod",
    )(c_pad, w_ada, b_ada.reshape(depth, 1, n))
    return out[:, :b]


INPROJ_TM = 2048
INPROJ_TN = 512


def _inproj_kernel(x_ref, nw_ref, sc_ref, sh_ref, w_ref, o_ref, h_ref):
    @pl.when(pl.program_id(1) == 0)
    def _():
        x = x_ref[...]
        y = x * lax.rsqrt(jnp.mean(x * x, axis=-1, keepdims=True) + NORM_EPS) * nw_ref[...]
        h_ref[...] = (y * (1.0 + sc_ref[...]) + sh_ref[...]).astype(BF16)

    o_ref[...] = jnp.dot(h_ref[...], w_ref[...], preferred_element_type=F32).astype(o_ref.dtype)


def in_projection(x2, norm_w, scale, shift, w_pad, seq):
    t, d = x2.shape
    n = w_pad.shape[1]
    tm = min(INPROJ_TM, seq)
    tn = INPROJ_TN
    per_b = seq // tm
    return pl.pallas_call(
        _inproj_kernel,
        grid=(t // tm, n // tn),
        in_specs=[
            pl.BlockSpec((tm, d), lambda i, j: (i, 0)),
            pl.BlockSpec((1, d), lambda i, j: (0, 0)),
            pl.BlockSpec((None, 1, d), lambda i, j: (i // per_b, 0, 0)),
            pl.BlockSpec((None, 1, d), lambda i, j: (i // per_b, 0, 0)),
            pl.BlockSpec((d, tn), lambda i, j: (0, j)),
        ],
        out_specs=pl.BlockSpec((tm, tn), lambda i, j: (i, j)),
        out_shape=jax.ShapeDtypeStruct((t, n), BF16),
        scratch_shapes=[pltpu.VMEM((tm, d), BF16)],
        compiler_params=_cparams(("arbitrary", "arbitrary")),
        name="in_proj",
    )(x2, norm_w.reshape(1, d), scale, shift, w_pad)


def pad_w_in(w_in_l):
    d = w_in_l.shape[0]
    o_dt = SSD_D_INNER + SSD_CONV_CH
    o_gate = w_in_l.shape[1] - N_BRANCH * D_MODEL
    return jnp.concatenate(
        [w_in_l[:, o_gate:], w_in_l[:, :o_dt + SSD_HEADS], jnp.zeros((d, DT_PAD - SSD_HEADS), w_in_l.dtype),
         w_in_l[:, o_dt + SSD_HEADS:o_gate]], axis=1).astype(BF16)


def _ssd_kernel(z_ref, xbc_ref, dt_ref, cw_ref, cb_ref, dtb_ref, alog_ref, dsk_ref, nw_ref, o_ref,
                xpad_ref, state_ref, y_ref):
    c = pl.program_id(1)
    L = SSD_CHUNK
    halo = 8

    @pl.when(c == 0)
    def _():
        xpad_ref[0:halo, :] = jnp.zeros((halo, SSD_CONV_CH), F32)
        state_ref[...] = jnp.zeros_like(state_ref)

    @pl.when(c > 0)
    def _():
        xpad_ref[0:halo, :] = xpad_ref[L:L + halo, :]

    z = z_ref[...].astype(F32)
    xpad_ref[halo:halo + L, :] = xbc_ref[...].astype(F32)

    conv = cb_ref[...]
    for j in range(SSD_CONV):
        off = halo - (SSD_CONV - 1) + j
        conv = conv + cw_ref[j:j + 1, :] * xpad_ref[off:off + L, :]
    xc = _silu(conv)
    xs = xc[:, 0:SSD_D_INNER]
    ns = SSD_GROUPS * SSD_STATE
    bm = xc[:, SSD_D_INNER:SSD_D_INNER + ns]
    cm = xc[:, SSD_D_INNER + ns:SSD_D_INNER + 2 * ns]

    dt = jax.nn.softplus(dt_ref[:, 0:LANE].astype(F32) + dtb_ref[...])
    a = -jnp.exp(alog_ref[...])
    adt = dt * a
    row = lax.broadcasted_iota(jnp.int32, (L, L), 0)
    col = lax.broadcasted_iota(jnp.int32, (L, L), 1)
    tril = row >= col
    a_cs = _exact_rows_dot(jnp.where(tril, 1.0, 0.0), adt)
    a_cs_t = a_cs.T
    tot = a_cs[L - 1:L, :]

    cbs = [_bdot_nt(cm[:, g * SSD_STATE:(g + 1) * SSD_STATE], bm[:, g * SSD_STATE:(g + 1) * SSD_STATE])
           for g in range(SSD_GROUPS)]
    for e in range(SSD_HEADS):
        g = e // SSD_HPG
        cs_col = a_cs[:, e:e + 1]
        cs_row = a_cs_t[e:e + 1, :]
        lmat = jnp.exp(jnp.where(tril, cs_col - cs_row, MASK_VALUE))
        xe = xs[:, e * SSD_HEAD_DIM:(e + 1) * SSD_HEAD_DIM]
        xdt = xe * dt[:, e:e + 1]
        bg = bm[:, g * SSD_STATE:(g + 1) * SSD_STATE]
        cg = cm[:, g * SSD_STATE:(g + 1) * SSD_STATE]
        st = state_ref[e]
        y = _bdot(cbs[g] * lmat, xdt)
        y = y + _bdot(cg, st) * jnp.exp(cs_col)
        tot_e = tot[:, e:e + 1]
        state_ref[e] = jnp.exp(tot_e) * st + _bdot_tn(bg * jnp.exp(tot_e - cs_col), xdt)
        y_ref[:, e * SSD_HEAD_DIM:(e + 1) * SSD_HEAD_DIM] = y + dsk_ref[:, e:e + 1] * xe

    yz = y_ref[...] * _silu(z)
    o_ref[...] = yz * lax.rsqrt(jnp.mean(yz * yz, axis=-1, keepdims=True) + NORM_EPS) * nw_ref[...]


def _pad_lanes(v, n=LANE):
    return jnp.zeros((1, n), F32).at[0, :v.shape[0]].set(v.astype(F32))


def ssd_branch(p3, conv_w, conv_b, dt_bias, a_log, d_skip, norm_w):
    b, s, _ = p3.shape
    L = SSD_CHUNK
    const = lambda shape: pl.BlockSpec(shape, lambda i, c: (0, 0))
    return pl.pallas_call(
        _ssd_kernel,
        grid=(b, s // L),
        in_specs=[
            pl.BlockSpec((None, L, SSD_D_INNER), lambda i, c: (i, c, C_Z * LANE // SSD_D_INNER)),
            pl.BlockSpec((None, L, SSD_CONV_CH), lambda i, c: (i, c, C_XBC * LANE // SSD_CONV_CH)),
            pl.BlockSpec((None, L, DT_PAD), lambda i, c: (i, c, C_DT * LANE // DT_PAD)),
            const((SSD_CONV, SSD_CONV_CH)),
            const((1, SSD_CONV_CH)),
            const((1, LANE)),
            const((1, LANE)),
            const((1, LANE)),
            const((1, SSD_D_INNER)),
        ],
        out_specs=pl.BlockSpec((None, L, SSD_D_INNER), lambda i, c: (i, c, 0)),
        out_shape=jax.ShapeDtypeStruct((b, s, SSD_D_INNER), F32),
        scratch_shapes=[
            pltpu.VMEM((L + 8, SSD_CONV_CH), F32),
            pltpu.VMEM((SSD_HEADS, SSD_STATE, SSD_HEAD_DIM), F32),
            pltpu.VMEM((L, SSD_D_INNER), F32),
        ],
        compiler_params=_cparams(("arbitrary", "arbitrary")),
        name="ssd",
    )(p3, p3, p3, conv_w, conv_b.reshape(1, -1), _pad_lanes(dt_bias), _pad_lanes(a_log), _pad_lanes(d_skip),
      norm_w.reshape(1, -1))


DIL_UNROLL = 4


def _dil_kernel(q_ref, k_ref, v_ref, o_ref, m_ref, l_ref, tmp_ref, qd_ref, kd_ref, vd_ref):
    g = pl.program_id(2)
    s_len = q_ref.shape[0]
    blk = DIL_BLOCK
    scale = DIL_HEAD_DIM ** -0.5

    @pl.when(g == 0)
    def _():
        m_ref[...] = jnp.full_like(m_ref, MASK_VALUE)
        l_ref[...] = jnp.zeros_like(l_ref)
        o_ref[...] = jnp.zeros_like(o_ref)

    def run_group(dil):
        sub = s_len // dil
        nblk = sub // blk

        if dil == 1:
            qd_ref[...] = (q_ref[...].astype(F32) * scale).astype(BF16)
            kd, vd = k_ref, v_ref
        else:
            for src, dst, mul in ((q_ref, qd_ref, scale), (k_ref, kd_ref, None), (v_ref, vd_ref, None)):
                x = src[...].astype(F32)
                tmp_ref[...] = x if mul is None else x * mul

                def gather(r, carry, dst=dst):
                    dst[pl.ds(pl.multiple_of(r * sub, blk), sub), :] = tmp_ref[pl.ds(r, sub, stride=dil), :].astype(BF16)
                    return carry

                lax.fori_loop(0, dil, gather, 0)
            kd, vd = kd_ref, vd_ref

        def unit(r, b, nkey):
            row0 = pl.multiple_of(r * sub + b * blk, blk)
            ks = pl.ds(pl.multiple_of(row0 - (nkey - blk), blk), nkey)
            base = r + dil * blk * b
            acc_rows = pl.ds(pl.multiple_of(base, blk), blk) if dil == 1 else pl.ds(base, blk, stride=dil)
            i = lax.broadcasted_iota(jnp.int32, (blk, nkey), 0) + (nkey - blk)
            j = lax.broadcasted_iota(jnp.int32, (blk, nkey), 1)
            ok = jnp.logical_and(j <= i, j >= i - blk)
            sc = lax.dot_general(qd_ref[pl.ds(row0, blk), :], kd[ks, :], (((1,), (1,)), ((), ())),
                                 preferred_element_type=F32)
            sc = jnp.where(ok, sc, MASK_VALUE)
            m_old = m_ref[acc_rows, :][:, 0:1]
            m_new = jnp.maximum(m_old, jnp.max(sc, axis=-1, keepdims=True))
            alpha = jnp.exp(m_old - m_new)
            p = jnp.exp(sc - m_new).astype(BF16)
            v_ext = jnp.concatenate([vd[ks, :], jnp.ones((nkey, LANE), BF16)], axis=1)
            pv = jnp.dot(p, v_ext, preferred_element_type=F32)
            o_ref[acc_rows, :] = alpha * o_ref[acc_rows, :] + pv[:, :LANE]
            l_ref[acc_rows, :] = alpha * l_ref[acc_rows, :] + pv[:, LANE:]
            m_ref[acc_rows, :] = jnp.broadcast_to(m_new, (blk, LANE))

        def first_unit(r, carry):
            unit(r, 0, blk)
            return carry

        def later_unit(u, carry):
            unit(u // (nblk - 1), 1 + u % (nblk - 1), 2 * blk)
            return carry

        lax.fori_loop(0, dil, first_unit, 0, unroll=min(dil, DIL_UNROLL))
        if nblk > 1:
            lax.fori_loop(0, dil * (nblk - 1), later_unit, 0, unroll=DIL_UNROLL)

    for gi, (window, dil) in enumerate(DIL_PAIRS):
        assert window // dil == blk

        @pl.when(g == gi)
        def _(dil=dil):
            run_group(dil)

    @pl.when(g == len(DIL_PAIRS) - 1)
    def _():
        o_ref[...] = o_ref[...] / l_ref[...]


def dilated_branch(p3):
    b, s, _ = p3.shape
    ng = len(DIL_PAIRS)
    spec = lambda c0: pl.BlockSpec((None, s, DIL_HEAD_DIM), lambda i, j, g: (i, 0, c0 + g * DIL_HPG + j))
    return pl.pallas_call(
        _dil_kernel,
        grid=(b, DIL_HPG, ng),
        in_specs=[spec(C_QB), spec(C_KB), spec(C_VB)],
        out_specs=pl.BlockSpec((None, s, DIL_HEAD_DIM), lambda i, j, g: (i, 0, j)),
        out_shape=jax.ShapeDtypeStruct((b, s, BRANCH_W), F32),
        scratch_shapes=[pltpu.VMEM((s, LANE), F32), pltpu.VMEM((s, LANE), F32), pltpu.VMEM((s, LANE), F32),
                        pltpu.VMEM((s, LANE), BF16), pltpu.VMEM((s, LANE), BF16), pltpu.VMEM((s, LANE), BF16)],
        compiler_params=_cparams(("arbitrary", "arbitrary", "arbitrary")),
        name="dilated",
    )(p3, p3, p3)


DIFF_TQ = 512
DIFF_TK = 1024
LOG2E = 1.4426950408889634


DIFF_ONES = 16


def _diff_kernel(lam_ref, q_ref, k_ref, v_ref, nw_ref, o_ref, vt_ref, m_ref, acc_ref, *, lam_init):
    qi = pl.program_id(2)
    tq, tk, dh = DIFF_TQ, DIFF_TK, DIFF_HEAD_DIM
    w = 2 * dh
    s_len = k_ref.shape[0]

    @pl.when(qi == 0)
    def _():
        for c0 in range(0, s_len, tk):
            vt_ref[0:w, c0:c0 + tk] = v_ref[c0:c0 + tk, :].astype(F32).T.astype(BF16)
        vt_ref[w:w + DIFF_ONES, :] = jnp.ones((DIFF_ONES, s_len), BF16)

    q_t = (q_ref[...].astype(F32) * (dh ** -0.5 * LOG2E)).T
    half = lax.broadcasted_iota(jnp.int32, (w, tq), 0) < dh
    q_sel = [jnp.where(half, q_t, 0.0).astype(BF16), jnp.where(half, 0.0, q_t).astype(BF16)]
    for t in range(2):
        m_ref[t] = jnp.full((8, tq), MASK_VALUE, F32)
        acc_ref[t] = jnp.zeros((w + DIFF_ONES, tq), F32)

    def scores(kstart, width, masked):
        kb = k_ref[pl.ds(kstart, width), :]
        out = [jnp.dot(kb, q_sel[t], preferred_element_type=F32) for t in range(2)]
        if masked:
            row = lax.broadcasted_iota(jnp.int32, (width, tq), 0)
            col = lax.broadcasted_iota(jnp.int32, (width, tq), 1)
            out = [jnp.where(row <= col, sc, MASK_VALUE) for sc in out]
        return out

    def over_causal_blocks(block):
        def main_step(j, carry):
            block(pl.multiple_of(j * tk, tk), tk, False)
            return carry

        per = tk // tq
        lax.fori_loop(0, qi // per, main_step, 0)
        for r in range(1, per):
            @pl.when(qi % per >= r)
            def _(r=r):
                block(pl.multiple_of((qi // per) * tk + (r - 1) * tq, tq), tq, False)
        block(pl.multiple_of(qi * tq, tq), tq, True)

    def max_block(kstart, width, masked):
        for t, sc in enumerate(scores(kstart, width, masked)):
            m_ref[t] = jnp.maximum(m_ref[t], jnp.max(sc.reshape(width // 8, 8, tq), axis=0))

    over_causal_blocks(max_block)
    m_fin = [jnp.max(m_ref[t], axis=0, keepdims=True) for t in range(2)]

    def acc_block(kstart, width, masked):
        vb = vt_ref[:, pl.ds(kstart, width)]
        for t, sc in enumerate(scores(kstart, width, masked)):
            p = jnp.exp2(sc - m_fin[t]).astype(BF16)
            acc_ref[t] += jnp.dot(vb, p, preferred_element_type=F32)

    over_causal_blocks(acc_block)

    lam_p = lam_ref[...]
    lam = (jnp.exp(jnp.sum(lam_p[0:1] * lam_p[1:2], axis=-1, keepdims=True))
           - jnp.exp(jnp.sum(lam_p[2:3] * lam_p[3:4], axis=-1, keepdims=True)) + lam_init)
    a0, a1 = acc_ref[0], acc_ref[1]
    o_t = a0[0:w] / a0[w:w + 1] - lam * (a1[0:w] / a1[w:w + 1])
    o = o_t.T
    o = o * lax.rsqrt(jnp.mean(o * o, axis=-1, keepdims=True) + NORM_EPS) * nw_ref[...]
    o_ref[...] = o * (1.0 - lam_init)


def diff_branch(p3, diff_lambda, subln_w, layer):
    b, s, _ = p3.shape
    lam_init = 0.8 - 0.6 * math.exp(-0.3 * layer)
    w = 2 * DIFF_HEAD_DIM
    return pl.pallas_call(
        functools.partial(_diff_kernel, lam_init=lam_init),
        grid=(b, DIFF_HEADS, s // DIFF_TQ),
        in_specs=[
            pl.BlockSpec((4, DIFF_HEAD_DIM), lambda i, h, t: (0, 0)),
            pl.BlockSpec((None, DIFF_TQ, w), lambda i, h, t: (i, t, C_QC + h)),
            pl.BlockSpec((None, s, w), lambda i, h, t: (i, 0, C_KC + h)),
            pl.BlockSpec((None, s, w), lambda i, h, t: (i, 0, C_VC + h)),
            pl.BlockSpec((1, w), lambda i, h, t: (0, 0)),
        ],
        out_specs=pl.BlockSpec((None, DIFF_TQ, w), lambda i, h, t: (i, t, h)),
        out_shape=jax.ShapeDtypeStruct((b, s, BRANCH_W), F32),
        scratch_shapes=[pltpu.VMEM((w + DIFF_ONES, s), BF16), pltpu.VMEM((2, 8, DIFF_TQ), F32),
                        pltpu.VMEM((2, w + DIFF_ONES, DIFF_TQ), F32)],
        compiler_params=_cparams(("arbitrary", "arbitrary", "arbitrary")),
        name="diff_attn",
    )(diff_lambda, p3, p3, p3, subln_w.reshape(1, w))


HG_LEVELS = 6
HG_STEP_CHUNKS = 2


def _hgrn_tables():
    import numpy as np
    c = HG_CHUNK
    i = np.arange(c)[:, None]
    j = np.arange(c)[None, :]
    tril = (j <= i).astype(np.float32)
    masks = []
    for lv in range(HG_LEVELS):
        sz = 1 << lv
        m = ((i // (2 * sz) == j // (2 * sz)) & (i // sz == j // sz + 1)).astype(np.float32)
        masks.append(np.concatenate([m, m], axis=0))
    lane = np.arange(LANE)
    same_head = (lane[:, None] // HG_KEY_DIM == lane[None, :] // HG_KEY_DIM).astype(np.float32)
    return tril, np.stack(masks, axis=0), same_head


def _hgrn_kernel(q_ref, f_ref, i_ref, g_ref, lb_ref, tri_ref, msk_ref, sh_ref, nw_ref, o_ref, state_ref, gpad_ref):
    @pl.when(pl.program_id(1) == 0)
    def _():
        state_ref[...] = jnp.zeros_like(state_ref)

    for ci in range(HG_STEP_CHUNKS):
        rows = slice(ci * HG_CHUNK, (ci + 1) * HG_CHUNK)
        _hgrn_chunk(q_ref[rows, :], f_ref[rows, :], i_ref[rows, :], g_ref, lb_ref, tri_ref, msk_ref, sh_ref, nw_ref,
                    o_ref, state_ref, gpad_ref, rows)


def _hgrn_chunk(q_in, f_in, i_in, g_ref, lb_ref, tri_ref, msk_ref, sh_ref, nw_ref, o_ref, state_ref, gpad_ref, rows):
    cs = HG_CHUNK
    nl = HG_LEVELS
    w = q_in.shape[1]
    sub_rows = 8

    lb = lb_ref[...]
    f_gate = lb + (1.0 - lb) * jax.nn.sigmoid(f_in.astype(F32))
    log_f = jnp.log(f_gate)
    k_in = 1.0 - f_gate
    q = _silu(q_in.astype(F32))
    v = i_in.astype(F32)

    hi, lo = _split_bf16(log_f)
    tri = tri_ref[...]
    g = jnp.dot(tri, hi, preferred_element_type=F32) + jnp.dot(tri, lo, preferred_element_type=F32)
    gpad_ref[0:sub_rows, :] = jnp.zeros((sub_rows, w), F32)
    gpad_ref[sub_rows:sub_rows + cs, :] = g
    g_last = g[cs - 1:cs, :]

    def g_row(r):
        return jnp.broadcast_to(gpad_ref[sub_rows + r:sub_rows + r + 1, :], (sub_rows, w))

    sub = lax.broadcasted_iota(jnp.int32, (sub_rows, w), 0)
    tiles = [[] for _ in range(nl)]
    for t in range(cs // sub_rows):
        r0 = t * sub_rows
        gt = g[r0:r0 + sub_rows]
        prev = gpad_ref[r0 + sub_rows - 1:r0 + 2 * sub_rows - 1, :]
        tiles[0].append(jnp.where(sub % 2 == 1, gt - prev, 0.0))
        c1 = jnp.where(sub < 4, g_row(r0 + 1), g_row(r0 + 5))
        tiles[1].append(jnp.where((sub // 2) % 2 == 1, gt - c1, c1 - gt))
        c2 = g_row(r0 + 3)
        tiles[2].append(jnp.where(sub >= 4, gt - c2, c2 - gt))
        for lv in range(3, nl):
            span = (1 << lv) // sub_rows
            mid = (t // (2 * span)) * 2 * span + span
            cm = g_row(mid * sub_rows - 1)
            tiles[lv].append(gt - cm if (t // span) % 2 == 1 else cm - gt)
    dec = [jnp.exp(jnp.concatenate(tl, axis=0)) for tl in tiles]
    q_lv = [q * d for d in dec]
    k_lv = [k_in * d for d in dec]
    q_full = q * jnp.exp(g)
    k_full = k_in * jnp.exp(g_last - g)
    decay_row = jnp.exp(g_last)

    same_head = sh_ref[...]
    same_head_b = same_head.astype(BF16)
    first = lax.broadcasted_iota(jnp.int32, (cs, LANE), 1) < HG_KEY_DIM
    for p in range(w // LANE):
        sl = slice(p * LANE, (p + 1) * LANE)
        attn = jnp.zeros((2 * cs, cs), F32)
        for lv in range(nl):
            qp = q_lv[lv][:, sl]
            lhs = jnp.concatenate([jnp.where(first, qp, 0.0), jnp.where(first, 0.0, qp)], axis=0)
            attn = attn + msk_ref[lv] * _bdot_nt(lhs, k_lv[lv][:, sl])
        vp = v[:, sl]
        vpb = vp.astype(BF16)
        r = jnp.dot(attn.astype(BF16), vpb, preferred_element_type=F32)
        o = jnp.where(first, r[0:cs], r[cs:2 * cs])
        o = o + jnp.dot((q[:, sl] * k_in[:, sl]).astype(BF16), same_head_b, preferred_element_type=F32) * vp
        st = state_ref[p]
        o = o + _bdot_nt(q_full[:, sl], st)
        state_ref[p] = decay_row[:, sl] * st + same_head * _bdot_tn(vpb, k_full[:, sl])
        ms = jnp.dot((o * o).astype(BF16), same_head_b, preferred_element_type=F32) * (1.0 / HG_VAL_DIM)
        o = o * lax.rsqrt(ms + NORM_EPS) * nw_ref[...]
        o_ref[rows, sl] = o * _silu(g_ref[rows, sl].astype(F32))


def hgrn_branch(p3, lower_bound, norm_w):
    b, s, _ = p3.shape
    cs = HG_CHUNK * HG_STEP_CHUNKS
    w = HG_HEADS * HG_KEY_DIM
    assert HG_KEY_DIM == HG_VAL_DIM and LANE == 2 * HG_KEY_DIM
    tril, masks, same_head = _hgrn_tables()
    seg = lambda c0: pl.BlockSpec((None, cs, w), lambda i, c: (i, c, c0 * LANE // w))
    return pl.pallas_call(
        _hgrn_kernel,
        grid=(b, s // cs),
        in_specs=[
            seg(C_QD), seg(C_FD), seg(C_ID), seg(C_GD),
            pl.BlockSpec((1, w), lambda i, c: (0, 0)),
            pl.BlockSpec(tril.shape, lambda i, c: (0, 0)),
            pl.BlockSpec(masks.shape, lambda i, c: (0, 0, 0)),
            pl.BlockSpec(same_head.shape, lambda i, c: (0, 0)),
            pl.BlockSpec((1, LANE), lambda i, c: (0, 0)),
        ],
        out_specs=pl.BlockSpec((None, cs, w), lambda i, c: (i, c, 0)),
        out_shape=jax.ShapeDtypeStruct((b, s, BRANCH_W), F32),
        scratch_shapes=[pltpu.VMEM((w // LANE, LANE, LANE), F32), pltpu.VMEM((HG_CHUNK + 8, w), F32)],
        compiler_params=_cparams(("arbitrary", "arbitrary")),
        name="hgrn2",
    )(p3, p3, p3, p3, lower_bound.reshape(1, w).astype(F32), jnp.asarray(tril, BF16), jnp.asarray(masks, F32),
      jnp.asarray(same_head, F32), jnp.tile(norm_w.astype(F32), LANE // HG_VAL_DIM).reshape(1, LANE))


MERGE_TM = 512


def _split_bf16(v):
    hi = v.astype(BF16)
    return hi, (v - hi.astype(F32)).astype(BF16)


def _first_index_of_max(vals, iota, n):
    top = jnp.max(vals, axis=0, keepdims=True)
    idx = jnp.min(jnp.where(vals == top, iota, n), axis=0, keepdims=True)
    return top, idx


def _merge_kernel(oa_ref, ob_ref, oc_ref, od_ref, gl_ref, x_ref, g1_ref, wm_ref, wo_ref,
                  nw_ref, sc_ref, sh_ref, wr_ref, rb_ref,
                  xo_ref, h2_ref, idx_ref, pos_ref, wt_ref, cnt_ref, carry_ref):
    step = pl.program_id(0)
    tm = x_ref.shape[0]
    d = D_MODEL

    acc = jnp.zeros((tm, d), F32)
    for n, o_ref in enumerate((oa_ref, ob_ref, oc_ref, od_ref)):
        proj = jnp.dot(o_ref[...].astype(BF16), wm_ref[n], preferred_element_type=F32)
        acc = acc + jax.nn.sigmoid(gl_ref[:, n * d:(n + 1) * d].astype(F32)) * proj
    mix = jnp.dot(acc.astype(BF16), wo_ref[...], preferred_element_type=F32)
    x_new = x_ref[...] + g1_ref[...] * mix
    xo_ref[...] = x_new

    y = x_new * lax.rsqrt(jnp.mean(x_new * x_new, axis=-1, keepdims=True) + NORM_EPS) * nw_ref[...]
    h2 = y * (1.0 + sc_ref[...]) + sh_ref[...]
    _to_row_tiles(h2_ref, h2)

    h_hi, h_lo = _split_bf16(h2)
    w_hi, w_lo = _split_bf16(wr_ref[...])
    nt = lambda a, b: lax.dot_general(a, b, (((1,), (1,)), ((), ())), preferred_element_type=F32)
    logits = nt(w_hi, h_hi) + nt(w_hi, h_lo) + nt(w_lo, h_hi)
    scores = jax.nn.sigmoid(logits)
    sel = scores + rb_ref[...]

    ne, ng, pg = N_EXPERTS, N_EXPERT_GROUPS, EXPERTS_PER_GROUP
    iota_g = lax.broadcasted_iota(jnp.int32, (pg, tm), 0)
    best_score = None
    best_group = None
    for g in range(ng):
        xg = sel[g * pg:(g + 1) * pg, :]
        top1, i1 = _first_index_of_max(xg, iota_g, pg)
        top2 = jnp.max(jnp.where(iota_g == i1, -jnp.inf, xg), axis=0, keepdims=True)
        gs = top1 + top2
        if g == 0:
            best_score, best_group = gs, jnp.zeros((1, tm), jnp.int32)
        else:
            better = gs > best_score
            best_score = jnp.where(better, gs, best_score)
            best_group = jnp.where(better, g, best_group)

    iota_e = lax.broadcasted_iota(jnp.int32, (ne, tm), 0)
    masked = jnp.where(iota_e // pg == best_group, sel, MASK_VALUE)
    _, e1 = _first_index_of_max(masked, iota_e, ne)
    oh1 = iota_e == e1
    _, e2 = _first_index_of_max(jnp.where(oh1, MASK_VALUE, masked), iota_e, ne)
    oh2 = iota_e == e2
    w1 = jnp.sum(jnp.where(oh1, scores, 0.0), axis=0, keepdims=True)
    w2 = jnp.sum(jnp.where(oh2, scores, 0.0), axis=0, keepdims=True)
    wsum = w1 + w2
    w1 = w1 / wsum
    w2 = w2 / wsum

    @pl.when(step == 0)
    def _():
        carry_ref[...] = jnp.zeros_like(carry_ref)

    f1 = jnp.where(oh1, 1.0, 0.0)
    f2 = jnp.where(oh2, 1.0, 0.0)
    both = f1 + f2
    r_i = lax.broadcasted_iota(jnp.int32, (tm, tm), 0)
    c_i = lax.broadcasted_iota(jnp.int32, (tm, tm), 1)
    before = jnp.where(r_i < c_i, 1.0, 0.0).astype(BF16)
    rank = jnp.dot(both.astype(BF16), before, preferred_element_type=F32) + carry_ref[:, 0:1]
    p1 = jnp.sum(f1 * rank, axis=0, keepdims=True)
    p2 = jnp.sum(f2 * rank, axis=0, keepdims=True)
    carry_new = carry_ref[...] + jnp.sum(both, axis=1, keepdims=True)
    carry_ref[...] = carry_new
    cnt_ref[...] = carry_new.astype(jnp.int32)

    idx_ref[...] = jnp.concatenate([e1, e2], axis=0)
    pos_ref[...] = jnp.concatenate([p1, p2], axis=0).astype(jnp.int32)
    wt_rows = jnp.concatenate([w1, w2, jnp.zeros((LANE - 2, tm), F32)], axis=0)
    wt_ref[...] = wt_rows.T


def merge_and_route(o_a, o_b, o_c, o_d, p2, x2, g1, w_merge_bf, w_out_bf, norm_w, sc2, sh2, w_router_t, router_bias, seq):
    t, d = x2.shape
    tm = min(MERGE_TM, seq)
    per_b = seq // tm
    gw = N_BRANCH * d
    tok = lambda w: pl.BlockSpec((tm, w), lambda i: (i, 0))
    bat = pl.BlockSpec((None, 1, d), lambda i: (i // per_b, 0, 0))
    ne = N_EXPERTS
    return pl.pallas_call(
        _merge_kernel,
        grid=(t // tm,),
        in_specs=[
            tok(BRANCH_W), tok(BRANCH_W), tok(BRANCH_W), tok(BRANCH_W),
            pl.BlockSpec((tm, gw), lambda i: (i, C_GATE * LANE // gw)),
            tok(d), bat,
            pl.BlockSpec((N_BRANCH, BRANCH_W, d), lambda i: (0, 0, 0)),
            pl.BlockSpec((d, d), lambda i: (0, 0)),
            pl.BlockSpec((1, d), lambda i: (0, 0)),
            bat, bat,
            pl.BlockSpec((ne, d), lambda i: (0, 0)),
            pl.BlockSpec((ne, 1), lambda i: (0, 0)),
        ],
        out_specs=[
            tok(d), pl.BlockSpec((tm * ROW_TILE, LANE), lambda i: (i, 0)),
            pl.BlockSpec((TOP_K, tm), lambda i: (0, i)),
            pl.BlockSpec((TOP_K, tm), lambda i: (0, i)),
            tok(LANE),
            pl.BlockSpec((ne, LANE), lambda i: (0, 0)),
        ],
        out_shape=[
            jax.ShapeDtypeStruct((t, d), F32),
            jax.ShapeDtypeStruct((t * ROW_TILE, LANE), F32),
            jax.ShapeDtypeStruct((TOP_K, t), jnp.int32),
            jax.ShapeDtypeStruct((TOP_K, t), jnp.int32),
            jax.ShapeDtypeStruct((t, LANE), F32),
            jax.ShapeDtypeStruct((ne, LANE), jnp.int32),
        ],
        scratch_shapes=[pltpu.VMEM((ne, LANE), F32)],
        compiler_params=_cparams(("arbitrary",)),
        name="merge_route",
    )(o_a, o_b, o_c, o_d, p2, x2, g1, w_merge_bf, w_out_bf, norm_w.reshape(1, d), sc2, sh2,
      w_router_t, router_bias.reshape(ne, 1))


MOE_BM = 256


def _plan_kernel(idx_ref, pos_ref, cnt_ref, dest_ref, blke_ref, nused_ref):
    bm = MOE_BM
    ne = N_EXPERTS
    cnt = cnt_ref[...].astype(F32)
    padded = jnp.floor((cnt + (bm - 1)) * (1.0 / bm)) * bm
    r = lax.broadcasted_iota(jnp.int32, (ne, ne), 0)
    c = lax.broadcasted_iota(jnp.int32, (ne, ne), 1)
    pstart = _exact_rows_dot(jnp.where(c < r, 1.0, 0.0), padded)
    pend = pstart + padded

    idx = idx_ref[...]
    base = jnp.zeros(idx.shape, F32)
    for e in range(ne):
        base = jnp.where(idx == e, pstart[e:e + 1, 0:1], base)
    dest_ref[...] = base.astype(jnp.int32) + pos_ref[...]

    nbp = blke_ref.shape[1]
    blk_start = (lax.broadcasted_iota(jnp.int32, (ne, nbp), 1) * bm).astype(F32)
    done = jnp.sum(jnp.where(pend[:, 0:1] <= blk_start, 1.0, 0.0), axis=0, keepdims=True)
    blke_ref[...] = jnp.minimum(done, ne - 1.0).astype(jnp.int32)
    nused_ref[...] = (pend[ne - 1:ne, :] * (1.0 / bm)).astype(jnp.int32)


def moe_plan(idx, pos, counts):
    n_tok = idx.shape[1]
    n_rows = TOP_K * n_tok + N_EXPERTS * MOE_BM
    n_blocks = n_rows // MOE_BM
    nbp = -(-n_blocks // LANE) * LANE
    dest, blk_e, n_used = pl.pallas_call(
        _plan_kernel,
        out_shape=[
            jax.ShapeDtypeStruct((TOP_K, n_tok), jnp.int32),
            jax.ShapeDtypeStruct((1, nbp), jnp.int32),
            jax.ShapeDtypeStruct((1, LANE), jnp.int32),
        ],
        compiler_params=pltpu.CompilerParams(vmem_limit_bytes=VMEM_LIMIT),
        name="moe_plan",
    )(idx, pos, counts)
    return dest.reshape(-1), blk_e[0, :n_blocks], n_used[0, :1]


ROW_TILE = 8
DISPATCH_TM = 512


def _to_row_tiles(ref, x):
    rows = x.shape[0]
    for j in range(ROW_TILE):
        ref[pl.ds(j, rows, stride=ROW_TILE), :] = x[:, j * LANE:(j + 1) * LANE]


def _from_row_tiles(ref, rows):
    return jnp.concatenate([ref[pl.ds(j, rows, stride=ROW_TILE), :] for j in range(ROW_TILE)], axis=1)


def _tile_rows(row):
    return pl.ds(pl.multiple_of(row * ROW_TILE, ROW_TILE), ROW_TILE)


def _dispatch_kernel(dest_ref, h_ref, xs_init_hbm, xs_hbm, sem_ref, *, n_tok):
    del xs_init_hbm
    i = pl.program_id(0)
    tm = h_ref.shape[0] // ROW_TILE

    def copy(r, k):
        row = dest_ref[k * n_tok + i * tm + r]
        return pltpu.make_async_copy(h_ref.at[pl.ds(r * ROW_TILE, ROW_TILE), :], xs_hbm.at[_tile_rows(row), :], sem_ref.at[k])

    for r in range(tm):
        for k in range(TOP_K):
            copy(r, k).start(priority=k)
    for r in range(tm):
        for k in range(TOP_K):
            copy(r, k).wait()


def moe_dispatch(h2_tiles, dest, xs_init):
    n_tok = dest.shape[0] // TOP_K
    tm = min(DISPATCH_TM, n_tok)
    grid_spec = pltpu.PrefetchScalarGridSpec(
        num_scalar_prefetch=1,
        grid=(n_tok // tm,),
        in_specs=[pl.BlockSpec((tm * ROW_TILE, LANE), lambda i, ds: (i, 0)), pl.BlockSpec(memory_space=pl.ANY)],
        out_specs=pl.BlockSpec(memory_space=pl.ANY),
        scratch_shapes=[pltpu.SemaphoreType.DMA((TOP_K,))],
    )
    return pl.pallas_call(
        functools.partial(_dispatch_kernel, n_tok=n_tok),
        grid_spec=grid_spec,
        out_shape=jax.ShapeDtypeStruct(xs_init.shape, F32),
        input_output_aliases={2: 0},
        compiler_params=_cparams(("arbitrary",)),
        name="moe_dispatch",
    )(dest, h2_tiles, xs_init)


def _expert_kernel(blke_ref, nused_ref, x_ref, wg_ref, wu_ref, wd_ref, y_ref, wgb_ref, wub_ref, wdb_ref):
    b = pl.program_id(0)
    bm = MOE_BM
    used = nused_ref[0]
    changed = jnp.logical_or(b == 0, blke_ref[b] != blke_ref[jnp.maximum(b - 1, 0)])

    @pl.when(jnp.logical_and(b < used, changed))
    def _():
        wgb_ref[...] = wg_ref[...].astype(BF16)
        wub_ref[...] = wu_ref[...].astype(BF16)
        wdb_ref[...] = wd_ref[...].astype(BF16)

    @pl.when(b < used)
    def _():
        xb = _from_row_tiles(x_ref, bm).astype(BF16)
        hid = _silu(jnp.dot(xb, wgb_ref[...], preferred_element_type=F32)) * jnp.dot(xb, wub_ref[...], preferred_element_type=F32)
        _to_row_tiles(y_ref, jnp.dot(hid.astype(BF16), wdb_ref[...], preferred_element_type=F32))

    @pl.when(b >= used)
    def _():
        y_ref[...] = jnp.zeros_like(y_ref)


def moe_experts(xs, blk_e, n_used, w_gate, w_up, w_down, layer):
    bm = MOE_BM
    n_rows = xs.shape[0] // ROW_TILE
    d, de = w_gate.shape[-2:]
    assert d == ROW_TILE * LANE
    wspec = lambda r, c: pl.BlockSpec((None, None, r, c), lambda b, be, nu: (layer, be[b], 0, 0))
    rows = pl.BlockSpec((bm * ROW_TILE, LANE), lambda b, be, nu: (b, 0))
    grid_spec = pltpu.PrefetchScalarGridSpec(
        num_scalar_prefetch=2,
        grid=(n_rows // bm,),
        in_specs=[rows, wspec(d, de), wspec(d, de), wspec(de, d)],
        out_specs=rows,
        scratch_shapes=[pltpu.VMEM((d, de), BF16), pltpu.VMEM((d, de), BF16), pltpu.VMEM((de, d), BF16)],
    )
    return pl.pallas_call(
        _expert_kernel,
        grid_spec=grid_spec,
        out_shape=jax.ShapeDtypeStruct(xs.shape, F32),
        compiler_params=_cparams(("arbitrary",)),
        name="moe_experts",
    )(blk_e, n_used, xs, w_gate, w_up, w_down)


COMB_TM = 256


def _tile_copy(src_hbm, row, dst, r, sem):
    return pltpu.make_async_copy(src_hbm.at[_tile_rows(row), :], dst.at[pl.ds(r * ROW_TILE, ROW_TILE), :], sem)


def _gather_start(src_hbm, row_of, dst, sem, n):
    for r in range(n):
        _tile_copy(src_hbm, row_of(r), dst, r, sem).start(priority=r % 2)


def _gather_wait(src_hbm, dst, sem, n):
    for r in range(n):
        _tile_copy(src_hbm, 0, dst, r, sem).wait()


def _combine_kernel(dest_ref, ys_hbm, x_ref, wt_ref, g2_ref, fw_ref, o_ref, buf_ref, sem_ref, *, n_tok, final_norm):
    i = pl.program_id(0)
    n = pl.num_programs(0)
    tm = x_ref.shape[0]
    slot = i % 2

    def start(tile, sl):
        for k in range(TOP_K):
            _gather_start(ys_hbm, lambda r: dest_ref[k * n_tok + tile * tm + r], buf_ref.at[sl, k], sem_ref.at[sl, k], tm)

    def wait(sl):
        for k in range(TOP_K):
            _gather_wait(ys_hbm, buf_ref.at[sl, k], sem_ref.at[sl, k], tm)

    @pl.when(i == 0)
    def _():
        start(0, 0)

    start(jnp.minimum(i + 1, n - 1), 1 - slot)
    wait(slot)
    wt = wt_ref[...]
    moe = (wt[:, 0:1] * _from_row_tiles(buf_ref.at[slot, 0], tm)
           + wt[:, 1:2] * _from_row_tiles(buf_ref.at[slot, 1], tm))
    out = x_ref[...] + g2_ref[...] * moe
    if final_norm:
        out = out * lax.rsqrt(jnp.mean(out * out, axis=-1, keepdims=True) + NORM_EPS) * fw_ref[...]
    o_ref[...] = out

    @pl.when(i == n - 1)
    def _():
        wait(1 - slot)


def moe_combine(ys, dest, x2, wts, g2, final_w, seq, final_norm):
    t, d = x2.shape
    tm = min(COMB_TM, seq)
    per_b = seq // tm
    grid_spec = pltpu.PrefetchScalarGridSpec(
        num_scalar_prefetch=1,
        grid=(t // tm,),
        in_specs=[
            pl.BlockSpec(memory_space=pl.ANY),
            pl.BlockSpec((tm, d), lambda i, ds: (i, 0)),
            pl.BlockSpec((tm, LANE), lambda i, ds: (i, 0)),
            pl.BlockSpec((None, 1, d), lambda i, ds: (i // per_b, 0, 0)),
            pl.BlockSpec((1, d), lambda i, ds: (0, 0)),
        ],
        out_specs=pl.BlockSpec((tm, d), lambda i, ds: (i, 0)),
        scratch_shapes=[pltpu.VMEM((2, TOP_K, tm * ROW_TILE, LANE), F32), pltpu.SemaphoreType.DMA((2, TOP_K))],
    )
    return pl.pallas_call(
        functools.partial(_combine_kernel, n_tok=t, final_norm=final_norm),
        grid_spec=grid_spec,
        out_shape=jax.ShapeDtypeStruct((t, d), F32),
        compiler_params=_cparams(("arbitrary",)),
        name="moe_combine",
    )(dest, ys, x2, wts, g2, final_w.reshape(1, d))


def kernel(x, c, w_ada, b_ada, norm_mix_w, norm_ffn_w, w_in, conv_w, conv_b, ssd_dt_bias, ssd_a_log, ssd_d, ssd_norm_w, diff_lambda, diff_subln_w, hgrn_lb_logits, hgrn_norm_w, w_merge, w_out, w_router, router_bias, w_expert_gate, w_expert_up, w_expert_down, final_norm_w):
    b, s, d = x.shape
    t = b * s
    depth = w_in.shape[0]
    mod = ada_modulation(c, w_ada, b_ada)
    lb_p = jax.nn.softmax(hgrn_lb_logits.astype(F32), axis=0)
    lower_bounds = jnp.cumsum(lb_p, axis=0) - lb_p[0]
    w_router_t = w_router.T
    x2 = x.reshape(t, d)
    ys = jnp.zeros(((TOP_K * t + N_EXPERTS * MOE_BM) * ROW_TILE, LANE), F32)
    for l in range(depth):
        sh1, sc1, g1, sh2, sc2, g2 = [mod[l, :, i * d:(i + 1) * d].reshape(b, 1, d) for i in range(6)]
        p2 = in_projection(x2, norm_mix_w[l], sc1, sh1, pad_w_in(w_in[l]), s)
        p3 = p2.reshape(b, s, -1)
        o_a = ssd_branch(p3, conv_w[l], conv_b[l], ssd_dt_bias[l], ssd_a_log[l], ssd_d[l], ssd_norm_w[l])
        o_b = dilated_branch(p3)
        o_c = diff_branch(p3, diff_lambda[l], diff_subln_w[l], l)
        o_d = hgrn_branch(p3, lower_bounds[l], hgrn_norm_w[l])
        flat = lambda o: o.reshape(t, BRANCH_W)
        x_mid, h2, idx, pos, wts, counts = merge_and_route(
            flat(o_a), flat(o_b), flat(o_c), flat(o_d), p2, x2, g1, w_merge[l].astype(BF16), w_out[l].astype(BF16),
            norm_ffn_w[l], sc2, sh2, w_router_t, router_bias, s)
        dest, blk_e, n_used = moe_plan(idx, pos, counts)
        xs = moe_dispatch(h2, dest, ys)
        ys = moe_experts(xs, blk_e, n_used, w_expert_gate, w_expert_up, w_expert_down, l)
        x2 = moe_combine(ys, dest, x_mid, wts, g2, final_norm_w, s, final_norm=(l == depth - 1))
    return x2.reshape(b, s, d)
```

```python
import functools
import math

import jax
import jax.numpy as jnp
from jax import lax
from jax.experimental import pallas as pl
from jax.experimental.pallas import tpu as pltpu

F32 = jnp.float32
BF16 = jnp.bfloat16

D_MODEL = 1024
DEPTH = 2
N_BRANCH = 4
BRANCH_W = 512

SSD_D_INNER = 512
SSD_HEAD_DIM = 64
SSD_HEADS = 8
SSD_GROUPS = 2
SSD_HPG = 4
SSD_STATE = 64
SSD_CONV = 4
SSD_CHUNK = 128
SSD_STEP_CHUNKS = 2
SSD_CONV_CH = 768

DIL_PAIRS = ((128, 1), (512, 4), (2048, 16))
DIL_HPG = 4
DIL_HEAD_DIM = 128
DIL_HEADS = 12
DIL_BLOCK = 128

DIFF_HEADS = 4
DIFF_HEAD_DIM = 64

HG_HEADS = 8
HG_KEY_DIM = 64
HG_VAL_DIM = 64
HG_CHUNK = 64

N_EXPERTS = 64
N_EXPERT_GROUPS = 8
EXPERTS_PER_GROUP = 8
TOP_K = 2
D_EXPERT = 256

NORM_EPS = 1e-6
MASK_VALUE = -1e30

LANE = 128
VMEM_LIMIT = 48 * 1024 * 1024

C_GATE = 0
C_Z = 32
C_XBC = 36
C_DT = 42
C_QB = 44
C_KB = 56
C_VB = 68
C_QC = 80
C_KC = 84
C_VC = 88
C_QD = 92
C_FD = 96
C_ID = 100
C_GD = 104
N_COLB = 108
D_IN_PAD = N_COLB * LANE
DT_PAD = 2 * LANE


def _cparams(sem):
    return pltpu.CompilerParams(dimension_semantics=sem, vmem_limit_bytes=VMEM_LIMIT)


def _aligned(x, m):
    return x if isinstance(x, int) else pl.multiple_of(x, m)


def _silu(v):
    return v * jax.nn.sigmoid(v)


def _bdot(a, b):
    return jnp.dot(a.astype(BF16), b.astype(BF16), preferred_element_type=F32)


def _bdot_nt(a, b):
    return lax.dot_general(a.astype(BF16), b.astype(BF16), (((1,), (1,)), ((), ())),
                           preferred_element_type=F32)


def _bdot_tn(a, b):
    return lax.dot_general(a.astype(BF16), b.astype(BF16), (((0,), (0,)), ((), ())),
                           preferred_element_type=F32)


def _exact_rows_dot(m01, v):
    hi = v.astype(BF16)
    r1 = v - hi.astype(F32)
    mid = r1.astype(BF16)
    lo = (r1 - mid.astype(F32)).astype(BF16)
    m = m01.astype(BF16)
    return (jnp.dot(m, hi, preferred_element_type=F32) + jnp.dot(m, mid, preferred_element_type=F32)
            + jnp.dot(m, lo, preferred_element_type=F32))


def _ada_kernel(c_ref, w_ref, b_ref, o_ref):
    o_ref[...] = _bdot(_silu(c_ref[...]), w_ref[...]) + b_ref[...]


def ada_modulation(c, w_ada, b_ada):
    depth, d, n = w_ada.shape
    b = c.shape[0]
    bp = 8
    c_pad = jnp.zeros((bp, d), F32).at[:b].set(c)
    tn = 1536
    out = pl.pallas_call(
        _ada_kernel,
        grid=(depth, n // tn),
        in_specs=[
            pl.BlockSpec((bp, d), lambda l, j: (0, 0)),
            pl.BlockSpec((None, d, tn), lambda l, j: (l, 0, j)),
            pl.BlockSpec((None, 1, tn), lambda l, j: (l, 0, j)),
        ],
        out_specs=pl.BlockSpec((None, bp, tn), lambda l, j: (l, 0, j)),
        out_shape=jax.ShapeDtypeStruct((depth, bp, n), F32),
        compiler_params=_cparams(("arbitrary", "arbitrary")),
        name="ada_mod",
    )(c_pad, w_ada, b_ada.reshape(depth, 1, n))
    return out[:, :b]


INPROJ_TM = 2048
INPROJ_TN = 512


def _inproj_kernel(x_ref, nw_ref, sc_ref, sh_ref, w_ref, o_ref, h_ref):
    @pl.when(pl.program_id(1) == 0)
    def _():
        x = x_ref[...]
        y = x * lax.rsqrt(jnp.mean(x * x, axis=-1, keepdims=True) + NORM_EPS) * nw_ref[...]
        h_ref[...] = (y * (1.0 + sc_ref[...]) + sh_ref[...]).astype(BF16)

    o_ref[...] = jnp.dot(h_ref[...], w_ref[...], preferred_element_type=F32).astype(o_ref.dtype)


def in_projection(x2, norm_w, scale, shift, w_pad, seq):
    t, d = x2.shape
    n = w_pad.shape[1]
    tm = min(INPROJ_TM, seq)
    tn = INPROJ_TN
    per_b = seq // tm
    return pl.pallas_call(
        _inproj_kernel,
        grid=(t // tm, n // tn),
        in_specs=[
            pl.BlockSpec((tm, d), lambda i, j: (i, 0)),
            pl.BlockSpec((1, d), lambda i, j: (0, 0)),
            pl.BlockSpec((None, 1, d), lambda i, j: (i // per_b, 0, 0)),
            pl.BlockSpec((None, 1, d), lambda i, j: (i // per_b, 0, 0)),
            pl.BlockSpec((d, tn), lambda i, j: (0, j)),
        ],
        out_specs=pl.BlockSpec((tm, tn), lambda i, j: (i, j)),
        out_shape=jax.ShapeDtypeStruct((t, n), BF16),
        scratch_shapes=[pltpu.VMEM((tm, d), BF16)],
        compiler_params=_cparams(("arbitrary", "arbitrary")),
        name="in_proj",
    )(x2, norm_w.reshape(1, d), scale, shift, w_pad)


def pad_w_in(w_in_l):
    d = w_in_l.shape[0]
    o_dt = SSD_D_INNER + SSD_CONV_CH
    o_gate = w_in_l.shape[1] - N_BRANCH * D_MODEL
    return jnp.concatenate(
        [w_in_l[:, o_gate:], w_in_l[:, :o_dt + SSD_HEADS], jnp.zeros((d, DT_PAD - SSD_HEADS), w_in_l.dtype),
         w_in_l[:, o_dt + SSD_HEADS:o_gate]], axis=1).astype(BF16)


def _ssd_kernel(z_ref, xbc_ref, dt_ref, cw_ref, cb_ref, dtb_ref, alog_ref, dsk_ref, nw_ref, o_ref,
                xpad_ref, state_ref, y_ref):
    c = pl.program_id(1)
    L = SSD_CHUNK
    halo = 8

    @pl.when(c == 0)
    def _():
        xpad_ref[L:L + halo, :] = jnp.zeros((halo, SSD_CONV_CH), F32)
        state_ref[...] = jnp.zeros_like(state_ref)

    for ci in range(SSD_STEP_CHUNKS):
        rows = slice(ci * L, (ci + 1) * L)
        _ssd_chunk(z_ref[rows, :], xbc_ref[rows, :], dt_ref[rows, :], cw_ref, cb_ref, dtb_ref, alog_ref, dsk_ref, nw_ref,
                   o_ref, xpad_ref, state_ref, y_ref, rows)


def _ssd_chunk(z_in, xbc_in, dt_in, cw_ref, cb_ref, dtb_ref, alog_ref, dsk_ref, nw_ref, o_ref, xpad_ref, state_ref, y_ref, rows):
    L = SSD_CHUNK
    halo = 8
    xpad_ref[0:halo, :] = xpad_ref[L:L + halo, :]
    z = z_in.astype(F32)
    xpad_ref[halo:halo + L, :] = xbc_in.astype(F32)

    conv = cb_ref[...]
    for j in range(SSD_CONV):
        off = halo - (SSD_CONV - 1) + j
        conv = conv + cw_ref[j:j + 1, :] * xpad_ref[off:off + L, :]
    xc = _silu(conv)
    xs = xc[:, 0:SSD_D_INNER]
    ns = SSD_GROUPS * SSD_STATE
    bm = xc[:, SSD_D_INNER:SSD_D_INNER + ns]
    cm = xc[:, SSD_D_INNER + ns:SSD_D_INNER + 2 * ns]

    dt = jax.nn.softplus(dt_in[:, 0:LANE].astype(F32) + dtb_ref[...])
    a = -jnp.exp(alog_ref[...])
    adt = dt * a
    row = lax.broadcasted_iota(jnp.int32, (L, L), 0)
    col = lax.broadcasted_iota(jnp.int32, (L, L), 1)
    tril = row >= col
    a_cs = _exact_rows_dot(jnp.where(tril, 1.0, 0.0), adt)
    a_cs_t = a_cs.T
    tot = a_cs[L - 1:L, :]

    cbs = [_bdot_nt(cm[:, g * SSD_STATE:(g + 1) * SSD_STATE], bm[:, g * SSD_STATE:(g + 1) * SSD_STATE])
           for g in range(SSD_GROUPS)]
    for e in range(SSD_HEADS):
        g = e // SSD_HPG
        cs_col = a_cs[:, e:e + 1]
        cs_row = a_cs_t[e:e + 1, :]
        lmat = jnp.exp(jnp.where(tril, cs_col - cs_row, MASK_VALUE))
        xe = xs[:, e * SSD_HEAD_DIM:(e + 1) * SSD_HEAD_DIM]
        xdt = xe * dt[:, e:e + 1]
        bg = bm[:, g * SSD_STATE:(g + 1) * SSD_STATE]
        cg = cm[:, g * SSD_STATE:(g + 1) * SSD_STATE]
        st = state_ref[e]
        y = _bdot(cbs[g] * lmat, xdt)
        y = y + _bdot(cg, st) * jnp.exp(cs_col)
        tot_e = tot[:, e:e + 1]
        state_ref[e] = jnp.exp(tot_e) * st + _bdot_tn(bg * jnp.exp(tot_e - cs_col), xdt)
        y_ref[:, e * SSD_HEAD_DIM:(e + 1) * SSD_HEAD_DIM] = y + dsk_ref[:, e:e + 1] * xe

    yz = y_ref[...] * _silu(z)
    o_ref[rows, :] = yz * lax.rsqrt(jnp.mean(yz * yz, axis=-1, keepdims=True) + NORM_EPS) * nw_ref[...]


def _pad_lanes(v, n=LANE):
    return jnp.zeros((1, n), F32).at[0, :v.shape[0]].set(v.astype(F32))


def ssd_branch(p3, conv_w, conv_b, dt_bias, a_log, d_skip, norm_w):
    b, s, _ = p3.shape
    L = SSD_CHUNK * SSD_STEP_CHUNKS
    const = lambda shape: pl.BlockSpec(shape, lambda i, c: (0, 0))
    return pl.pallas_call(
        _ssd_kernel,
        grid=(b, s // L),
        in_specs=[
            pl.BlockSpec((None, L, SSD_D_INNER), lambda i, c: (i, c, C_Z * LANE // SSD_D_INNER)),
            pl.BlockSpec((None, L, SSD_CONV_CH), lambda i, c: (i, c, C_XBC * LANE // SSD_CONV_CH)),
            pl.BlockSpec((None, L, DT_PAD), lambda i, c: (i, c, C_DT * LANE // DT_PAD)),
            const((SSD_CONV, SSD_CONV_CH)),
            const((1, SSD_CONV_CH)),
            const((1, LANE)),
            const((1, LANE)),
            const((1, LANE)),
            const((1, SSD_D_INNER)),
        ],
        out_specs=pl.BlockSpec((None, L, SSD_D_INNER), lambda i, c: (i, c, 0)),
        out_shape=jax.ShapeDtypeStruct((b, s, SSD_D_INNER), F32),
        scratch_shapes=[
            pltpu.VMEM((SSD_CHUNK + 8, SSD_CONV_CH), F32),
            pltpu.VMEM((SSD_HEADS, SSD_STATE, SSD_HEAD_DIM), F32),
            pltpu.VMEM((SSD_CHUNK, SSD_D_INNER), F32),
        ],
        compiler_params=_cparams(("arbitrary", "arbitrary")),
        name="ssd",
    )(p3, p3, p3, conv_w, conv_b.reshape(1, -1), _pad_lanes(dt_bias), _pad_lanes(a_log), _pad_lanes(d_skip),
      norm_w.reshape(1, -1))


DIL_UNROLL = 4


def _dil_kernel(q_ref, k_ref, v_ref, o_ref, m_ref, l_ref, tmp_ref, qd_ref, kd_ref, vd_ref):
    g = pl.program_id(2)
    s_len = q_ref.shape[0]
    blk = DIL_BLOCK
    scale = DIL_HEAD_DIM ** -0.5

    @pl.when(g == 0)
    def _():
        m_ref[...] = jnp.full_like(m_ref, MASK_VALUE)
        l_ref[...] = jnp.zeros_like(l_ref)
        o_ref[...] = jnp.zeros_like(o_ref)

    def run_group(dil):
        sub = s_len // dil
        nblk = sub // blk

        if dil == 1:
            qd_ref[...] = (q_ref[...].astype(F32) * scale).astype(BF16)
            kd, vd = k_ref, v_ref
        else:
            for src, dst, mul in ((q_ref, qd_ref, scale), (k_ref, kd_ref, None), (v_ref, vd_ref, None)):
                x = src[...].astype(F32)
                tmp_ref[...] = x if mul is None else x * mul

                def gather(r, carry, dst=dst):
                    dst[pl.ds(pl.multiple_of(r * sub, blk), sub), :] = tmp_ref[pl.ds(r, sub, stride=dil), :].astype(BF16)
                    return carry

                lax.fori_loop(0, dil, gather, 0)
            kd, vd = kd_ref, vd_ref

        def block_softmax(r, b, nkey):
            row0 = _aligned(r * sub + b * blk, blk)
            ks = pl.ds(_aligned(row0 - (nkey - blk), blk), nkey)
            i = lax.broadcasted_iota(jnp.int32, (blk, nkey), 0) + (nkey - blk)
            j = lax.broadcasted_iota(jnp.int32, (blk, nkey), 1)
            ok = jnp.logical_and(j <= i, j >= i - blk)
            sc = lax.dot_general(qd_ref[pl.ds(row0, blk), :], kd[ks, :], (((1,), (1,)), ((), ())),
                                 preferred_element_type=F32)
            sc = jnp.where(ok, sc, MASK_VALUE)
            m_blk = jnp.max(sc, axis=-1, keepdims=True)
            p = jnp.exp(sc - m_blk).astype(BF16)
            v_ext = jnp.concatenate([vd[ks, :], jnp.ones((nkey, LANE), BF16)], axis=1)
            return m_blk, jnp.dot(p, v_ext, preferred_element_type=F32)

        def merge(r, b, m_blk, pv):
            base = r + dil * blk * b
            acc_rows = pl.ds(_aligned(base, blk), blk) if dil == 1 else pl.ds(base, blk, stride=dil)
            m_old = m_ref[acc_rows, :]
            m_blk = jnp.broadcast_to(m_blk, (blk, LANE))
            m_new = jnp.maximum(m_old, m_blk)
            a_old = jnp.exp(m_old - m_new)
            a_blk = jnp.exp(m_blk - m_new)
            o_ref[acc_rows, :] = a_old * o_ref[acc_rows, :] + a_blk * pv[:, :LANE]
            l_ref[acc_rows, :] = a_old * l_ref[acc_rows, :] + a_blk * pv[:, LANE:]
            m_ref[acc_rows, :] = m_new

        def run_units(n, where, nkey):
            def group(units):
                parts = [block_softmax(r, b, nkey) for r, b in units]
                for (r, b), (m_blk, pv) in zip(units, parts):
                    merge(r, b, m_blk, pv)

            def body(i, carry):
                group([where(i * DIL_UNROLL + j) for j in range(DIL_UNROLL)])
                return carry

            lax.fori_loop(0, n // DIL_UNROLL, body, 0)
            if n % DIL_UNROLL:
                group([where(n - n % DIL_UNROLL + j) for j in range(n % DIL_UNROLL)])

        run_units(dil, lambda u: (u, 0), blk)
        if nblk > 1:
            run_units(dil * (nblk - 1), lambda u: (u // (nblk - 1), 1 + u % (nblk - 1)), 2 * blk)

    for gi, (window, dil) in enumerate(DIL_PAIRS):
        assert window // dil == blk

        @pl.when(g == gi)
        def _(dil=dil):
            run_group(dil)

    @pl.when(g == len(DIL_PAIRS) - 1)
    def _():
        o_ref[...] = o_ref[...] / l_ref[...]


def dilated_branch(p3):
    b, s, _ = p3.shape
    ng = len(DIL_PAIRS)
    spec = lambda c0: pl.BlockSpec((None, s, DIL_HEAD_DIM), lambda i, j, g: (i, 0, c0 + g * DIL_HPG + j))
    return pl.pallas_call(
        _dil_kernel,
        grid=(b, DIL_HPG, ng),
        in_specs=[spec(C_QB), spec(C_KB), spec(C_VB)],
        out_specs=pl.BlockSpec((None, s, DIL_HEAD_DIM), lambda i, j, g: (i, 0, j)),
        out_shape=jax.ShapeDtypeStruct((b, s, BRANCH_W), F32),
        scratch_shapes=[pltpu.VMEM((s, LANE), F32), pltpu.VMEM((s, LANE), F32), pltpu.VMEM((s, LANE), F32),
                        pltpu.VMEM((s, LANE), BF16), pltpu.VMEM((s, LANE), BF16), pltpu.VMEM((s, LANE), BF16)],
        compiler_params=_cparams(("arbitrary", "arbitrary", "arbitrary")),
        name="dilated",
    )(p3, p3, p3)


DIFF_TQ = 512
DIFF_TK = 1024
DIFF_DIAG = 512
LOG2E = 1.4426950408889634


DIFF_ONES = 16


def _diff_kernel(lam_ref, q_ref, k_ref, v_ref, nw_ref, o_ref, vt_ref, m_ref, acc_ref, *, lam_init):
    qi = pl.program_id(2)
    tq, tk, dh = DIFF_TQ, DIFF_TK, DIFF_HEAD_DIM
    w = 2 * dh
    s_len = k_ref.shape[0]

    @pl.when(qi == 0)
    def _():
        for c0 in range(0, s_len, tk):
            vt_ref[0:w, c0:c0 + tk] = v_ref[c0:c0 + tk, :].astype(F32).T.astype(BF16)
        vt_ref[w:w + DIFF_ONES, :] = jnp.ones((DIFF_ONES, s_len), BF16)

    q_t = (q_ref[...].astype(F32) * (dh ** -0.5 * LOG2E)).T
    half = lax.broadcasted_iota(jnp.int32, (w, tq), 0) < dh
    q_sel = [jnp.where(half, q_t, 0.0).astype(BF16), jnp.where(half, 0.0, q_t).astype(BF16)]
    for t in range(2):
        m_ref[t] = jnp.full((8, tq), MASK_VALUE, F32)
        acc_ref[t] = jnp.zeros((w + DIFF_ONES, tq), F32)

    def block(kstart, width, qlo=None):
        c0 = qlo or 0
        nq = tq - c0
        kb = k_ref[pl.ds(kstart, width), :]
        vb = vt_ref[:, pl.ds(kstart, width)]
        scores = [jnp.dot(kb, q_sel[t][:, c0:], preferred_element_type=F32) for t in range(2)]
        if qlo is not None:
            row = lax.broadcasted_iota(jnp.int32, (width, nq), 0)
            col = lax.broadcasted_iota(jnp.int32, (width, nq), 1)
            scores = [jnp.where(row <= col, sc, MASK_VALUE) for sc in scores]
        m_blks = [jnp.max(jnp.max(sc.reshape(width // 8, 8, nq), axis=0), axis=0, keepdims=True) for sc in scores]
        pvs = [jnp.dot(vb, jnp.exp2(sc - m_blk).astype(BF16), preferred_element_type=F32)
               for sc, m_blk in zip(scores, m_blks)]
        for t, (m_blk, pv) in enumerate(zip(m_blks, pvs)):
            m_old = m_ref[t, 0:1, c0:]
            m_new = jnp.maximum(m_old, m_blk)
            acc_ref[t, :, c0:] = acc_ref[t, :, c0:] * jnp.exp2(m_old - m_new) + pv * jnp.exp2(m_blk - m_new)
            m_ref[t, :, c0:] = jnp.broadcast_to(m_new, (8, nq))

    def main_step(j, carry):
        block(pl.multiple_of(j * tk, tk), tk)
        return carry

    per = tk // tq
    lax.fori_loop(0, qi // per, main_step, 0)
    for r in range(1, per):
        @pl.when(qi % per >= r)
        def _(r=r):
            block(pl.multiple_of((qi // per) * tk + (r - 1) * tq, tq), tq)
    for c0 in range(0, tq, DIFF_DIAG):
        block(pl.multiple_of(qi * tq + c0, DIFF_DIAG), DIFF_DIAG, qlo=c0)

    lam_p = lam_ref[...]
    lam = (jnp.exp(jnp.sum(lam_p[0:1] * lam_p[1:2], axis=-1, keepdims=True))
           - jnp.exp(jnp.sum(lam_p[2:3] * lam_p[3:4], axis=-1, keepdims=True)) + lam_init)
    a0, a1 = acc_ref[0], acc_ref[1]
    o_t = a0[0:w] / a0[w:w + 1] - lam * (a1[0:w] / a1[w:w + 1])
    o = o_t.T
    o = o * lax.rsqrt(jnp.mean(o * o, axis=-1, keepdims=True) + NORM_EPS) * nw_ref[...]
    o_ref[...] = o * (1.0 - lam_init)


def diff_branch(p3, diff_lambda, subln_w, layer):
    b, s, _ = p3.shape
    lam_init = 0.8 - 0.6 * math.exp(-0.3 * layer)
    w = 2 * DIFF_HEAD_DIM
    return pl.pallas_call(
        functools.partial(_diff_kernel, lam_init=lam_init),
        grid=(b, DIFF_HEADS, s // DIFF_TQ),
        in_specs=[
            pl.BlockSpec((4, DIFF_HEAD_DIM), lambda i, h, t: (0, 0)),
            pl.BlockSpec((None, DIFF_TQ, w), lambda i, h, t: (i, t, C_QC + h)),
            pl.BlockSpec((None, s, w), lambda i, h, t: (i, 0, C_KC + h)),
            pl.BlockSpec((None, s, w), lambda i, h, t: (i, 0, C_VC + h)),
            pl.BlockSpec((1, w), lambda i, h, t: (0, 0)),
        ],
        out_specs=pl.BlockSpec((None, DIFF_TQ, w), lambda i, h, t: (i, t, h)),
        out_shape=jax.ShapeDtypeStruct((b, s, BRANCH_W), F32),
        scratch_shapes=[pltpu.VMEM((w + DIFF_ONES, s), BF16), pltpu.VMEM((2, 8, DIFF_TQ), F32),
                        pltpu.VMEM((2, w + DIFF_ONES, DIFF_TQ), F32)],
        compiler_params=_cparams(("arbitrary", "arbitrary", "arbitrary")),
        name="diff_attn",
    )(diff_lambda, p3, p3, p3, subln_w.reshape(1, w))


HG_LEVELS = 6
HG_STEP_CHUNKS = 2


def _hgrn_tables():
    import numpy as np
    c = HG_CHUNK
    i = np.arange(c)[:, None]
    j = np.arange(c)[None, :]
    tril = (j <= i).astype(np.float32)
    masks = []
    for lv in range(HG_LEVELS):
        sz = 1 << lv
        m = ((i // (2 * sz) == j // (2 * sz)) & (i // sz == j // sz + 1)).astype(np.float32)
        masks.append(np.concatenate([m, m], axis=0))
    lane = np.arange(LANE)
    same_head = (lane[:, None] // HG_KEY_DIM == lane[None, :] // HG_KEY_DIM).astype(np.float32)
    return tril, np.stack(masks, axis=0), same_head


def _hgrn_kernel(q_ref, f_ref, i_ref, g_ref, lb_ref, tri_ref, msk_ref, sh_ref, nw_ref, o_ref, state_ref, gpad_ref):
    @pl.when(pl.program_id(1) == 0)
    def _():
        state_ref[...] = jnp.zeros_like(state_ref)

    for ci in range(HG_STEP_CHUNKS):
        rows = slice(ci * HG_CHUNK, (ci + 1) * HG_CHUNK)
        _hgrn_chunk(q_ref[rows, :], f_ref[rows, :], i_ref[rows, :], g_ref, lb_ref, tri_ref, msk_ref, sh_ref, nw_ref,
                    o_ref, state_ref, gpad_ref, rows)


def _hgrn_chunk(q_in, f_in, i_in, g_ref, lb_ref, tri_ref, msk_ref, sh_ref, nw_ref, o_ref, state_ref, gpad_ref, rows):
    cs = HG_CHUNK
    nl = HG_LEVELS
    w = q_in.shape[1]
    sub_rows = 8

    lb = lb_ref[...]
    f_gate = lb + (1.0 - lb) * jax.nn.sigmoid(f_in.astype(F32))
    log_f = jnp.log(f_gate)
    k_in = 1.0 - f_gate
    q = _silu(q_in.astype(F32))
    v = i_in.astype(F32)

    hi, lo = _split_bf16(log_f)
    tri = tri_ref[...]
    g = jnp.dot(tri, hi, preferred_element_type=F32) + jnp.dot(tri, lo, preferred_element_type=F32)
    gpad_ref[0:sub_rows, :] = jnp.zeros((sub_rows, w), F32)
    gpad_ref[sub_rows:sub_rows + cs, :] = g
    g_last = g[cs - 1:cs, :]

    def g_row(r):
        return jnp.broadcast_to(gpad_ref[sub_rows + r:sub_rows + r + 1, :], (sub_rows, w))

    sub = lax.broadcasted_iota(jnp.int32, (sub_rows, w), 0)
    tiles = [[] for _ in range(nl)]
    for t in range(cs // sub_rows):
        r0 = t * sub_rows
        gt = g[r0:r0 + sub_rows]
        prev = gpad_ref[r0 + sub_rows - 1:r0 + 2 * sub_rows - 1, :]
        tiles[0].append(jnp.where(sub % 2 == 1, gt - prev, 0.0))
        c1 = jnp.where(sub < 4, g_row(r0 + 1), g_row(r0 + 5))
        tiles[1].append(jnp.where((sub // 2) % 2 == 1, gt - c1, c1 - gt))
        c2 = g_row(r0 + 3)
        tiles[2].append(jnp.where(sub >= 4, gt - c2, c2 - gt))
        for lv in range(3, nl):
            span = (1 << lv) // sub_rows
            mid = (t // (2 * span)) * 2 * span + span
            cm = g_row(mid * sub_rows - 1)
            tiles[lv].append(gt - cm if (t // span) % 2 == 1 else cm - gt)
    dec = [jnp.exp(jnp.concatenate(tl, axis=0)) for tl in tiles]
    q_lv = [q * d for d in dec]
    k_lv = [k_in * d for d in dec]
    q_full = q * jnp.exp(g)
    k_full = k_in * jnp.exp(g_last - g)
    decay_row = jnp.exp(g_last)

    same_head = sh_ref[...]
    same_head_b = same_head.astype(BF16)
    first = lax.broadcasted_iota(jnp.int32, (cs, LANE), 1) < HG_KEY_DIM
    for p in range(w // LANE):
        sl = slice(p * LANE, (p + 1) * LANE)
        attn = jnp.zeros((2 * cs, cs), F32)
        for lv in range(nl):
            qp = q_lv[lv][:, sl]
            lhs = jnp.concatenate([jnp.where(first, qp, 0.0), jnp.where(first, 0.0, qp)], axis=0)
            attn = attn + msk_ref[lv] * _bdot_nt(lhs, k_lv[lv][:, sl])
        vp = v[:, sl]
        vpb = vp.astype(BF16)
        r = jnp.dot(attn.astype(BF16), vpb, preferred_element_type=F32)
        o = jnp.where(first, r[0:cs], r[cs:2 * cs])
        o = o + jnp.dot((q[:, sl] * k_in[:, sl]).astype(BF16), same_head_b, preferred_element_type=F32) * vp
        st = state_ref[p]
        o = o + _bdot_nt(q_full[:, sl], st)
        state_ref[p] = decay_row[:, sl] * st + same_head * _bdot_tn(vpb, k_full[:, sl])
        ms = jnp.dot((o * o).astype(BF16), same_head_b, preferred_element_type=F32) * (1.0 / HG_VAL_DIM)
        o = o * lax.rsqrt(ms + NORM_EPS) * nw_ref[...]
        o_ref[rows, sl] = o * _silu(g_ref[rows, sl].astype(F32))


def hgrn_branch(p3, lower_bound, norm_w):
    b, s, _ = p3.shape
    cs = HG_CHUNK * HG_STEP_CHUNKS
    w = HG_HEADS * HG_KEY_DIM
    assert HG_KEY_DIM == HG_VAL_DIM and LANE == 2 * HG_KEY_DIM
    tril, masks, same_head = _hgrn_tables()
    seg = lambda c0: pl.BlockSpec((None, cs, w), lambda i, c: (i, c, c0 * LANE // w))
    return pl.pallas_call(
        _hgrn_kernel,
        grid=(b, s // cs),
        in_specs=[
            seg(C_QD), seg(C_FD), seg(C_ID), seg(C_GD),
            pl.BlockSpec((1, w), lambda i, c: (0, 0)),
            pl.BlockSpec(tril.shape, lambda i, c: (0, 0)),
            pl.BlockSpec(masks.shape, lambda i, c: (0, 0, 0)),
            pl.BlockSpec(same_head.shape, lambda i, c: (0, 0)),
            pl.BlockSpec((1, LANE), lambda i, c: (0, 0)),
        ],
        out_specs=pl.BlockSpec((None, cs, w), lambda i, c: (i, c, 0)),
        out_shape=jax.ShapeDtypeStruct((b, s, BRANCH_W), F32),
        scratch_shapes=[pltpu.VMEM((w // LANE, LANE, LANE), F32), pltpu.VMEM((HG_CHUNK + 8, w), F32)],
        compiler_params=_cparams(("arbitrary", "arbitrary")),
        name="hgrn2",
    )(p3, p3, p3, p3, lower_bound.reshape(1, w).astype(F32), jnp.asarray(tril, BF16), jnp.asarray(masks, F32),
      jnp.asarray(same_head, F32), jnp.tile(norm_w.astype(F32), LANE // HG_VAL_DIM).reshape(1, LANE))


MERGE_TM = 512


def _split_bf16(v):
    hi = v.astype(BF16)
    return hi, (v - hi.astype(F32)).astype(BF16)


def _first_index_of_max(vals, iota, n):
    top = jnp.max(vals, axis=0, keepdims=True)
    idx = jnp.min(jnp.where(vals == top, iota, n), axis=0, keepdims=True)
    return top, idx


def _merge_kernel(oa_ref, ob_ref, oc_ref, od_ref, gl_ref, x_ref, g1_ref, wm_ref, wo_ref,
                  nw_ref, sc_ref, sh_ref, wr_ref, rb_ref,
                  xo_ref, h2_ref, idx_ref, pos_ref, wt_ref, cnt_ref, carry_ref):
    step = pl.program_id(0)
    tm = x_ref.shape[0]
    d = D_MODEL

    acc = jnp.zeros((tm, d), F32)
    for n, o_ref in enumerate((oa_ref, ob_ref, oc_ref, od_ref)):
        proj = jnp.dot(o_ref[...].astype(BF16), wm_ref[n], preferred_element_type=F32)
        acc = acc + jax.nn.sigmoid(gl_ref[:, n * d:(n + 1) * d].astype(F32)) * proj
    mix = jnp.dot(acc.astype(BF16), wo_ref[...], preferred_element_type=F32)
    x_new = x_ref[...] + g1_ref[...] * mix
    xo_ref[...] = x_new

    y = x_new * lax.rsqrt(jnp.mean(x_new * x_new, axis=-1, keepdims=True) + NORM_EPS) * nw_ref[...]
    h2 = y * (1.0 + sc_ref[...]) + sh_ref[...]
    _to_row_tiles(h2_ref, h2)

    h_hi, h_lo = _split_bf16(h2)
    w_hi, w_lo = _split_bf16(wr_ref[...])
    nt = lambda a, b: lax.dot_general(a, b, (((1,), (1,)), ((), ())), preferred_element_type=F32)
    logits = nt(w_hi, h_hi) + nt(w_hi, h_lo) + nt(w_lo, h_hi)
    scores = jax.nn.sigmoid(logits)
    sel = scores + rb_ref[...]

    ne, ng, pg = N_EXPERTS, N_EXPERT_GROUPS, EXPERTS_PER_GROUP
    iota_g = lax.broadcasted_iota(jnp.int32, (pg, tm), 0)
    best_score = None
    best_group = None
    for g in range(ng):
        xg = sel[g * pg:(g + 1) * pg, :]
        top1, i1 = _first_index_of_max(xg, iota_g, pg)
        top2 = jnp.max(jnp.where(iota_g == i1, -jnp.inf, xg), axis=0, keepdims=True)
        gs = top1 + top2
        if g == 0:
            best_score, best_group = gs, jnp.zeros((1, tm), jnp.int32)
        else:
            better = gs > best_score
            best_score = jnp.where(better, gs, best_score)
            best_group = jnp.where(better, g, best_group)

    iota_e = lax.broadcasted_iota(jnp.int32, (ne, tm), 0)
    masked = jnp.where(iota_e // pg == best_group, sel, MASK_VALUE)
    _, e1 = _first_index_of_max(masked, iota_e, ne)
    oh1 = iota_e == e1
    _, e2 = _first_index_of_max(jnp.where(oh1, MASK_VALUE, masked), iota_e, ne)
    oh2 = iota_e == e2
    w1 = jnp.sum(jnp.where(oh1, scores, 0.0), axis=0, keepdims=True)
    w2 = jnp.sum(jnp.where(oh2, scores, 0.0), axis=0, keepdims=True)
    wsum = w1 + w2
    w1 = w1 / wsum
    w2 = w2 / wsum

    @pl.when(step == 0)
    def _():
        carry_ref[...] = jnp.zeros_like(carry_ref)

    f1 = jnp.where(oh1, 1.0, 0.0)
    f2 = jnp.where(oh2, 1.0, 0.0)
    both = f1 + f2
    r_i = lax.broadcasted_iota(jnp.int32, (tm, tm), 0)
    c_i = lax.broadcasted_iota(jnp.int32, (tm, tm), 1)
    before = jnp.where(r_i < c_i, 1.0, 0.0).astype(BF16)
    rank = jnp.dot(both.astype(BF16), before, preferred_element_type=F32) + carry_ref[:, 0:1]
    p1 = jnp.sum(f1 * rank, axis=0, keepdims=True)
    p2 = jnp.sum(f2 * rank, axis=0, keepdims=True)
    carry_new = carry_ref[...] + jnp.sum(both, axis=1, keepdims=True)
    carry_ref[...] = carry_new
    cnt_ref[...] = carry_new.astype(jnp.int32)

    idx_ref[...] = jnp.concatenate([e1, e2], axis=0)
    pos_ref[...] = jnp.concatenate([p1, p2], axis=0).astype(jnp.int32)
    wt_rows = jnp.concatenate([w1, w2, jnp.zeros((LANE - 2, tm), F32)], axis=0)
    wt_ref[...] = wt_rows.T


def merge_and_route(o_a, o_b, o_c, o_d, p2, x2, g1, w_merge_bf, w_out_bf, norm_w, sc2, sh2, w_router_t, router_bias, seq):
    t, d = x2.shape
    tm = min(MERGE_TM, seq)
    per_b = seq // tm
    gw = N_BRANCH * d
    tok = lambda w: pl.BlockSpec((tm, w), lambda i: (i, 0))
    bat = pl.BlockSpec((None, 1, d), lambda i: (i // per_b, 0, 0))
    ne = N_EXPERTS
    return pl.pallas_call(
        _merge_kernel,
        grid=(t // tm,),
        in_specs=[
            tok(BRANCH_W), tok(BRANCH_W), tok(BRANCH_W), tok(BRANCH_W),
            pl.BlockSpec((tm, gw), lambda i: (i, C_GATE * LANE // gw)),
            tok(d), bat,
            pl.BlockSpec((N_BRANCH, BRANCH_W, d), lambda i: (0, 0, 0)),
            pl.BlockSpec((d, d), lambda i: (0, 0)),
            pl.BlockSpec((1, d), lambda i: (0, 0)),
            bat, bat,
            pl.BlockSpec((ne, d), lambda i: (0, 0)),
            pl.BlockSpec((ne, 1), lambda i: (0, 0)),
        ],
        out_specs=[
            tok(d), pl.BlockSpec((tm * ROW_TILE, LANE), lambda i: (i, 0)),
            pl.BlockSpec((TOP_K, tm), lambda i: (0, i)),
            pl.BlockSpec((TOP_K, tm), lambda i: (0, i)),
            tok(LANE),
            pl.BlockSpec((ne, LANE), lambda i: (0, 0)),
        ],
        out_shape=[
            jax.ShapeDtypeStruct((t, d), F32),
            jax.ShapeDtypeStruct((t * ROW_TILE, LANE), F32),
            jax.ShapeDtypeStruct((TOP_K, t), jnp.int32),
            jax.ShapeDtypeStruct((TOP_K, t), jnp.int32),
            jax.ShapeDtypeStruct((t, LANE), F32),
            jax.ShapeDtypeStruct((ne, LANE), jnp.int32),
        ],
        scratch_shapes=[pltpu.VMEM((ne, LANE), F32)],
        compiler_params=_cparams(("arbitrary",)),
        name="merge_route",
    )(o_a, o_b, o_c, o_d, p2, x2, g1, w_merge_bf, w_out_bf, norm_w.reshape(1, d), sc2, sh2,
      w_router_t, router_bias.reshape(ne, 1))


MOE_BM = 256


def _plan_kernel(idx_ref, pos_ref, cnt_ref, dest_ref, blke_ref, nused_ref):
    bm = MOE_BM
    ne = N_EXPERTS
    cnt = cnt_ref[...].astype(F32)
    padded = jnp.floor((cnt + (bm - 1)) * (1.0 / bm)) * bm
    r = lax.broadcasted_iota(jnp.int32, (ne, ne), 0)
    c = lax.broadcasted_iota(jnp.int32, (ne, ne), 1)
    pstart = _exact_rows_dot(jnp.where(c < r, 1.0, 0.0), padded)
    pend = pstart + padded

    idx = idx_ref[...]
    base = jnp.zeros(idx.shape, F32)
    for e in range(ne):
        base = jnp.where(idx == e, pstart[e:e + 1, 0:1], base)
    dest_ref[...] = base.astype(jnp.int32) + pos_ref[...]

    nbp = blke_ref.shape[1]
    blk_start = (lax.broadcasted_iota(jnp.int32, (ne, nbp), 1) * bm).astype(F32)
    done = jnp.sum(jnp.where(pend[:, 0:1] <= blk_start, 1.0, 0.0), axis=0, keepdims=True)
    blke_ref[...] = jnp.minimum(done, ne - 1.0).astype(jnp.int32)
    nused_ref[...] = (pend[ne - 1:ne, :] * (1.0 / bm)).astype(jnp.int32)


def moe_plan(idx, pos, counts):
    n_tok = idx.shape[1]
    n_rows = TOP_K * n_tok + N_EXPERTS * MOE_BM
    n_blocks = n_rows // MOE_BM
    nbp = -(-n_blocks // LANE) * LANE
    dest, blk_e, n_used = pl.pallas_call(
        _plan_kernel,
        out_shape=[
            jax.ShapeDtypeStruct((TOP_K, n_tok), jnp.int32),
            jax.ShapeDtypeStruct((1, nbp), jnp.int32),
            jax.ShapeDtypeStruct((1, LANE), jnp.int32),
        ],
        compiler_params=pltpu.CompilerParams(vmem_limit_bytes=VMEM_LIMIT),
        name="moe_plan",
    )(idx, pos, counts)
    return dest.reshape(-1), blk_e[0, :n_blocks], n_used[0, :1]


ROW_TILE = 8
DISPATCH_TM = 512


def _to_row_tiles(ref, x):
    rows = x.shape[0]
    for j in range(ROW_TILE):
        ref[pl.ds(j, rows, stride=ROW_TILE), :] = x[:, j * LANE:(j + 1) * LANE]


def _from_row_tiles(ref, rows):
    return jnp.concatenate([ref[pl.ds(j, rows, stride=ROW_TILE), :] for j in range(ROW_TILE)], axis=1)


def _tile_rows(row):
    return pl.ds(pl.multiple_of(row * ROW_TILE, ROW_TILE), ROW_TILE)


def _dispatch_kernel(dest_ref, h_ref, xs_init_hbm, xs_hbm, sem_ref, *, n_tok):
    del xs_init_hbm
    i = pl.program_id(0)
    tm = h_ref.shape[0] // ROW_TILE

    def copy(r, k):
        row = dest_ref[k * n_tok + i * tm + r]
        return pltpu.make_async_copy(h_ref.at[pl.ds(r * ROW_TILE, ROW_TILE), :], xs_hbm.at[_tile_rows(row), :], sem_ref.at[k])

    for r in range(tm):
        for k in range(TOP_K):
            copy(r, k).start(priority=k)
    for r in range(tm):
        for k in range(TOP_K):
            copy(r, k).wait()


def moe_dispatch(h2_tiles, dest, xs_init):
    n_tok = dest.shape[0] // TOP_K
    tm = min(DISPATCH_TM, n_tok)
    grid_spec = pltpu.PrefetchScalarGridSpec(
        num_scalar_prefetch=1,
        grid=(n_tok // tm,),
        in_specs=[pl.BlockSpec((tm * ROW_TILE, LANE), lambda i, ds: (i, 0)), pl.BlockSpec(memory_space=pl.ANY)],
        out_specs=pl.BlockSpec(memory_space=pl.ANY),
        scratch_shapes=[pltpu.SemaphoreType.DMA((TOP_K,))],
    )
    return pl.pallas_call(
        functools.partial(_dispatch_kernel, n_tok=n_tok),
        grid_spec=grid_spec,
        out_shape=jax.ShapeDtypeStruct(xs_init.shape, F32),
        input_output_aliases={2: 0},
        compiler_params=_cparams(("arbitrary",)),
        name="moe_dispatch",
    )(dest, h2_tiles, xs_init)


def _expert_kernel(blke_ref, nused_ref, x_ref, wg_ref, wu_ref, wd_ref, y_ref, wgb_ref, wub_ref, wdb_ref):
    b = pl.program_id(0)
    bm = MOE_BM
    used = nused_ref[0]
    changed = jnp.logical_or(b == 0, blke_ref[b] != blke_ref[jnp.maximum(b - 1, 0)])

    @pl.when(jnp.logical_and(b < used, changed))
    def _():
        wgb_ref[...] = wg_ref[...].astype(BF16)
        wub_ref[...] = wu_ref[...].astype(BF16)
        wdb_ref[...] = wd_ref[...].astype(BF16)

    @pl.when(b < used)
    def _():
        xb = _from_row_tiles(x_ref, bm).astype(BF16)
        hid = _silu(jnp.dot(xb, wgb_ref[...], preferred_element_type=F32)) * jnp.dot(xb, wub_ref[...], preferred_element_type=F32)
        _to_row_tiles(y_ref, jnp.dot(hid.astype(BF16), wdb_ref[...], preferred_element_type=F32))

    @pl.when(b >= used)
    def _():
        y_ref[...] = jnp.zeros_like(y_ref)


def moe_experts(xs, blk_e, n_used, w_gate, w_up, w_down, layer):
    bm = MOE_BM
    n_rows = xs.shape[0] // ROW_TILE
    d, de = w_gate.shape[-2:]
    assert d == ROW_TILE * LANE
    wspec = lambda r, c: pl.BlockSpec((None, None, r, c), lambda b, be, nu: (layer, be[b], 0, 0))
    rows = pl.BlockSpec((bm * ROW_TILE, LANE), lambda b, be, nu: (b, 0))
    grid_spec = pltpu.PrefetchScalarGridSpec(
        num_scalar_prefetch=2,
        grid=(n_rows // bm,),
        in_specs=[rows, wspec(d, de), wspec(d, de), wspec(de, d)],
        out_specs=rows,
        scratch_shapes=[pltpu.VMEM((d, de), BF16), pltpu.VMEM((d, de), BF16), pltpu.VMEM((de, d), BF16)],
    )
    return pl.pallas_call(
        _expert_kernel,
        grid_spec=grid_spec,
        out_shape=jax.ShapeDtypeStruct(xs.shape, F32),
        compiler_params=_cparams(("arbitrary",)),
        name="moe_experts",
    )(blk_e, n_used, xs, w_gate, w_up, w_down)


COMB_TM = 256


def _tile_copy(src_hbm, row, dst, r, sem):
    return pltpu.make_async_copy(src_hbm.at[_tile_rows(row), :], dst.at[pl.ds(r * ROW_TILE, ROW_TILE), :], sem)


def _gather_start(src_hbm, row_of, dst, sem, n):
    for r in range(n):
        _tile_copy(src_hbm, row_of(r), dst, r, sem).start(priority=r % 2)


def _gather_wait(src_hbm, dst, sem, n):
    for r in range(n):
        _tile_copy(src_hbm, 0, dst, r, sem).wait()


def _combine_kernel(dest_ref, ys_hbm, x_ref, wt_ref, g2_ref, fw_ref, o_ref, buf_ref, sem_ref, *, n_tok, final_norm):
    i = pl.program_id(0)
    n = pl.num_programs(0)
    tm = x_ref.shape[0]
    slot = i % 2

    def start(tile, sl):
        for k in range(TOP_K):
            _gather_start(ys_hbm, lambda r: dest_ref[k * n_tok + tile * tm + r], buf_ref.at[sl, k], sem_ref.at[sl, k], tm)

    def wait(sl):
        for k in range(TOP_K):
            _gather_wait(ys_hbm, buf_ref.at[sl, k], sem_ref.at[sl, k], tm)

    @pl.when(i == 0)
    def _():
        start(0, 0)

    start(jnp.minimum(i + 1, n - 1), 1 - slot)
    wait(slot)
    wt = wt_ref[...]
    moe = (wt[:, 0:1] * _from_row_tiles(buf_ref.at[slot, 0], tm)
           + wt[:, 1:2] * _from_row_tiles(buf_ref.at[slot, 1], tm))
    out = x_ref[...] + g2_ref[...] * moe
    if final_norm:
        out = out * lax.rsqrt(jnp.mean(out * out, axis=-1, keepdims=True) + NORM_EPS) * fw_ref[...]
    o_ref[...] = out

    @pl.when(i == n - 1)
    def _():
        wait(1 - slot)


def moe_combine(ys, dest, x2, wts, g2, final_w, seq, final_norm):
    t, d = x2.shape
    tm = min(COMB_TM, seq)
    per_b = seq // tm
    grid_spec = pltpu.PrefetchScalarGridSpec(
        num_scalar_prefetch=1,
        grid=(t // tm,),
        in_specs=[
            pl.BlockSpec(memory_space=pl.ANY),
            pl.BlockSpec((tm, d), lambda i, ds: (i, 0)),
            pl.BlockSpec((tm, LANE), lambda i, ds: (i, 0)),
            pl.BlockSpec((None, 1, d), lambda i, ds: (i // per_b, 0, 0)),
            pl.BlockSpec((1, d), lambda i, ds: (0, 0)),
        ],
        out_specs=pl.BlockSpec((tm, d), lambda i, ds: (i, 0)),
        scratch_shapes=[pltpu.VMEM((2, TOP_K, tm * ROW_TILE, LANE), F32), pltpu.SemaphoreType.DMA((2, TOP_K))],
    )
    return pl.pallas_call(
        functools.partial(_combine_kernel, n_tok=t, final_norm=final_norm),
        grid_spec=grid_spec,
        out_shape=jax.ShapeDtypeStruct((t, d), F32),
        compiler_params=_cparams(("arbitrary",)),
        name="moe_combine",
    )(dest, ys, x2, wts, g2, final_w.reshape(1, d))


def kernel(x, c, w_ada, b_ada, norm_mix_w, norm_ffn_w, w_in, conv_w, conv_b, ssd_dt_bias, ssd_a_log, ssd_d, ssd_norm_w, diff_lambda, diff_subln_w, hgrn_lb_logits, hgrn_norm_w, w_merge, w_out, w_router, router_bias, w_expert_gate, w_expert_up, w_expert_down, final_norm_w):
    b, s, d = x.shape
    t = b * s
    depth = w_in.shape[0]
    mod = ada_modulation(c, w_ada, b_ada)
    lb_p = jax.nn.softmax(hgrn_lb_logits.astype(F32), axis=0)
    lower_bounds = jnp.cumsum(lb_p, axis=0) - lb_p[0]
    w_router_t = w_router.T
    x2 = x.reshape(t, d)
    ys = jnp.zeros(((TOP_K * t + N_EXPERTS * MOE_BM) * ROW_TILE, LANE), F32)
    for l in range(depth):
        sh1, sc1, g1, sh2, sc2, g2 = [mod[l, :, i * d:(i + 1) * d].reshape(b, 1, d) for i in range(6)]
        p2 = in_projection(x2, norm_mix_w[l], sc1, sh1, pad_w_in(w_in[l]), s)
        p3 = p2.reshape(b, s, -1)
        o_a = ssd_branch(p3, conv_w[l], conv_b[l], ssd_dt_bias[l], ssd_a_log[l], ssd_d[l], ssd_norm_w[l])
        o_b = dilated_branch(p3)
        o_c = diff_branch(p3, diff_lambda[l], diff_subln_w[l], l)
        o_d = hgrn_branch(p3, lower_bounds[l], hgrn_norm_w[l])
        flat = lambda o: o.reshape(t, BRANCH_W)
        x_mid, h2, idx, pos, wts, counts = merge_and_route(
            flat(o_a), flat(o_b), flat(o_c), flat(o_d), p2, x2, g1, w_merge[l].astype(BF16), w_out[l].astype(BF16),
            norm_ffn_w[l], sc2, sh2, w_router_t, router_bias, s)
        dest, blk_e, n_used = moe_plan(idx, pos, counts)
        xs = moe_dispatch(h2, dest, ys)
        ys = moe_experts(xs, blk_e, n_used, w_expert_gate, w_expert_up, w_expert_down, l)
        x2 = moe_combine(ys, dest, x_mid, wts, g2, final_norm_w, s, final_norm=(l == depth - 1))
    return x2.reshape(b, s, d)
```

```python
import functools
import math

import jax
import jax.numpy as jnp
from jax import lax
from jax.experimental import pallas as pl
from jax.experimental.pallas import tpu as pltpu

F32 = jnp.float32
BF16 = jnp.bfloat16

D_MODEL = 1024
DEPTH = 2
N_BRANCH = 4
BRANCH_W = 512

SSD_D_INNER = 512
SSD_HEAD_DIM = 64
SSD_HEADS = 8
SSD_GROUPS = 2
SSD_HPG = 4
SSD_STATE = 64
SSD_CONV = 4
SSD_CHUNK = 128
SSD_STEP_CHUNKS = 2
SSD_HALO = 16
SSD_CONV_CH = 768

DIL_PAIRS = ((128, 1), (512, 4), (2048, 16))
DIL_HPG = 4
DIL_HEAD_DIM = 128
DIL_HEADS = 12
DIL_BLOCK = 128

DIFF_HEADS = 4
DIFF_HEAD_DIM = 64

HG_HEADS = 8
HG_KEY_DIM = 64
HG_VAL_DIM = 64
HG_CHUNK = 64

N_EXPERTS = 64
N_EXPERT_GROUPS = 8
EXPERTS_PER_GROUP = 8
TOP_K = 2
D_EXPERT = 256

NORM_EPS = 1e-6
MASK_VALUE = -1e30

LANE = 128
VMEM_LIMIT = 48 * 1024 * 1024

C_GATE = 0
C_Z = 32
C_XBC = 36
C_DT = 42
C_QB = 44
C_KB = 56
C_VB = 68
C_QC = 80
C_KC = 84
C_VC = 88
C_QD = 92
C_FD = 96
C_ID = 100
C_GD = 104
N_COLB = 108
D_IN_PAD = N_COLB * LANE
DT_PAD = 2 * LANE


def _cparams(sem):
    return pltpu.CompilerParams(dimension_semantics=sem, vmem_limit_bytes=VMEM_LIMIT)


def _aligned(x, m):
    return x if isinstance(x, int) else pl.multiple_of(x, m)


def _sigmoid(v):
    return 0.5 * jnp.tanh(0.5 * v) + 0.5


def _silu(v):
    return v * _sigmoid(v)


def _bdot(a, b):
    return jnp.dot(a.astype(BF16), b.astype(BF16), preferred_element_type=F32)


def _bdot_nt(a, b):
    return lax.dot_general(a.astype(BF16), b.astype(BF16), (((1,), (1,)), ((), ())),
                           preferred_element_type=F32)


def _bdot_tn(a, b):
    return lax.dot_general(a.astype(BF16), b.astype(BF16), (((0,), (0,)), ((), ())),
                           preferred_element_type=F32)


def _dot_rhs01(x, m01):
    hi, lo = _split_bf16(x)
    return jnp.dot(hi, m01, preferred_element_type=F32) + jnp.dot(lo, m01, preferred_element_type=F32)


def _exact_rows_dot(m01, v):
    hi = v.astype(BF16)
    r1 = v - hi.astype(F32)
    mid = r1.astype(BF16)
    lo = (r1 - mid.astype(F32)).astype(BF16)
    m = m01.astype(BF16)
    return (jnp.dot(m, hi, preferred_element_type=F32) + jnp.dot(m, mid, preferred_element_type=F32)
            + jnp.dot(m, lo, preferred_element_type=F32))


def _ada_kernel(c_ref, w_ref, b_ref, o_ref):
    o_ref[...] = _bdot(_silu(c_ref[...]), w_ref[...]) + b_ref[...]


def ada_modulation(c, w_ada, b_ada):
    depth, d, n = w_ada.shape
    b = c.shape[0]
    bp = 8
    c_pad = jnp.zeros((bp, d), F32).at[:b].set(c)
    tn = 1536
    out = pl.pallas_call(
        _ada_kernel,
        grid=(depth, n // tn),
        in_specs=[
            pl.BlockSpec((bp, d), lambda l, j: (0, 0)),
            pl.BlockSpec((None, d, tn), lambda l, j: (l, 0, j)),
            pl.BlockSpec((None, 1, tn), lambda l, j: (l, 0, j)),
        ],
        out_specs=pl.BlockSpec((None, bp, tn), lambda l, j: (l, 0, j)),
        out_shape=jax.ShapeDtypeStruct((depth, bp, n), F32),
        compiler_params=_cparams(("arbitrary", "arbitrary")),
        name="ada_mod",
    )(c_pad, w_ada, b_ada.reshape(depth, 1, n))
    return out[:, :b]


INPROJ_TM = 2048
INPROJ_TN = 512


def _inproj_kernel(x_ref, nw_ref, sc_ref, sh_ref, w_ref, o_ref, h_ref):
    @pl.when(pl.program_id(1) == 0)
    def _():
        x = x_ref[...]
        y = x * lax.rsqrt(jnp.mean(x * x, axis=-1, keepdims=True) + NORM_EPS) * nw_ref[...]
        h_ref[...] = (y * (1.0 + sc_ref[...]) + sh_ref[...]).astype(BF16)

    o_ref[...] = jnp.dot(h_ref[...], w_ref[...], preferred_element_type=F32).astype(o_ref.dtype)


def in_projection(x2, norm_w, scale, shift, w_pad, seq):
    t, d = x2.shape
    n = w_pad.shape[1]
    tm = min(INPROJ_TM, seq)
    tn = INPROJ_TN
    per_b = seq // tm
    return pl.pallas_call(
        _inproj_kernel,
        grid=(t // tm, n // tn),
        in_specs=[
            pl.BlockSpec((tm, d), lambda i, j: (i, 0)),
            pl.BlockSpec((1, d), lambda i, j: (0, 0)),
            pl.BlockSpec((None, 1, d), lambda i, j: (i // per_b, 0, 0)),
            pl.BlockSpec((None, 1, d), lambda i, j: (i // per_b, 0, 0)),
            pl.BlockSpec((d, tn), lambda i, j: (0, j)),
        ],
        out_specs=pl.BlockSpec((tm, tn), lambda i, j: (i, j)),
        out_shape=jax.ShapeDtypeStruct((t, n), BF16),
        scratch_shapes=[pltpu.VMEM((tm, d), BF16)],
        compiler_params=_cparams(("arbitrary", "arbitrary")),
        name="in_proj",
    )(x2, norm_w.reshape(1, d), scale, shift, w_pad)


def pad_w_in(w_in_l):
    d = w_in_l.shape[0]
    o_dt = SSD_D_INNER + SSD_CONV_CH
    o_gate = w_in_l.shape[1] - N_BRANCH * D_MODEL
    return jnp.concatenate(
        [w_in_l[:, o_gate:], w_in_l[:, :o_dt + SSD_HEADS], jnp.zeros((d, DT_PAD - SSD_HEADS), w_in_l.dtype),
         w_in_l[:, o_dt + SSD_HEADS:o_gate]], axis=1).astype(BF16)


def _ssd_kernel(z_ref, xbc_ref, dt_ref, cw_ref, cb_ref, dtb_ref, alog_ref, dsk_ref, nw_ref, exp_ref, o_ref,
                xpad_ref, state_ref, y_ref):
    c = pl.program_id(1)
    L = SSD_CHUNK
    halo = SSD_HALO

    @pl.when(c == 0)
    def _():
        xpad_ref[L:L + halo, :] = jnp.zeros((halo, SSD_CONV_CH), BF16)
        state_ref[...] = jnp.zeros_like(state_ref)

    for ci in range(SSD_STEP_CHUNKS):
        rows = slice(ci * L, (ci + 1) * L)
        _ssd_chunk(z_ref[rows, :], xbc_ref[rows, :], dt_ref[rows, :], cw_ref, cb_ref, dtb_ref, alog_ref, dsk_ref, nw_ref,
                   exp_ref, o_ref, xpad_ref, state_ref, y_ref, rows)


def _ssd_chunk(z_in, xbc_in, dt_in, cw_ref, cb_ref, dtb_ref, alog_ref, dsk_ref, nw_ref, exp_ref, o_ref, xpad_ref, state_ref,
               y_ref, rows):
    L = SSD_CHUNK
    halo = SSD_HALO
    xpad_ref[0:halo, :] = xpad_ref[L:L + halo, :]
    z = z_in.astype(F32)
    xpad_ref[halo:halo + L, :] = xbc_in

    xpad = xpad_ref[...]
    t_i = lax.broadcasted_iota(jnp.int32, (L, halo + L), 0)
    m_i = lax.broadcasted_iota(jnp.int32, (L, halo + L), 1)
    conv = cb_ref[...] + cw_ref[SSD_CONV - 1:SSD_CONV, :] * xbc_in.astype(F32)
    for j in range(SSD_CONV - 1):
        shift = jnp.where(m_i == t_i + (halo - (SSD_CONV - 1) + j), 1.0, 0.0).astype(BF16)
        conv = conv + cw_ref[j:j + 1, :] * jnp.dot(shift, xpad, preferred_element_type=F32)
    xc = _silu(conv)
    xs = xc[:, 0:SSD_D_INNER]
    ns = SSD_GROUPS * SSD_STATE
    bm = xc[:, SSD_D_INNER:SSD_D_INNER + ns]
    cm = xc[:, SSD_D_INNER + ns:SSD_D_INNER + 2 * ns]

    dt = jax.nn.softplus(dt_in[:, 0:LANE].astype(F32) + dtb_ref[...])
    a = -jnp.exp(alog_ref[...])
    adt = dt * a
    row = lax.broadcasted_iota(jnp.int32, (L, L), 0)
    col = lax.broadcasted_iota(jnp.int32, (L, L), 1)
    tril = row >= col
    tril_f = jnp.where(tril, 1.0, 0.0)
    a_cs = _exact_rows_dot(tril_f, adt)
    a_cs_t = a_cs.T

    expand = exp_ref[...]
    dt_x = _dot_rhs01(dt, expand)
    a_cs_x = _exact_rows_dot(tril_f, _dot_rhs01(adt, expand))
    tot_x = a_cs_x[L - 1:L, :]
    xdt = xs * dt_x
    xdt_b = xdt.astype(BF16)
    xdec_b = (xdt * jnp.exp(tot_x - a_cs_x)).astype(BF16)
    dec_out = jnp.exp(a_cs_x)
    dec_tot = jnp.exp(tot_x)

    npair = SSD_HEADS // 2
    grp_of_pair = [(2 * p) // SSD_HPG for p in range(npair)]
    lane = lax.broadcasted_iota(jnp.int32, (L, LANE), 1)
    first = lane < SSD_STATE
    bm_b = bm.astype(BF16)
    cm_b = cm.astype(BF16)
    cbs = [lax.dot_general(jnp.where(first if g == 0 else jnp.logical_not(first), cm, 0.0).astype(BF16), bm_b,
                           (((1,), (1,)), ((), ())), preferred_element_type=F32) for g in range(SSD_GROUPS)]
    lmats = [jnp.exp(jnp.where(tril, jnp.broadcast_to(a_cs[:, e:e + 1], (L, L)) - a_cs_t[e:e + 1, :], MASK_VALUE))
             for e in range(SSD_HEADS)]
    ms = [(cbs[e // SSD_HPG] * lmats[e]).astype(BF16) for e in range(SSD_HEADS)]
    sls = [slice(p * LANE, (p + 1) * LANE) for p in range(npair)]
    diag = [jnp.dot(ms[e], xdt_b[:, sls[e // 2]], preferred_element_type=F32) for e in range(SSD_HEADS)]
    sts = [state_ref[p] for p in range(npair)]
    offs = [jnp.dot(cm_b, sts[p].astype(BF16), preferred_element_type=F32) for p in range(npair)]
    bm_t = bm.T.astype(BF16)
    locs = [jnp.dot(bm_t, xdec_b[:, sls[p]], preferred_element_type=F32) for p in range(npair)]
    sub = lax.broadcasted_iota(jnp.int32, (LANE, LANE), 0)
    for p in range(npair):
        own_rows = (sub < SSD_STATE) if grp_of_pair[p] == 0 else (sub >= SSD_STATE)
        state_ref[p] = dec_tot[:, sls[p]] * sts[p] + jnp.where(own_rows, locs[p], 0.0)
    for p in range(npair):
        y = jnp.where(first, diag[2 * p], diag[2 * p + 1]) + offs[p] * dec_out[:, sls[p]] + dsk_ref[:, sls[p]] * xs[:, sls[p]]
        y_ref[:, sls[p]] = y

    yz = y_ref[...] * _silu(z)
    o_ref[rows, :] = yz * lax.rsqrt(jnp.mean(yz * yz, axis=-1, keepdims=True) + NORM_EPS) * nw_ref[...]


def _pad_lanes(v, n=LANE):
    return jnp.zeros((1, n), F32).at[0, :v.shape[0]].set(v.astype(F32))


def ssd_branch(p3, conv_w, conv_b, dt_bias, a_log, d_skip, norm_w):
    b, s, _ = p3.shape
    L = SSD_CHUNK * SSD_STEP_CHUNKS
    const = lambda shape: pl.BlockSpec(shape, lambda i, c: (0, 0))
    assert SSD_HEAD_DIM == SSD_STATE and LANE == 2 * SSD_HEAD_DIM and SSD_HPG % 2 == 0
    import numpy as np
    expand = (np.arange(LANE)[:, None] == np.arange(SSD_D_INNER)[None, :] // SSD_HEAD_DIM).astype(np.float32)
    return pl.pallas_call(
        _ssd_kernel,
        grid=(b, s // L),
        in_specs=[
            pl.BlockSpec((None, L, SSD_D_INNER), lambda i, c: (i, c, C_Z * LANE // SSD_D_INNER)),
            pl.BlockSpec((None, L, SSD_CONV_CH), lambda i, c: (i, c, C_XBC * LANE // SSD_CONV_CH)),
            pl.BlockSpec((None, L, DT_PAD), lambda i, c: (i, c, C_DT * LANE // DT_PAD)),
            const((SSD_CONV, SSD_CONV_CH)),
            const((1, SSD_CONV_CH)),
            const((1, LANE)),
            const((1, LANE)),
            const((1, SSD_D_INNER)),
            const((1, SSD_D_INNER)),
            const((LANE, SSD_D_INNER)),
        ],
        out_specs=pl.BlockSpec((None, L, SSD_D_INNER), lambda i, c: (i, c, 0)),
        out_shape=jax.ShapeDtypeStruct((b, s, SSD_D_INNER), F32),
        scratch_shapes=[
            pltpu.VMEM((SSD_CHUNK + SSD_HALO, SSD_CONV_CH), BF16),
            pltpu.VMEM((SSD_HEADS // 2, LANE, LANE), F32),
            pltpu.VMEM((SSD_CHUNK, SSD_D_INNER), F32),
        ],
        compiler_params=_cparams(("arbitrary", "arbitrary")),
        name="ssd",
    )(p3, p3, p3, conv_w, conv_b.reshape(1, -1), _pad_lanes(dt_bias), _pad_lanes(a_log),
      jnp.repeat(d_skip.astype(F32), SSD_HEAD_DIM).reshape(1, -1), norm_w.reshape(1, -1), jnp.asarray(expand, BF16))


DIL_UNROLL = 4


def _dil_kernel(q_ref, k_ref, v_ref, o_ref, m_ref, l_ref, tmp_ref, qd_ref, kd_ref, vd_ref):
    g = pl.program_id(2)
    s_len = q_ref.shape[0]
    blk = DIL_BLOCK
    scale = DIL_HEAD_DIM ** -0.5

    @pl.when(g == 0)
    def _():
        m_ref[...] = jnp.full_like(m_ref, MASK_VALUE)
        l_ref[...] = jnp.zeros_like(l_ref)
        o_ref[...] = jnp.zeros_like(o_ref)

    def run_group(dil):
        sub = s_len // dil
        nblk = sub // blk

        if dil == 1:
            qd_ref[...] = (q_ref[...].astype(F32) * scale).astype(BF16)
            kd, vd = k_ref, v_ref
        else:
            for src, dst, mul in ((q_ref, qd_ref, scale), (k_ref, kd_ref, None), (v_ref, vd_ref, None)):
                x = src[...].astype(F32)
                tmp_ref[...] = x if mul is None else x * mul

                def gather(r, carry, dst=dst):
                    dst[pl.ds(pl.multiple_of(r * sub, blk), sub), :] = tmp_ref[pl.ds(r, sub, stride=dil), :].astype(BF16)
                    return carry

                lax.fori_loop(0, dil, gather, 0)
            kd, vd = kd_ref, vd_ref

        def block_softmax(r, b, nkey):
            row0 = _aligned(r * sub + b * blk, blk)
            ks = pl.ds(_aligned(row0 - (nkey - blk), blk), nkey)
            i = lax.broadcasted_iota(jnp.int32, (blk, nkey), 0) + (nkey - blk)
            j = lax.broadcasted_iota(jnp.int32, (blk, nkey), 1)
            ok = jnp.logical_and(j <= i, j >= i - blk)
            sc = lax.dot_general(qd_ref[pl.ds(row0, blk), :], kd[ks, :], (((1,), (1,)), ((), ())),
                                 preferred_element_type=F32)
            sc = jnp.where(ok, sc, MASK_VALUE)
            m_blk = jnp.max(sc, axis=-1, keepdims=True)
            p = jnp.exp(sc - m_blk).astype(BF16)
            v_ext = jnp.concatenate([vd[ks, :], jnp.ones((nkey, LANE), BF16)], axis=1)
            return m_blk, jnp.dot(p, v_ext, preferred_element_type=F32)

        def merge(r, b, m_blk, pv):
            base = r + dil * blk * b
            acc_rows = pl.ds(_aligned(base, blk), blk) if dil == 1 else pl.ds(base, blk, stride=dil)
            m_old = m_ref[acc_rows, :]
            m_blk = jnp.broadcast_to(m_blk, (blk, LANE))
            m_new = jnp.maximum(m_old, m_blk)
            a_old = jnp.exp(m_old - m_new)
            a_blk = jnp.exp(m_blk - m_new)
            o_ref[acc_rows, :] = a_old * o_ref[acc_rows, :] + a_blk * pv[:, :LANE]
            l_ref[acc_rows, :] = a_old * l_ref[acc_rows, :] + a_blk * pv[:, LANE:]
            m_ref[acc_rows, :] = m_new

        def run_units(n, where, nkey):
            def group(units):
                parts = [block_softmax(r, b, nkey) for r, b in units]
                for (r, b), (m_blk, pv) in zip(units, parts):
                    merge(r, b, m_blk, pv)

            def body(i, carry):
                group([where(i * DIL_UNROLL + j) for j in range(DIL_UNROLL)])
                return carry

            lax.fori_loop(0, n // DIL_UNROLL, body, 0)
            if n % DIL_UNROLL:
                group([where(n - n % DIL_UNROLL + j) for j in range(n % DIL_UNROLL)])

        run_units(dil, lambda u: (u, 0), blk)
        if nblk > 1:
            run_units(dil * (nblk - 1), lambda u: (u // (nblk - 1), 1 + u % (nblk - 1)), 2 * blk)

    for gi, (window, dil) in enumerate(DIL_PAIRS):
        assert window // dil == blk

        @pl.when(g == gi)
        def _(dil=dil):
            run_group(dil)

    @pl.when(g == len(DIL_PAIRS) - 1)
    def _():
        o_ref[...] = o_ref[...] / l_ref[...]


def dilated_branch(p3):
    b, s, _ = p3.shape
    ng = len(DIL_PAIRS)
    spec = lambda c0: pl.BlockSpec((None, s, DIL_HEAD_DIM), lambda i, j, g: (i, 0, c0 + g * DIL_HPG + j))
    return pl.pallas_call(
        _dil_kernel,
        grid=(b, DIL_HPG, ng),
        in_specs=[spec(C_QB), spec(C_KB), spec(C_VB)],
        out_specs=pl.BlockSpec((None, s, DIL_HEAD_DIM), lambda i, j, g: (i, 0, j)),
        out_shape=jax.ShapeDtypeStruct((b, s, BRANCH_W), F32),
        scratch_shapes=[pltpu.VMEM((s, LANE), F32), pltpu.VMEM((s, LANE), F32), pltpu.VMEM((s, LANE), F32),
                        pltpu.VMEM((s, LANE), BF16), pltpu.VMEM((s, LANE), BF16), pltpu.VMEM((s, LANE), BF16)],
        compiler_params=_cparams(("arbitrary", "arbitrary", "arbitrary")),
        name="dilated",
    )(p3, p3, p3)


DIFF_TQ = 512
DIFF_TK = 1024
DIFF_DIAG = 512
LOG2E = 1.4426950408889634


DIFF_ONES = 16


def _diff_kernel(lam_ref, q_ref, k_ref, v_ref, nw_ref, o_ref, vt_ref, m_ref, acc_ref, *, lam_init):
    qi = pl.program_id(2)
    tq, tk, dh = DIFF_TQ, DIFF_TK, DIFF_HEAD_DIM
    w = 2 * dh
    s_len = k_ref.shape[0]

    @pl.when(qi == 0)
    def _():
        for c0 in range(0, s_len, tk):
            vt_ref[0:w, c0:c0 + tk] = v_ref[c0:c0 + tk, :].astype(F32).T.astype(BF16)
        vt_ref[w:w + DIFF_ONES, :] = jnp.ones((DIFF_ONES, s_len), BF16)

    q_t = (q_ref[...].astype(F32) * (dh ** -0.5 * LOG2E)).T
    half = lax.broadcasted_iota(jnp.int32, (w, tq), 0) < dh
    q_sel = [jnp.where(half, q_t, 0.0).astype(BF16), jnp.where(half, 0.0, q_t).astype(BF16)]
    for t in range(2):
        m_ref[t] = jnp.full((8, tq), MASK_VALUE, F32)
        acc_ref[t] = jnp.zeros((w + DIFF_ONES, tq), F32)

    def block(kstart, width, qlo=None):
        c0 = qlo or 0
        nq = tq - c0
        kb = k_ref[pl.ds(kstart, width), :]
        vb = vt_ref[:, pl.ds(kstart, width)]
        scores = [jnp.dot(kb, q_sel[t][:, c0:], preferred_element_type=F32) for t in range(2)]
        if qlo is not None:
            row = lax.broadcasted_iota(jnp.int32, (width, nq), 0)
            col = lax.broadcasted_iota(jnp.int32, (width, nq), 1)
            scores = [jnp.where(row <= col, sc, MASK_VALUE) for sc in scores]
        m_blks = [jnp.max(jnp.max(sc.reshape(width // 8, 8, nq), axis=0), axis=0, keepdims=True) for sc in scores]
        pvs = [jnp.dot(vb, jnp.exp2(sc - m_blk).astype(BF16), preferred_element_type=F32)
               for sc, m_blk in zip(scores, m_blks)]
        for t, (m_blk, pv) in enumerate(zip(m_blks, pvs)):
            m_old = m_ref[t, 0:1, c0:]
            m_new = jnp.maximum(m_old, m_blk)
            acc_ref[t, :, c0:] = acc_ref[t, :, c0:] * jnp.exp2(m_old - m_new) + pv * jnp.exp2(m_blk - m_new)
            m_ref[t, :, c0:] = jnp.broadcast_to(m_new, (8, nq))

    def main_step(j, carry):
        block(pl.multiple_of(j * tk, tk), tk)
        return carry

    per = tk // tq
    lax.fori_loop(0, qi // per, main_step, 0)
    for r in range(1, per):
        @pl.when(qi % per >= r)
        def _(r=r):
            block(pl.multiple_of((qi // per) * tk + (r - 1) * tq, tq), tq)
    for c0 in range(0, tq, DIFF_DIAG):
        block(pl.multiple_of(qi * tq + c0, DIFF_DIAG), DIFF_DIAG, qlo=c0)

    lam_p = lam_ref[...]
    lam = (jnp.exp(jnp.sum(lam_p[0:1] * lam_p[1:2], axis=-1, keepdims=True))
           - jnp.exp(jnp.sum(lam_p[2:3] * lam_p[3:4], axis=-1, keepdims=True)) + lam_init)
    a0, a1 = acc_ref[0], acc_ref[1]
    o_t = a0[0:w] / a0[w:w + 1] - lam * (a1[0:w] / a1[w:w + 1])
    o = o_t.T
    o = o * lax.rsqrt(jnp.mean(o * o, axis=-1, keepdims=True) + NORM_EPS) * nw_ref[...]
    o_ref[...] = o * (1.0 - lam_init)


def diff_branch(p3, diff_lambda, subln_w, layer):
    b, s, _ = p3.shape
    lam_init = 0.8 - 0.6 * math.exp(-0.3 * layer)
    w = 2 * DIFF_HEAD_DIM
    return pl.pallas_call(
        functools.partial(_diff_kernel, lam_init=lam_init),
        grid=(b, DIFF_HEADS, s // DIFF_TQ),
        in_specs=[
            pl.BlockSpec((4, DIFF_HEAD_DIM), lambda i, h, t: (0, 0)),
            pl.BlockSpec((None, DIFF_TQ, w), lambda i, h, t: (i, t, C_QC + h)),
            pl.BlockSpec((None, s, w), lambda i, h, t: (i, 0, C_KC + h)),
            pl.BlockSpec((None, s, w), lambda i, h, t: (i, 0, C_VC + h)),
            pl.BlockSpec((1, w), lambda i, h, t: (0, 0)),
        ],
        out_specs=pl.BlockSpec((None, DIFF_TQ, w), lambda i, h, t: (i, t, h)),
        out_shape=jax.ShapeDtypeStruct((b, s, BRANCH_W), F32),
        scratch_shapes=[pltpu.VMEM((w + DIFF_ONES, s), BF16), pltpu.VMEM((2, 8, DIFF_TQ), F32),
                        pltpu.VMEM((2, w + DIFF_ONES, DIFF_TQ), F32)],
        compiler_params=_cparams(("arbitrary", "arbitrary", "arbitrary")),
        name="diff_attn",
    )(diff_lambda, p3, p3, p3, subln_w.reshape(1, w))


HG_LEVELS = 6
HG_STEP_CHUNKS = 2


def _hgrn_tables():
    import numpy as np
    c = HG_CHUNK
    i = np.arange(c)[:, None]
    j = np.arange(c)[None, :]
    tril = (j <= i).astype(np.float32)
    masks = []
    for lv in range(HG_LEVELS):
        sz = 1 << lv
        m = ((i // (2 * sz) == j // (2 * sz)) & (i // sz == j // sz + 1)).astype(np.float32)
        masks.append(np.concatenate([m, m], axis=0))
    lane = np.arange(LANE)
    same_head = (lane[:, None] // HG_KEY_DIM == lane[None, :] // HG_KEY_DIM).astype(np.float32)
    return tril, np.stack(masks, axis=0), same_head


def _hgrn_kernel(q_ref, f_ref, i_ref, g_ref, lb_ref, tri_ref, msk_ref, sh_ref, nw_ref, o_ref, state_ref, gpad_ref):
    @pl.when(pl.program_id(1) == 0)
    def _():
        state_ref[...] = jnp.zeros_like(state_ref)

    for ci in range(HG_STEP_CHUNKS):
        rows = slice(ci * HG_CHUNK, (ci + 1) * HG_CHUNK)
        _hgrn_chunk(q_ref[rows, :], f_ref[rows, :], i_ref[rows, :], g_ref, lb_ref, tri_ref, msk_ref, sh_ref, nw_ref,
                    o_ref, state_ref, gpad_ref, rows)


def _hgrn_chunk(q_in, f_in, i_in, g_ref, lb_ref, tri_ref, msk_ref, sh_ref, nw_ref, o_ref, state_ref, gpad_ref, rows):
    cs = HG_CHUNK
    nl = HG_LEVELS
    w = q_in.shape[1]
    sub_rows = 8

    lb = lb_ref[...]
    f_gate = lb + (1.0 - lb) * jax.nn.sigmoid(f_in.astype(F32))
    log_f = jnp.log(f_gate)
    k_in = 1.0 - f_gate
    q = _silu(q_in.astype(F32))
    v = i_in.astype(F32)

    hi, lo = _split_bf16(log_f)
    tri = tri_ref[...]
    g = jnp.dot(tri, hi, preferred_element_type=F32) + jnp.dot(tri, lo, preferred_element_type=F32)
    gpad_ref[0:sub_rows, :] = jnp.zeros((sub_rows, w), F32)
    gpad_ref[sub_rows:sub_rows + cs, :] = g
    g_last = g[cs - 1:cs, :]

    def g_row(r):
        return jnp.broadcast_to(gpad_ref[sub_rows + r:sub_rows + r + 1, :], (sub_rows, w))

    sub = lax.broadcasted_iota(jnp.int32, (sub_rows, w), 0)
    tiles = [[] for _ in range(nl)]
    for t in range(cs // sub_rows):
        r0 = t * sub_rows
        gt = g[r0:r0 + sub_rows]
        prev = gpad_ref[r0 + sub_rows - 1:r0 + 2 * sub_rows - 1, :]
        tiles[0].append(jnp.where(sub % 2 == 1, gt - prev, 0.0))
        c1 = jnp.where(sub < 4, g_row(r0 + 1), g_row(r0 + 5))
        tiles[1].append(jnp.where((sub // 2) % 2 == 1, gt - c1, c1 - gt))
        c2 = g_row(r0 + 3)
        tiles[2].append(jnp.where(sub >= 4, gt - c2, c2 - gt))
        for lv in range(3, nl):
            span = (1 << lv) // sub_rows
            mid = (t // (2 * span)) * 2 * span + span
            cm = g_row(mid * sub_rows - 1)
            tiles[lv].append(gt - cm if (t // span) % 2 == 1 else cm - gt)
    dec = [jnp.exp(jnp.concatenate(tl, axis=0)) for tl in tiles]
    first_w = lax.broadcasted_iota(jnp.int32, (cs, w), 1) % LANE < HG_KEY_DIM
    q_even = jnp.where(first_w, q, 0.0).astype(BF16)
    q_odd = jnp.where(first_w, 0.0, q).astype(BF16)
    k_b = k_in.astype(BF16)
    dec_b = [d.astype(BF16) for d in dec]
    q_full = q * jnp.exp(g)
    k_full = k_in * jnp.exp(g_last - g)
    decay_row = jnp.exp(g_last)

    same_head = sh_ref[...]
    same_head_b = same_head.astype(BF16)
    first = lax.broadcasted_iota(jnp.int32, (cs, LANE), 1) < HG_KEY_DIM
    pairs = range(w // LANE)
    sls = [slice(p * LANE, (p + 1) * LANE) for p in pairs]
    attns = [jnp.zeros((2 * cs, cs), F32) for _ in pairs]
    for lv in range(nl):
        for p in pairs:
            db = dec_b[lv][:, sls[p]]
            lhs = jnp.concatenate([q_even[:, sls[p]] * db, q_odd[:, sls[p]] * db], axis=0)
            scores = lax.dot_general(lhs, k_b[:, sls[p]] * db, (((1,), (1,)), ((), ())), preferred_element_type=F32)
            attns[p] = attns[p] + msk_ref[lv] * scores
    vps = [v[:, sl] for sl in sls]
    vpbs = [vp.astype(BF16) for vp in vps]
    rs = [jnp.dot(attns[p].astype(BF16), vpbs[p], preferred_element_type=F32) for p in pairs]
    diags = [jnp.dot((q[:, sl] * k_in[:, sl]).astype(BF16), same_head_b, preferred_element_type=F32) for sl in sls]
    sts = [state_ref[p] for p in pairs]
    inters = [_bdot_nt(q_full[:, sls[p]], sts[p]) for p in pairs]
    upds = [_bdot_tn(vpbs[p], k_full[:, sls[p]]) for p in pairs]
    for p in pairs:
        state_ref[p] = decay_row[:, sls[p]] * sts[p] + same_head * upds[p]
    outs = [jnp.where(first, rs[p][0:cs], rs[p][cs:2 * cs]) + diags[p] * vps[p] + inters[p] for p in pairs]
    mss = [jnp.dot((o * o).astype(BF16), same_head_b, preferred_element_type=F32) * (1.0 / HG_VAL_DIM) for o in outs]
    for p in pairs:
        o = outs[p] * lax.rsqrt(mss[p] + NORM_EPS) * nw_ref[...]
        o_ref[rows, sls[p]] = o * _silu(g_ref[rows, sls[p]].astype(F32))


def hgrn_branch(p3, lower_bound, norm_w):
    b, s, _ = p3.shape
    cs = HG_CHUNK * HG_STEP_CHUNKS
    w = HG_HEADS * HG_KEY_DIM
    assert HG_KEY_DIM == HG_VAL_DIM and LANE == 2 * HG_KEY_DIM
    tril, masks, same_head = _hgrn_tables()
    seg = lambda c0: pl.BlockSpec((None, cs, w), lambda i, c: (i, c, c0 * LANE // w))
    return pl.pallas_call(
        _hgrn_kernel,
        grid=(b, s // cs),
        in_specs=[
            seg(C_QD), seg(C_FD), seg(C_ID), seg(C_GD),
            pl.BlockSpec((1, w), lambda i, c: (0, 0)),
            pl.BlockSpec(tril.shape, lambda i, c: (0, 0)),
            pl.BlockSpec(masks.shape, lambda i, c: (0, 0, 0)),
            pl.BlockSpec(same_head.shape, lambda i, c: (0, 0)),
            pl.BlockSpec((1, LANE), lambda i, c: (0, 0)),
        ],
        out_specs=pl.BlockSpec((None, cs, w), lambda i, c: (i, c, 0)),
        out_shape=jax.ShapeDtypeStruct((b, s, BRANCH_W), F32),
        scratch_shapes=[pltpu.VMEM((w // LANE, LANE, LANE), F32), pltpu.VMEM((HG_CHUNK + 8, w), F32)],
        compiler_params=_cparams(("arbitrary", "arbitrary")),
        name="hgrn2",
    )(p3, p3, p3, p3, lower_bound.reshape(1, w).astype(F32), jnp.asarray(tril, BF16), jnp.asarray(masks, F32),
      jnp.asarray(same_head, F32), jnp.tile(norm_w.astype(F32), LANE // HG_VAL_DIM).reshape(1, LANE))


MERGE_TM = 512


def _split_bf16(v):
    hi = v.astype(BF16)
    return hi, (v - hi.astype(F32)).astype(BF16)


def _first_index_of_max(vals, iota, n):
    top = jnp.max(vals, axis=0, keepdims=True)
    idx = jnp.min(jnp.where(vals == top, iota, n), axis=0, keepdims=True)
    return top, idx


def _merge_kernel(oa_ref, ob_ref, oc_ref, od_ref, gl_ref, x_ref, g1_ref, wm_ref, wo_ref,
                  nw_ref, sc_ref, sh_ref, wr_ref, rb_ref,
                  xo_ref, h2_ref, idx_ref, pos_ref, wt_ref, cnt_ref, carry_ref):
    step = pl.program_id(0)
    tm = x_ref.shape[0]
    d = D_MODEL

    acc = jnp.zeros((tm, d), F32)
    for n, o_ref in enumerate((oa_ref, ob_ref, oc_ref, od_ref)):
        proj = jnp.dot(o_ref[...].astype(BF16), wm_ref[n], preferred_element_type=F32)
        acc = acc + _sigmoid(gl_ref[:, n * d:(n + 1) * d].astype(F32)) * proj
    mix = jnp.dot(acc.astype(BF16), wo_ref[...], preferred_element_type=F32)
    x_new = x_ref[...] + g1_ref[...] * mix
    xo_ref[...] = x_new

    y = x_new * lax.rsqrt(jnp.mean(x_new * x_new, axis=-1, keepdims=True) + NORM_EPS) * nw_ref[...]
    h2 = y * (1.0 + sc_ref[...]) + sh_ref[...]
    _to_row_tiles(h2_ref, h2)

    h_hi, h_lo = _split_bf16(h2)
    w_hi, w_lo = _split_bf16(wr_ref[...])
    nt = lambda a, b: lax.dot_general(a, b, (((1,), (1,)), ((), ())), preferred_element_type=F32)
    logits = nt(w_hi, h_hi) + nt(w_hi, h_lo) + nt(w_lo, h_hi)
    scores = jax.nn.sigmoid(logits)
    sel = scores + rb_ref[...]

    ne, ng, pg = N_EXPERTS, N_EXPERT_GROUPS, EXPERTS_PER_GROUP
    iota_g = lax.broadcasted_iota(jnp.int32, (pg, tm), 0)
    best_score = None
    best_group = None
    for g in range(ng):
        xg = sel[g * pg:(g + 1) * pg, :]
        top1, i1 = _first_index_of_max(xg, iota_g, pg)
        top2 = jnp.max(jnp.where(iota_g == i1, -jnp.inf, xg), axis=0, keepdims=True)
        gs = top1 + top2
        if g == 0:
            best_score, best_group = gs, jnp.zeros((1, tm), jnp.int32)
        else:
            better = gs > best_score
            best_score = jnp.where(better, gs, best_score)
            best_group = jnp.where(better, g, best_group)

    iota_e = lax.broadcasted_iota(jnp.int32, (ne, tm), 0)
    masked = jnp.where(iota_e // pg == best_group, sel, MASK_VALUE)
    _, e1 = _first_index_of_max(masked, iota_e, ne)
    oh1 = iota_e == e1
    _, e2 = _first_index_of_max(jnp.where(oh1, MASK_VALUE, masked), iota_e, ne)
    oh2 = iota_e == e2
    w1 = jnp.sum(jnp.where(oh1, scores, 0.0), axis=0, keepdims=True)
    w2 = jnp.sum(jnp.where(oh2, scores, 0.0), axis=0, keepdims=True)
    wsum = w1 + w2
    w1 = w1 / wsum
    w2 = w2 / wsum

    @pl.when(step == 0)
    def _():
        carry_ref[...] = jnp.zeros_like(carry_ref)

    f1 = jnp.where(oh1, 1.0, 0.0)
    f2 = jnp.where(oh2, 1.0, 0.0)
    both = f1 + f2
    r_i = lax.broadcasted_iota(jnp.int32, (tm, tm), 0)
    c_i = lax.broadcasted_iota(jnp.int32, (tm, tm), 1)
    before = jnp.where(r_i < c_i, 1.0, 0.0).astype(BF16)
    rank = jnp.dot(both.astype(BF16), before, preferred_element_type=F32) + carry_ref[:, 0:1]
    p1 = jnp.sum(f1 * rank, axis=0, keepdims=True)
    p2 = jnp.sum(f2 * rank, axis=0, keepdims=True)
    carry_new = carry_ref[...] + jnp.sum(both, axis=1, keepdims=True)
    carry_ref[...] = carry_new
    cnt_ref[...] = carry_new.astype(jnp.int32)

    idx_ref[...] = jnp.concatenate([e1, e2], axis=0)
    pos_ref[...] = jnp.concatenate([p1, p2], axis=0).astype(jnp.int32)
    wt_rows = jnp.concatenate([w1, w2, jnp.zeros((LANE - 2, tm), F32)], axis=0)
    wt_ref[...] = wt_rows.T


def merge_and_route(o_a, o_b, o_c, o_d, p2, x2, g1, w_merge_bf, w_out_bf, norm_w, sc2, sh2, w_router_t, router_bias, seq):
    t, d = x2.shape
    tm = min(MERGE_TM, seq)
    per_b = seq // tm
    gw = N_BRANCH * d
    tok = lambda w: pl.BlockSpec((tm, w), lambda i: (i, 0))
    bat = pl.BlockSpec((None, 1, d), lambda i: (i // per_b, 0, 0))
    ne = N_EXPERTS
    return pl.pallas_call(
        _merge_kernel,
        grid=(t // tm,),
        in_specs=[
            tok(BRANCH_W), tok(BRANCH_W), tok(BRANCH_W), tok(BRANCH_W),
            pl.BlockSpec((tm, gw), lambda i: (i, C_GATE * LANE // gw)),
            tok(d), bat,
            pl.BlockSpec((N_BRANCH, BRANCH_W, d), lambda i: (0, 0, 0)),
            pl.BlockSpec((d, d), lambda i: (0, 0)),
            pl.BlockSpec((1, d), lambda i: (0, 0)),
            bat, bat,
            pl.BlockSpec((ne, d), lambda i: (0, 0)),
            pl.BlockSpec((ne, 1), lambda i: (0, 0)),
        ],
        out_specs=[
            tok(d), pl.BlockSpec((tm * ROW_TILE, LANE), lambda i: (i, 0)),
            pl.BlockSpec((TOP_K, tm), lambda i: (0, i)),
            pl.BlockSpec((TOP_K, tm), lambda i: (0, i)),
            tok(LANE),
            pl.BlockSpec((ne, LANE), lambda i: (0, 0)),
        ],
        out_shape=[
            jax.ShapeDtypeStruct((t, d), F32),
            jax.ShapeDtypeStruct((t * ROW_TILE, LANE), F32),
            jax.ShapeDtypeStruct((TOP_K, t), jnp.int32),
            jax.ShapeDtypeStruct((TOP_K, t), jnp.int32),
            jax.ShapeDtypeStruct((t, LANE), F32),
            jax.ShapeDtypeStruct((ne, LANE), jnp.int32),
        ],
        scratch_shapes=[pltpu.VMEM((ne, LANE), F32)],
        compiler_params=_cparams(("arbitrary",)),
        name="merge_route",
    )(o_a, o_b, o_c, o_d, p2, x2, g1, w_merge_bf, w_out_bf, norm_w.reshape(1, d), sc2, sh2,
      w_router_t, router_bias.reshape(ne, 1))


MOE_BM = 256


def _plan_kernel(idx_ref, pos_ref, cnt_ref, dest_ref, blke_ref, nused_ref):
    bm = MOE_BM
    ne = N_EXPERTS
    cnt = cnt_ref[...].astype(F32)
    padded = jnp.floor((cnt + (bm - 1)) * (1.0 / bm)) * bm
    r = lax.broadcasted_iota(jnp.int32, (ne, ne), 0)
    c = lax.broadcasted_iota(jnp.int32, (ne, ne), 1)
    pstart = _exact_rows_dot(jnp.where(c < r, 1.0, 0.0), padded)
    pend = pstart + padded

    idx = idx_ref[...]
    base = jnp.zeros(idx.shape, F32)
    for e in range(ne):
        base = jnp.where(idx == e, pstart[e:e + 1, 0:1], base)
    dest_ref[...] = base.astype(jnp.int32) + pos_ref[...]

    nbp = blke_ref.shape[1]
    blk_start = (lax.broadcasted_iota(jnp.int32, (ne, nbp), 1) * bm).astype(F32)
    done = jnp.sum(jnp.where(pend[:, 0:1] <= blk_start, 1.0, 0.0), axis=0, keepdims=True)
    blke_ref[...] = jnp.minimum(done, ne - 1.0).astype(jnp.int32)
    nused_ref[...] = (pend[ne - 1:ne, :] * (1.0 / bm)).astype(jnp.int32)


def moe_plan(idx, pos, counts):
    n_tok = idx.shape[1]
    n_rows = TOP_K * n_tok + N_EXPERTS * MOE_BM
    n_blocks = n_rows // MOE_BM
    nbp = -(-n_blocks // LANE) * LANE
    dest, blk_e, n_used = pl.pallas_call(
        _plan_kernel,
        out_shape=[
            jax.ShapeDtypeStruct((TOP_K, n_tok), jnp.int32),
            jax.ShapeDtypeStruct((1, nbp), jnp.int32),
            jax.ShapeDtypeStruct((1, LANE), jnp.int32),
        ],
        compiler_params=pltpu.CompilerParams(vmem_limit_bytes=VMEM_LIMIT),
        name="moe_plan",
    )(idx, pos, counts)
    return dest.reshape(-1), blk_e[0, :n_blocks], n_used[0, :1]


ROW_TILE = 8
DISPATCH_TM = 512


def _to_row_tiles(ref, x):
    rows = x.shape[0]
    for j in range(ROW_TILE):
        ref[pl.ds(j, rows, stride=ROW_TILE), :] = x[:, j * LANE:(j + 1) * LANE]


def _from_row_tiles(ref, rows):
    return jnp.concatenate([ref[pl.ds(j, rows, stride=ROW_TILE), :] for j in range(ROW_TILE)], axis=1)


def _tile_rows(row):
    return pl.ds(pl.multiple_of(row * ROW_TILE, ROW_TILE), ROW_TILE)


def _dispatch_kernel(dest_ref, h_ref, xs_init_hbm, xs_hbm, sem_ref, *, n_tok):
    del xs_init_hbm
    i = pl.program_id(0)
    tm = h_ref.shape[0] // ROW_TILE

    def copy(r, k):
        row = dest_ref[k * n_tok + i * tm + r]
        return pltpu.make_async_copy(h_ref.at[pl.ds(r * ROW_TILE, ROW_TILE), :], xs_hbm.at[_tile_rows(row), :], sem_ref.at[k])

    for r in range(tm):
        for k in range(TOP_K):
            copy(r, k).start(priority=k)
    for r in range(tm):
        for k in range(TOP_K):
            copy(r, k).wait()


def moe_dispatch(h2_tiles, dest, xs_init):
    n_tok = dest.shape[0] // TOP_K
    tm = min(DISPATCH_TM, n_tok)
    grid_spec = pltpu.PrefetchScalarGridSpec(
        num_scalar_prefetch=1,
        grid=(n_tok // tm,),
        in_specs=[pl.BlockSpec((tm * ROW_TILE, LANE), lambda i, ds: (i, 0)), pl.BlockSpec(memory_space=pl.ANY)],
        out_specs=pl.BlockSpec(memory_space=pl.ANY),
        scratch_shapes=[pltpu.SemaphoreType.DMA((TOP_K,))],
    )
    return pl.pallas_call(
        functools.partial(_dispatch_kernel, n_tok=n_tok),
        grid_spec=grid_spec,
        out_shape=jax.ShapeDtypeStruct(xs_init.shape, F32),
        input_output_aliases={2: 0},
        compiler_params=_cparams(("arbitrary",)),
        name="moe_dispatch",
    )(dest, h2_tiles, xs_init)


def _expert_kernel(blke_ref, nused_ref, x_ref, wg_ref, wu_ref, wd_ref, y_ref, wgb_ref, wub_ref, wdb_ref):
    b = pl.program_id(0)
    bm = MOE_BM
    used = nused_ref[0]
    changed = jnp.logical_or(b == 0, blke_ref[b] != blke_ref[jnp.maximum(b - 1, 0)])

    @pl.when(jnp.logical_and(b < used, changed))
    def _():
        wgb_ref[...] = wg_ref[...].astype(BF16)
        wub_ref[...] = wu_ref[...].astype(BF16)
        wdb_ref[...] = wd_ref[...].astype(BF16)

    @pl.when(b < used)
    def _():
        xb = _from_row_tiles(x_ref, bm).astype(BF16)
        hid = _silu(jnp.dot(xb, wgb_ref[...], preferred_element_type=F32)) * jnp.dot(xb, wub_ref[...], preferred_element_type=F32)
        _to_row_tiles(y_ref, jnp.dot(hid.astype(BF16), wdb_ref[...], preferred_element_type=F32))

    @pl.when(b >= used)
    def _():
        y_ref[...] = jnp.zeros_like(y_ref)


def moe_experts(xs, blk_e, n_used, w_gate, w_up, w_down, layer):
    bm = MOE_BM
    n_rows = xs.shape[0] // ROW_TILE
    d, de = w_gate.shape[-2:]
    assert d == ROW_TILE * LANE
    wspec = lambda r, c: pl.BlockSpec((None, None, r, c), lambda b, be, nu: (layer, be[b], 0, 0))
    rows = pl.BlockSpec((bm * ROW_TILE, LANE), lambda b, be, nu: (b, 0))
    grid_spec = pltpu.PrefetchScalarGridSpec(
        num_scalar_prefetch=2,
        grid=(n_rows // bm,),
        in_specs=[rows, wspec(d, de), wspec(d, de), wspec(de, d)],
        out_specs=rows,
        scratch_shapes=[pltpu.VMEM((d, de), BF16), pltpu.VMEM((d, de), BF16), pltpu.VMEM((de, d), BF16)],
    )
    return pl.pallas_call(
        _expert_kernel,
        grid_spec=grid_spec,
        out_shape=jax.ShapeDtypeStruct(xs.shape, F32),
        compiler_params=_cparams(("arbitrary",)),
        name="moe_experts",
    )(blk_e, n_used, xs, w_gate, w_up, w_down)


COMB_TM = 256


def _tile_copy(src_hbm, row, dst, r, sem):
    return pltpu.make_async_copy(src_hbm.at[_tile_rows(row), :], dst.at[pl.ds(r * ROW_TILE, ROW_TILE), :], sem)


def _gather_start(src_hbm, row_of, dst, sem, n):
    for r in range(n):
        _tile_copy(src_hbm, row_of(r), dst, r, sem).start(priority=r % 2)


def _gather_wait(src_hbm, dst, sem, n):
    for r in range(n):
        _tile_copy(src_hbm, 0, dst, r, sem).wait()


def _combine_kernel(dest_ref, ys_hbm, x_ref, wt_ref, g2_ref, fw_ref, o_ref, buf_ref, sem_ref, *, n_tok, final_norm):
    i = pl.program_id(0)
    n = pl.num_programs(0)
    tm = x_ref.shape[0]
    slot = i % 2

    def start(tile, sl):
        for k in range(TOP_K):
            _gather_start(ys_hbm, lambda r: dest_ref[k * n_tok + tile * tm + r], buf_ref.at[sl, k], sem_ref.at[sl, k], tm)

    def wait(sl):
        for k in range(TOP_K):
            _gather_wait(ys_hbm, buf_ref.at[sl, k], sem_ref.at[sl, k], tm)

    @pl.when(i == 0)
    def _():
        start(0, 0)

    start(jnp.minimum(i + 1, n - 1), 1 - slot)
    wait(slot)
    wt = wt_ref[...]
    moe = (wt[:, 0:1] * _from_row_tiles(buf_ref.at[slot, 0], tm)
           + wt[:, 1:2] * _from_row_tiles(buf_ref.at[slot, 1], tm))
    out = x_ref[...] + g2_ref[...] * moe
    if final_norm:
        out = out * lax.rsqrt(jnp.mean(out * out, axis=-1, keepdims=True) + NORM_EPS) * fw_ref[...]
    o_ref[...] = out

    @pl.when(i == n - 1)
    def _():
        wait(1 - slot)


def moe_combine(ys, dest, x2, wts, g2, final_w, seq, final_norm):
    t, d = x2.shape
    tm = min(COMB_TM, seq)
    per_b = seq // tm
    grid_spec = pltpu.PrefetchScalarGridSpec(
        num_scalar_prefetch=1,
        grid=(t // tm,),
        in_specs=[
            pl.BlockSpec(memory_space=pl.ANY),
            pl.BlockSpec((tm, d), lambda i, ds: (i, 0)),
            pl.BlockSpec((tm, LANE), lambda i, ds: (i, 0)),
            pl.BlockSpec((None, 1, d), lambda i, ds: (i // per_b, 0, 0)),
            pl.BlockSpec((1, d), lambda i, ds: (0, 0)),
        ],
        out_specs=pl.BlockSpec((tm, d), lambda i, ds: (i, 0)),
        scratch_shapes=[pltpu.VMEM((2, TOP_K, tm * ROW_TILE, LANE), F32), pltpu.SemaphoreType.DMA((2, TOP_K))],
    )
    return pl.pallas_call(
        functools.partial(_combine_kernel, n_tok=t, final_norm=final_norm),
        grid_spec=grid_spec,
        out_shape=jax.ShapeDtypeStruct((t, d), F32),
        compiler_params=_cparams(("arbitrary",)),
        name="moe_combine",
    )(dest, ys, x2, wts, g2, final_w.reshape(1, d))


def kernel(x, c, w_ada, b_ada, norm_mix_w, norm_ffn_w, w_in, conv_w, conv_b, ssd_dt_bias, ssd_a_log, ssd_d, ssd_norm_w, diff_lambda, diff_subln_w, hgrn_lb_logits, hgrn_norm_w, w_merge, w_out, w_router, router_bias, w_expert_gate, w_expert_up, w_expert_down, final_norm_w):
    b, s, d = x.shape
    t = b * s
    depth = w_in.shape[0]
    mod = ada_modulation(c, w_ada, b_ada)
    lb_p = jax.nn.softmax(hgrn_lb_logits.astype(F32), axis=0)
    lower_bounds = jnp.cumsum(lb_p, axis=0) - lb_p[0]
    w_router_t = w_router.T
    x2 = x.reshape(t, d)
    ys = jnp.zeros(((TOP_K * t + N_EXPERTS * MOE_BM) * ROW_TILE, LANE), F32)
    for l in range(depth):
        sh1, sc1, g1, sh2, sc2, g2 = [mod[l, :, i * d:(i + 1) * d].reshape(b, 1, d) for i in range(6)]
        p2 = in_projection(x2, norm_mix_w[l], sc1, sh1, pad_w_in(w_in[l]), s)
        p3 = p2.reshape(b, s, -1)
        o_a = ssd_branch(p3, conv_w[l], conv_b[l], ssd_dt_bias[l], ssd_a_log[l], ssd_d[l], ssd_norm_w[l])
        o_b = dilated_branch(p3)
        o_c = diff_branch(p3, diff_lambda[l], diff_subln_w[l], l)
        o_d = hgrn_branch(p3, lower_bounds[l], hgrn_norm_w[l])
        flat = lambda o: o.reshape(t, BRANCH_W)
        x_mid, h2, idx, pos, wts, counts = merge_and_route(
            flat(o_a), flat(o_b), flat(o_c), flat(o_d), p2, x2, g1, w_merge[l].astype(BF16), w_out[l].astype(BF16),
            norm_ffn_w[l], sc2, sh2, w_router_t, router_bias, s)
        dest, blk_e, n_used = moe_plan(idx, pos, counts)
        xs = moe_dispatch(h2, dest, ys)
        ys = moe_experts(xs, blk_e, n_used, w_expert_gate, w_expert_up, w_expert_down, l)
        x2 = moe_combine(ys, dest, x_mid, wts, g2, final_norm_w, s, final_norm=(l == depth - 1))
    return x2.reshape(b, s, d)
```

```python
import functools
import math

import jax
import jax.numpy as jnp
from jax import lax
from jax.experimental import pallas as pl
from jax.experimental.pallas import tpu as pltpu

F32 = jnp.float32
BF16 = jnp.bfloat16

D_MODEL = 1024
DEPTH = 2
N_BRANCH = 4
BRANCH_W = 512

SSD_D_INNER = 512
SSD_HEAD_DIM = 64
SSD_HEADS = 8
SSD_GROUPS = 2
SSD_HPG = 4
SSD_STATE = 64
SSD_CONV = 4
SSD_CHUNK = 128
SSD_STEP_CHUNKS = 2
SSD_HALO = 16
SSD_CONV_CH = 768

DIL_PAIRS = ((128, 1), (512, 4), (2048, 16))
DIL_HPG = 4
DIL_HEAD_DIM = 128
DIL_HEADS = 12
DIL_BLOCK = 128

DIFF_HEADS = 4
DIFF_HEAD_DIM = 64

HG_HEADS = 8
HG_KEY_DIM = 64
HG_VAL_DIM = 64
HG_CHUNK = 64

N_EXPERTS = 64
N_EXPERT_GROUPS = 8
EXPERTS_PER_GROUP = 8
TOP_K = 2
D_EXPERT = 256

NORM_EPS = 1e-6
MASK_VALUE = -1e30

LANE = 128
VMEM_LIMIT = 48 * 1024 * 1024

C_GATE = 0
C_Z = 32
C_XBC = 36
C_DT = 42
C_QB = 44
C_KB = 56
C_VB = 68
C_QC = 80
C_KC = 84
C_VC = 88
C_QD = 92
C_FD = 96
C_ID = 100
C_GD = 104
N_COLB = 108
D_IN_PAD = N_COLB * LANE
DT_PAD = 2 * LANE


def _cparams(sem):
    return pltpu.CompilerParams(dimension_semantics=sem, vmem_limit_bytes=VMEM_LIMIT)


def _aligned(x, m):
    return x if isinstance(x, int) else pl.multiple_of(x, m)


def _sigmoid(v):
    return 0.5 * jnp.tanh(0.5 * v) + 0.5


def _silu(v):
    return v * _sigmoid(v)


def _bdot(a, b):
    return jnp.dot(a.astype(BF16), b.astype(BF16), preferred_element_type=F32)


def _bdot_nt(a, b):
    return lax.dot_general(a.astype(BF16), b.astype(BF16), (((1,), (1,)), ((), ())),
                           preferred_element_type=F32)


def _bdot_tn(a, b):
    return lax.dot_general(a.astype(BF16), b.astype(BF16), (((0,), (0,)), ((), ())),
                           preferred_element_type=F32)


def _dot_rhs01(x, m01):
    hi, lo = _split_bf16(x)
    return jnp.dot(hi, m01, preferred_element_type=F32) + jnp.dot(lo, m01, preferred_element_type=F32)


def _exact_rows_dot(m01, v):
    hi = v.astype(BF16)
    r1 = v - hi.astype(F32)
    mid = r1.astype(BF16)
    lo = (r1 - mid.astype(F32)).astype(BF16)
    m = m01.astype(BF16)
    return (jnp.dot(m, hi, preferred_element_type=F32) + jnp.dot(m, mid, preferred_element_type=F32)
            + jnp.dot(m, lo, preferred_element_type=F32))


def _ada_kernel(c_ref, w_ref, b_ref, o_ref):
    o_ref[...] = _bdot(_silu(c_ref[...]), w_ref[...]) + b_ref[...]


def ada_modulation(c, w_ada, b_ada):
    depth, d, n = w_ada.shape
    b = c.shape[0]
    bp = 8
    c_pad = jnp.zeros((bp, d), F32).at[:b].set(c)
    tn = 1536
    out = pl.pallas_call(
        _ada_kernel,
        grid=(depth, n // tn),
        in_specs=[
            pl.BlockSpec((bp, d), lambda l, j: (0, 0)),
            pl.BlockSpec((None, d, tn), lambda l, j: (l, 0, j)),
            pl.BlockSpec((None, 1, tn), lambda l, j: (l, 0, j)),
        ],
        out_specs=pl.BlockSpec((None, bp, tn), lambda l, j: (l, 0, j)),
        out_shape=jax.ShapeDtypeStruct((depth, bp, n), F32),
        compiler_params=_cparams(("arbitrary", "arbitrary")),
        name="ada_mod",
    )(c_pad, w_ada, b_ada.reshape(depth, 1, n))
    return out[:, :b]


INPROJ_TM = 2048
INPROJ_TN = 512


def _inproj_kernel(x_ref, nw_ref, sc_ref, sh_ref, w_ref, o_ref, h_ref):
    @pl.when(pl.program_id(1) == 0)
    def _():
        x = x_ref[...]
        y = x * lax.rsqrt(jnp.mean(x * x, axis=-1, keepdims=True) + NORM_EPS) * nw_ref[...]
        h_ref[...] = (y * (1.0 + sc_ref[...]) + sh_ref[...]).astype(BF16)

    o_ref[...] = jnp.dot(h_ref[...], w_ref[...], preferred_element_type=F32).astype(o_ref.dtype)


def in_projection(x2, norm_w, scale, shift, w_pad, seq):
    t, d = x2.shape
    n = w_pad.shape[1]
    tm = min(INPROJ_TM, seq)
    tn = INPROJ_TN
    per_b = seq // tm
    return pl.pallas_call(
        _inproj_kernel,
        grid=(t // tm, n // tn),
        in_specs=[
            pl.BlockSpec((tm, d), lambda i, j: (i, 0)),
            pl.BlockSpec((1, d), lambda i, j: (0, 0)),
            pl.BlockSpec((None, 1, d), lambda i, j: (i // per_b, 0, 0)),
            pl.BlockSpec((None, 1, d), lambda i, j: (i // per_b, 0, 0)),
            pl.BlockSpec((d, tn), lambda i, j: (0, j)),
        ],
        out_specs=pl.BlockSpec((tm, tn), lambda i, j: (i, j)),
        out_shape=jax.ShapeDtypeStruct((t, n), BF16),
        scratch_shapes=[pltpu.VMEM((tm, d), BF16)],
        compiler_params=_cparams(("arbitrary", "arbitrary")),
        name="in_proj",
    )(x2, norm_w.reshape(1, d), scale, shift, w_pad)


def pad_w_in(w_in_l):
    d = w_in_l.shape[0]
    o_dt = SSD_D_INNER + SSD_CONV_CH
    o_gate = w_in_l.shape[1] - N_BRANCH * D_MODEL
    return jnp.concatenate(
        [w_in_l[:, o_gate:], w_in_l[:, :o_dt + SSD_HEADS], jnp.zeros((d, DT_PAD - SSD_HEADS), w_in_l.dtype),
         w_in_l[:, o_dt + SSD_HEADS:o_gate]], axis=1).astype(BF16)


def _ssd_kernel(z_ref, xbc_ref, dt_ref, cw_ref, cb_ref, dtb_ref, alog_ref, dsk_ref, nw_ref, exp_ref, o_ref,
                xpad_ref, state_ref, y_ref):
    c = pl.program_id(1)
    L = SSD_CHUNK
    halo = SSD_HALO

    @pl.when(c == 0)
    def _():
        xpad_ref[L:L + halo, :] = jnp.zeros((halo, SSD_CONV_CH), BF16)
        state_ref[...] = jnp.zeros_like(state_ref)

    for ci in range(SSD_STEP_CHUNKS):
        rows = slice(ci * L, (ci + 1) * L)
        _ssd_chunk(z_ref[rows, :], xbc_ref[rows, :], dt_ref[rows, :], cw_ref, cb_ref, dtb_ref, alog_ref, dsk_ref, nw_ref,
                   exp_ref, o_ref, xpad_ref, state_ref, y_ref, rows)


def _ssd_chunk(z_in, xbc_in, dt_in, cw_ref, cb_ref, dtb_ref, alog_ref, dsk_ref, nw_ref, exp_ref, o_ref, xpad_ref, state_ref,
               y_ref, rows):
    L = SSD_CHUNK
    halo = SSD_HALO
    xpad_ref[0:halo, :] = xpad_ref[L:L + halo, :]
    z = z_in.astype(F32)
    xpad_ref[halo:halo + L, :] = xbc_in

    xpad = xpad_ref[...]
    t_i = lax.broadcasted_iota(jnp.int32, (L, halo + L), 0)
    m_i = lax.broadcasted_iota(jnp.int32, (L, halo + L), 1)
    conv = cb_ref[...] + cw_ref[SSD_CONV - 1:SSD_CONV, :] * xbc_in.astype(F32)
    for j in range(SSD_CONV - 1):
        shift = jnp.where(m_i == t_i + (halo - (SSD_CONV - 1) + j), 1.0, 0.0).astype(BF16)
        conv = conv + cw_ref[j:j + 1, :] * jnp.dot(shift, xpad, preferred_element_type=F32)
    xc = _silu(conv)
    xs = xc[:, 0:SSD_D_INNER]
    ns = SSD_GROUPS * SSD_STATE
    bm = xc[:, SSD_D_INNER:SSD_D_INNER + ns]
    cm = xc[:, SSD_D_INNER + ns:SSD_D_INNER + 2 * ns]

    dt = jax.nn.softplus(dt_in[:, 0:LANE].astype(F32) + dtb_ref[...])
    a = -jnp.exp(alog_ref[...])
    adt = dt * a
    row = lax.broadcasted_iota(jnp.int32, (L, L), 0)
    col = lax.broadcasted_iota(jnp.int32, (L, L), 1)
    tril = row >= col
    tril_f = jnp.where(tril, 1.0, 0.0)
    a_cs = _exact_rows_dot(tril_f, adt)
    a_cs_t = a_cs.T

    expand = exp_ref[...]
    dt_x = _dot_rhs01(dt, expand)
    a_cs_x = _exact_rows_dot(tril_f, _dot_rhs01(adt, expand))
    tot_x = a_cs_x[L - 1:L, :]
    xdt = xs * dt_x
    xdt_b = xdt.astype(BF16)
    xdec_b = (xdt * jnp.exp(tot_x - a_cs_x)).astype(BF16)
    dec_out = jnp.exp(a_cs_x)
    dec_tot = jnp.exp(tot_x)

    npair = SSD_HEADS // 2
    grp_of_pair = [(2 * p) // SSD_HPG for p in range(npair)]
    lane = lax.broadcasted_iota(jnp.int32, (L, LANE), 1)
    first = lane < SSD_STATE
    bm_b = bm.astype(BF16)
    cm_b = cm.astype(BF16)
    cbs = [lax.dot_general(jnp.where(first if g == 0 else jnp.logical_not(first), cm, 0.0).astype(BF16), bm_b,
                           (((1,), (1,)), ((), ())), preferred_element_type=F32) for g in range(SSD_GROUPS)]
    lmats = [jnp.exp(jnp.where(tril, jnp.broadcast_to(a_cs[:, e:e + 1], (L, L)) - a_cs_t[e:e + 1, :], MASK_VALUE))
             for e in range(SSD_HEADS)]
    ms = [(cbs[e // SSD_HPG] * lmats[e]).astype(BF16) for e in range(SSD_HEADS)]
    sls = [slice(p * LANE, (p + 1) * LANE) for p in range(npair)]
    diag = [jnp.dot(ms[e], xdt_b[:, sls[e // 2]], preferred_element_type=F32) for e in range(SSD_HEADS)]
    sts = [state_ref[p] for p in range(npair)]
    offs = [jnp.dot(cm_b, sts[p].astype(BF16), preferred_element_type=F32) for p in range(npair)]
    bm_t = bm.T.astype(BF16)
    locs = [jnp.dot(bm_t, xdec_b[:, sls[p]], preferred_element_type=F32) for p in range(npair)]
    sub = lax.broadcasted_iota(jnp.int32, (LANE, LANE), 0)
    for p in range(npair):
        own_rows = (sub < SSD_STATE) if grp_of_pair[p] == 0 else (sub >= SSD_STATE)
        state_ref[p] = dec_tot[:, sls[p]] * sts[p] + jnp.where(own_rows, locs[p], 0.0)
    for p in range(npair):
        y = jnp.where(first, diag[2 * p], diag[2 * p + 1]) + offs[p] * dec_out[:, sls[p]] + dsk_ref[:, sls[p]] * xs[:, sls[p]]
        y_ref[:, sls[p]] = y

    yz = y_ref[...] * _silu(z)
    o_ref[rows, :] = yz * lax.rsqrt(jnp.mean(yz * yz, axis=-1, keepdims=True) + NORM_EPS) * nw_ref[...]


def _pad_lanes(v, n=LANE):
    return jnp.zeros((1, n), F32).at[0, :v.shape[0]].set(v.astype(F32))


def ssd_branch(p3, conv_w, conv_b, dt_bias, a_log, d_skip, norm_w):
    b, s, _ = p3.shape
    L = SSD_CHUNK * SSD_STEP_CHUNKS
    const = lambda shape: pl.BlockSpec(shape, lambda i, c: (0, 0))
    assert SSD_HEAD_DIM == SSD_STATE and LANE == 2 * SSD_HEAD_DIM and SSD_HPG % 2 == 0
    import numpy as np
    expand = (np.arange(LANE)[:, None] == np.arange(SSD_D_INNER)[None, :] // SSD_HEAD_DIM).astype(np.float32)
    return pl.pallas_call(
        _ssd_kernel,
        grid=(b, s // L),
        in_specs=[
            pl.BlockSpec((None, L, SSD_D_INNER), lambda i, c: (i, c, C_Z * LANE // SSD_D_INNER)),
            pl.BlockSpec((None, L, SSD_CONV_CH), lambda i, c: (i, c, C_XBC * LANE // SSD_CONV_CH)),
            pl.BlockSpec((None, L, DT_PAD), lambda i, c: (i, c, C_DT * LANE // DT_PAD)),
            const((SSD_CONV, SSD_CONV_CH)),
            const((1, SSD_CONV_CH)),
            const((1, LANE)),
            const((1, LANE)),
            const((1, SSD_D_INNER)),
            const((1, SSD_D_INNER)),
            const((LANE, SSD_D_INNER)),
        ],
        out_specs=pl.BlockSpec((None, L, SSD_D_INNER), lambda i, c: (i, c, 0)),
        out_shape=jax.ShapeDtypeStruct((b, s, SSD_D_INNER), F32),
        scratch_shapes=[
            pltpu.VMEM((SSD_CHUNK + SSD_HALO, SSD_CONV_CH), BF16),
            pltpu.VMEM((SSD_HEADS // 2, LANE, LANE), F32),
            pltpu.VMEM((SSD_CHUNK, SSD_D_INNER), F32),
        ],
        compiler_params=_cparams(("arbitrary", "arbitrary")),
        name="ssd",
    )(p3, p3, p3, conv_w, conv_b.reshape(1, -1), _pad_lanes(dt_bias), _pad_lanes(a_log),
      jnp.repeat(d_skip.astype(F32), SSD_HEAD_DIM).reshape(1, -1), norm_w.reshape(1, -1), jnp.asarray(expand, BF16))


DIL_ORDER = (2, 1, 0)
DIL_CHEAP_STRIDE = 4
DIL_UNROLL = 8


def _dil_kernel(q_ref, k_ref, v_ref, o_ref, m_ref, l_ref, tmp_ref, tmp2_ref, qd_ref, kd_ref, vd_ref):
    g = pl.program_id(2)
    s_len = q_ref.shape[0]
    blk = DIL_BLOCK
    scale = DIL_HEAD_DIM ** -0.5

    def run_group(dil, first_group):
        sub = s_len // dil
        nblk = sub // blk

        if dil == 1:
            qd_ref[...] = (q_ref[...].astype(F32) * scale).astype(BF16)
            kd, vd = k_ref, v_ref
        else:
            for src, dst, mul in ((q_ref, qd_ref, scale), (k_ref, kd_ref, None), (v_ref, vd_ref, None)):
                x = src[...].astype(F32)
                tmp_ref[...] = x if mul is None else x * mul

                if dil <= DIL_CHEAP_STRIDE:
                    for r in range(dil):
                        dst[r * sub:(r + 1) * sub, :] = tmp_ref[pl.ds(r, sub, stride=dil), :].astype(BF16)
                else:
                    inner = DIL_CHEAP_STRIDE
                    outer = dil // inner
                    part = s_len // inner
                    for r_lo in range(inner):
                        tmp2_ref[r_lo * part:(r_lo + 1) * part, :] = tmp_ref[pl.ds(r_lo, part, stride=inner), :]
                    for r_lo in range(inner):
                        for r_hi in range(outer):
                            r = r_lo + inner * r_hi
                            dst[r * sub:(r + 1) * sub, :] = tmp2_ref[pl.ds(r_lo * part + r_hi, sub, stride=outer), :].astype(BF16)
            kd, vd = kd_ref, vd_ref

        def block_softmax(r, b, nkey):
            row0 = _aligned(r * sub + b * blk, blk)
            ks = pl.ds(_aligned(row0 - (nkey - blk), blk), nkey)
            i = lax.broadcasted_iota(jnp.int32, (blk, nkey), 0) + (nkey - blk)
            j = lax.broadcasted_iota(jnp.int32, (blk, nkey), 1)
            ok = jnp.logical_and(j <= i, j >= i - blk)
            sc = lax.dot_general(qd_ref[pl.ds(row0, blk), :], kd[ks, :], (((1,), (1,)), ((), ())),
                                 preferred_element_type=F32)
            sc = jnp.where(ok, sc, MASK_VALUE)
            m_blk = jnp.max(sc, axis=-1, keepdims=True)
            p = jnp.exp(sc - m_blk).astype(BF16)
            v_ext = jnp.concatenate([vd[ks, :], jnp.ones((nkey, LANE), BF16)], axis=1)
            return m_blk, jnp.dot(p, v_ext, preferred_element_type=F32)

        def merge(r, b, m_blk, pv):
            base = r + dil * blk * b
            acc_rows = pl.ds(_aligned(base, blk), blk) if dil == 1 else pl.ds(base, blk, stride=dil)
            m_blk = jnp.broadcast_to(m_blk, (blk, LANE))
            if first_group:
                o_ref[acc_rows, :] = pv[:, :LANE]
                l_ref[acc_rows, :] = pv[:, LANE:]
                m_ref[acc_rows, :] = m_blk
                return
            m_old = m_ref[acc_rows, :]
            m_new = jnp.maximum(m_old, m_blk)
            a_old = jnp.exp(m_old - m_new)
            a_blk = jnp.exp(m_blk - m_new)
            o_ref[acc_rows, :] = a_old * o_ref[acc_rows, :] + a_blk * pv[:, :LANE]
            l_ref[acc_rows, :] = a_old * l_ref[acc_rows, :] + a_blk * pv[:, LANE:]
            m_ref[acc_rows, :] = m_new

        def run_units(n, where, nkey):
            def group(units):
                parts = [block_softmax(r, b, nkey) for r, b in units]
                for (r, b), (m_blk, pv) in zip(units, parts):
                    merge(r, b, m_blk, pv)

            def body(i, carry):
                group([where(i * DIL_UNROLL + j) for j in range(DIL_UNROLL)])
                return carry

            lax.fori_loop(0, n // DIL_UNROLL, body, 0)
            if n % DIL_UNROLL:
                group([where(n - n % DIL_UNROLL + j) for j in range(n % DIL_UNROLL)])

        run_units(dil, lambda u: (u, 0), blk)
        if nblk > 1:
            run_units(dil * (nblk - 1), lambda u: (u // (nblk - 1), 1 + u % (nblk - 1)), 2 * blk)

    for step, gi in enumerate(DIL_ORDER):
        window, dil = DIL_PAIRS[gi]
        assert window // dil == blk

        @pl.when(g == step)
        def _(dil=dil, step=step):
            run_group(dil, step == 0)

    @pl.when(g == len(DIL_PAIRS) - 1)
    def _():
        o_ref[...] = o_ref[...] / l_ref[...]


def dilated_branch(p3):
    b, s, _ = p3.shape
    ng = len(DIL_PAIRS)
    last = len(DIL_PAIRS) - 1
    assert DIL_ORDER == tuple(range(last, -1, -1))
    spec = lambda c0: pl.BlockSpec((None, s, DIL_HEAD_DIM), lambda i, j, g: (i, 0, c0 + (last - g) * DIL_HPG + j))
    return pl.pallas_call(
        _dil_kernel,
        grid=(b, DIL_HPG, ng),
        in_specs=[spec(C_QB), spec(C_KB), spec(C_VB)],
        out_specs=pl.BlockSpec((None, s, DIL_HEAD_DIM), lambda i, j, g: (i, 0, j)),
        out_shape=jax.ShapeDtypeStruct((b, s, BRANCH_W), F32),
        scratch_shapes=[pltpu.VMEM((s, LANE), F32), pltpu.VMEM((s, LANE), F32), pltpu.VMEM((s, LANE), F32),
                        pltpu.VMEM((s, LANE), F32), pltpu.VMEM((s, LANE), BF16), pltpu.VMEM((s, LANE), BF16), pltpu.VMEM((s, LANE), BF16)],
        compiler_params=_cparams(("arbitrary", "arbitrary", "arbitrary")),
        name="dilated",
    )(p3, p3, p3)


DIFF_TQ = 512
DIFF_TK = 1024
DIFF_DIAG = 512
LOG2E = 1.4426950408889634


DIFF_ONES = 16


def _diff_kernel(lam_ref, q_ref, k_ref, v_ref, nw_ref, o_ref, vt_ref, m_ref, acc_ref, *, lam_init):
    qi = pl.program_id(2)
    tq, tk, dh = DIFF_TQ, DIFF_TK, DIFF_HEAD_DIM
    w = 2 * dh
    s_len = k_ref.shape[0]

    @pl.when(qi == 0)
    def _():
        for c0 in range(0, s_len, tk):
            vt_ref[0:w, c0:c0 + tk] = v_ref[c0:c0 + tk, :].astype(F32).T.astype(BF16)
        vt_ref[w:w + DIFF_ONES, :] = jnp.ones((DIFF_ONES, s_len), BF16)

    q_t = (q_ref[...].astype(F32) * (dh ** -0.5 * LOG2E)).T
    half = lax.broadcasted_iota(jnp.int32, (w, tq), 0) < dh
    q_sel = [jnp.where(half, q_t, 0.0).astype(BF16), jnp.where(half, 0.0, q_t).astype(BF16)]
    for t in range(2):
        m_ref[t] = jnp.full((8, tq), MASK_VALUE, F32)
        acc_ref[t] = jnp.zeros((w + DIFF_ONES, tq), F32)

    def block(kstart, width, qlo=None):
        c0 = qlo or 0
        nq = tq - c0
        kb = k_ref[pl.ds(kstart, width), :]
        vb = vt_ref[:, pl.ds(kstart, width)]
        scores = [jnp.dot(kb, q_sel[t][:, c0:], preferred_element_type=F32) for t in range(2)]
        if qlo is not None:
            row = lax.broadcasted_iota(jnp.int32, (width, nq), 0)
            col = lax.broadcasted_iota(jnp.int32, (width, nq), 1)
            scores = [jnp.where(row <= col, sc, MASK_VALUE) for sc in scores]
        m_blks = [jnp.max(jnp.max(sc.reshape(width // 8, 8, nq), axis=0), axis=0, keepdims=True) for sc in scores]
        pvs = [jnp.dot(vb, jnp.exp2(sc - m_blk).astype(BF16), preferred_element_type=F32)
               for sc, m_blk in zip(scores, m_blks)]
        for t, (m_blk, pv) in enumerate(zip(m_blks, pvs)):
            m_old = m_ref[t, 0:1, c0:]
            m_new = jnp.maximum(m_old, m_blk)
            acc_ref[t, :, c0:] = acc_ref[t, :, c0:] * jnp.exp2(m_old - m_new) + pv * jnp.exp2(m_blk - m_new)
            m_ref[t, :, c0:] = jnp.broadcast_to(m_new, (8, nq))

    def main_step(j, carry):
        block(pl.multiple_of(j * tk, tk), tk)
        return carry

    per = tk // tq
    lax.fori_loop(0, qi // per, main_step, 0)
    for r in range(1, per):
        @pl.when(qi % per >= r)
        def _(r=r):
            block(pl.multiple_of((qi // per) * tk + (r - 1) * tq, tq), tq)
    for c0 in range(0, tq, DIFF_DIAG):
        block(pl.multiple_of(qi * tq + c0, DIFF_DIAG), DIFF_DIAG, qlo=c0)

    lam_p = lam_ref[...]
    lam = (jnp.exp(jnp.sum(lam_p[0:1] * lam_p[1:2], axis=-1, keepdims=True))
           - jnp.exp(jnp.sum(lam_p[2:3] * lam_p[3:4], axis=-1, keepdims=True)) + lam_init)
    a0, a1 = acc_ref[0], acc_ref[1]
    o_t = a0[0:w] / a0[w:w + 1] - lam * (a1[0:w] / a1[w:w + 1])
    o = o_t.T
    o = o * lax.rsqrt(jnp.mean(o * o, axis=-1, keepdims=True) + NORM_EPS) * nw_ref[...]
    o_ref[...] = o * (1.0 - lam_init)


def diff_branch(p3, diff_lambda, subln_w, layer):
    b, s, _ = p3.shape
    lam_init = 0.8 - 0.6 * math.exp(-0.3 * layer)
    w = 2 * DIFF_HEAD_DIM
    return pl.pallas_call(
        functools.partial(_diff_kernel, lam_init=lam_init),
        grid=(b, DIFF_HEADS, s // DIFF_TQ),
        in_specs=[
            pl.BlockSpec((4, DIFF_HEAD_DIM), lambda i, h, t: (0, 0)),
            pl.BlockSpec((None, DIFF_TQ, w), lambda i, h, t: (i, t, C_QC + h)),
            pl.BlockSpec((None, s, w), lambda i, h, t: (i, 0, C_KC + h)),
            pl.BlockSpec((None, s, w), lambda i, h, t: (i, 0, C_VC + h)),
            pl.BlockSpec((1, w), lambda i, h, t: (0, 0)),
        ],
        out_specs=pl.BlockSpec((None, DIFF_TQ, w), lambda i, h, t: (i, t, h)),
        out_shape=jax.ShapeDtypeStruct((b, s, BRANCH_W), F32),
        scratch_shapes=[pltpu.VMEM((w + DIFF_ONES, s), BF16), pltpu.VMEM((2, 8, DIFF_TQ), F32),
                        pltpu.VMEM((2, w + DIFF_ONES, DIFF_TQ), F32)],
        compiler_params=_cparams(("arbitrary", "arbitrary", "arbitrary")),
        name="diff_attn",
    )(diff_lambda, p3, p3, p3, subln_w.reshape(1, w))


HG_LEVELS = 6
HG_STEP_CHUNKS = 2


def _hgrn_tables():
    import numpy as np
    c = HG_CHUNK
    i = np.arange(c)[:, None]
    j = np.arange(c)[None, :]
    tril = (j <= i).astype(np.float32)
    masks = []
    for lv in range(HG_LEVELS):
        sz = 1 << lv
        m = ((i // (2 * sz) == j // (2 * sz)) & (i // sz == j // sz + 1)).astype(np.float32)
        masks.append(np.concatenate([m, m], axis=0))
    lane = np.arange(LANE)
    same_head = (lane[:, None] // HG_KEY_DIM == lane[None, :] // HG_KEY_DIM).astype(np.float32)
    return tril, np.stack(masks, axis=0), same_head


def _hgrn_kernel(q_ref, f_ref, i_ref, g_ref, lb_ref, tri_ref, msk_ref, sh_ref, nw_ref, o_ref, state_ref, gpad_ref):
    @pl.when(pl.program_id(1) == 0)
    def _():
        state_ref[...] = jnp.zeros_like(state_ref)

    for ci in range(HG_STEP_CHUNKS):
        rows = slice(ci * HG_CHUNK, (ci + 1) * HG_CHUNK)
        _hgrn_chunk(q_ref[rows, :], f_ref[rows, :], i_ref[rows, :], g_ref, lb_ref, tri_ref, msk_ref, sh_ref, nw_ref,
                    o_ref, state_ref, gpad_ref, rows)


def _hgrn_chunk(q_in, f_in, i_in, g_ref, lb_ref, tri_ref, msk_ref, sh_ref, nw_ref, o_ref, state_ref, gpad_ref, rows):
    cs = HG_CHUNK
    nl = HG_LEVELS
    w = q_in.shape[1]
    sub_rows = 8

    lb = lb_ref[...]
    f_gate = lb + (1.0 - lb) * jax.nn.sigmoid(f_in.astype(F32))
    log_f = jnp.log(f_gate)
    k_in = 1.0 - f_gate
    q = _silu(q_in.astype(F32))
    v = i_in.astype(F32)

    hi, lo = _split_bf16(log_f)
    tri = tri_ref[...]
    g = jnp.dot(tri, hi, preferred_element_type=F32) + jnp.dot(tri, lo, preferred_element_type=F32)
    gpad_ref[0:sub_rows, :] = jnp.zeros((sub_rows, w), F32)
    gpad_ref[sub_rows:sub_rows + cs, :] = g
    g_last = g[cs - 1:cs, :]

    def g_row(r):
        return jnp.broadcast_to(gpad_ref[sub_rows + r:sub_rows + r + 1, :], (sub_rows, w))

    sub = lax.broadcasted_iota(jnp.int32, (sub_rows, w), 0)
    tiles = [[] for _ in range(nl)]
    for t in range(cs // sub_rows):
        r0 = t * sub_rows
        gt = g[r0:r0 + sub_rows]
        prev = gpad_ref[r0 + sub_rows - 1:r0 + 2 * sub_rows - 1, :]
        tiles[0].append(jnp.where(sub % 2 == 1, gt - prev, 0.0))
        c1 = jnp.where(sub < 4, g_row(r0 + 1), g_row(r0 + 5))
        tiles[1].append(jnp.where((sub // 2) % 2 == 1, gt - c1, c1 - gt))
        c2 = g_row(r0 + 3)
        tiles[2].append(jnp.where(sub >= 4, gt - c2, c2 - gt))
        for lv in range(3, nl):
            span = (1 << lv) // sub_rows
            mid = (t // (2 * span)) * 2 * span + span
            cm = g_row(mid * sub_rows - 1)
            tiles[lv].append(gt - cm if (t // span) % 2 == 1 else cm - gt)
    dec = [jnp.exp(jnp.concatenate(tl, axis=0)) for tl in tiles]
    first_w = lax.broadcasted_iota(jnp.int32, (cs, w), 1) % LANE < HG_KEY_DIM
    q_even = jnp.where(first_w, q, 0.0).astype(BF16)
    q_odd = jnp.where(first_w, 0.0, q).astype(BF16)
    k_b = k_in.astype(BF16)
    dec_b = [d.astype(BF16) for d in dec]
    q_full = q * jnp.exp(g)
    k_full = k_in * jnp.exp(g_last - g)
    decay_row = jnp.exp(g_last)

    same_head = sh_ref[...]
    same_head_b = same_head.astype(BF16)
    first = lax.broadcasted_iota(jnp.int32, (cs, LANE), 1) < HG_KEY_DIM
    pairs = range(w // LANE)
    sls = [slice(p * LANE, (p + 1) * LANE) for p in pairs]
    attns = [jnp.zeros((2 * cs, cs), F32) for _ in pairs]
    for lv in range(nl):
        for p in pairs:
            db = dec_b[lv][:, sls[p]]
            lhs = jnp.concatenate([q_even[:, sls[p]] * db, q_odd[:, sls[p]] * db], axis=0)
            scores = lax.dot_general(lhs, k_b[:, sls[p]] * db, (((1,), (1,)), ((), ())), preferred_element_type=F32)
            attns[p] = attns[p] + msk_ref[lv] * scores
    vps = [v[:, sl] for sl in sls]
    vpbs = [vp.astype(BF16) for vp in vps]
    rs = [jnp.dot(attns[p].astype(BF16), vpbs[p], preferred_element_type=F32) for p in pairs]
    diags = [jnp.dot((q[:, sl] * k_in[:, sl]).astype(BF16), same_head_b, preferred_element_type=F32) for sl in sls]
    sts = [state_ref[p] for p in pairs]
    inters = [_bdot_nt(q_full[:, sls[p]], sts[p]) for p in pairs]
    upds = [_bdot_tn(vpbs[p], k_full[:, sls[p]]) for p in pairs]
    for p in pairs:
        state_ref[p] = decay_row[:, sls[p]] * sts[p] + same_head * upds[p]
    outs = [jnp.where(first, rs[p][0:cs], rs[p][cs:2 * cs]) + diags[p] * vps[p] + inters[p] for p in pairs]
    mss = [jnp.dot((o * o).astype(BF16), same_head_b, preferred_element_type=F32) * (1.0 / HG_VAL_DIM) for o in outs]
    for p in pairs:
        o = outs[p] * lax.rsqrt(mss[p] + NORM_EPS) * nw_ref[...]
        o_ref[rows, sls[p]] = o * _silu(g_ref[rows, sls[p]].astype(F32))


def hgrn_branch(p3, lower_bound, norm_w):
    b, s, _ = p3.shape
    cs = HG_CHUNK * HG_STEP_CHUNKS
    w = HG_HEADS * HG_KEY_DIM
    assert HG_KEY_DIM == HG_VAL_DIM and LANE == 2 * HG_KEY_DIM
    tril, masks, same_head = _hgrn_tables()
    seg = lambda c0: pl.BlockSpec((None, cs, w), lambda i, c: (i, c, c0 * LANE // w))
    return pl.pallas_call(
        _hgrn_kernel,
        grid=(b, s // cs),
        in_specs=[
            seg(C_QD), seg(C_FD), seg(C_ID), seg(C_GD),
            pl.BlockSpec((1, w), lambda i, c: (0, 0)),
            pl.BlockSpec(tril.shape, lambda i, c: (0, 0)),
            pl.BlockSpec(masks.shape, lambda i, c: (0, 0, 0)),
            pl.BlockSpec(same_head.shape, lambda i, c: (0, 0)),
            pl.BlockSpec((1, LANE), lambda i, c: (0, 0)),
        ],
        out_specs=pl.BlockSpec((None, cs, w), lambda i, c: (i, c, 0)),
        out_shape=jax.ShapeDtypeStruct((b, s, BRANCH_W), F32),
        scratch_shapes=[pltpu.VMEM((w // LANE, LANE, LANE), F32), pltpu.VMEM((HG_CHUNK + 8, w), F32)],
        compiler_params=_cparams(("arbitrary", "arbitrary")),
        name="hgrn2",
    )(p3, p3, p3, p3, lower_bound.reshape(1, w).astype(F32), jnp.asarray(tril, BF16), jnp.asarray(masks, F32),
      jnp.asarray(same_head, F32), jnp.tile(norm_w.astype(F32), LANE // HG_VAL_DIM).reshape(1, LANE))


MERGE_TM = 512


def _split_bf16(v):
    hi = v.astype(BF16)
    return hi, (v - hi.astype(F32)).astype(BF16)


def _first_index_of_max(vals, iota, n):
    top = jnp.max(vals, axis=0, keepdims=True)
    idx = jnp.min(jnp.where(vals == top, iota, n), axis=0, keepdims=True)
    return top, idx


def _merge_kernel(oa_ref, ob_ref, oc_ref, od_ref, gl_ref, x_ref, g1_ref, wm_ref, wo_ref,
                  nw_ref, sc_ref, sh_ref, wr_ref, rb_ref,
                  xo_ref, h2_ref, idx_ref, pos_ref, wt_ref, cnt_ref, carry_ref):
    step = pl.program_id(0)
    tm = x_ref.shape[0]
    d = D_MODEL

    projs = [jnp.dot(o_ref[...].astype(BF16), wm_ref[n], preferred_element_type=F32)
             for n, o_ref in enumerate((oa_ref, ob_ref, oc_ref, od_ref))]
    acc = jnp.zeros((tm, d), F32)
    for n, proj in enumerate(projs):
        acc = acc + _sigmoid(gl_ref[:, n * d:(n + 1) * d].astype(F32)) * proj
    mix = jnp.dot(acc.astype(BF16), wo_ref[...], preferred_element_type=F32)
    x_new = x_ref[...] + g1_ref[...] * mix
    xo_ref[...] = x_new

    y = x_new * lax.rsqrt(jnp.mean(x_new * x_new, axis=-1, keepdims=True) + NORM_EPS) * nw_ref[...]
    h2 = y * (1.0 + sc_ref[...]) + sh_ref[...]
    _to_row_tiles(h2_ref, h2)

    h_hi, h_lo = _split_bf16(h2)
    w_hi, w_lo = _split_bf16(wr_ref[...])
    nt = lambda a, b: lax.dot_general(a, b, (((1,), (1,)), ((), ())), preferred_element_type=F32)
    logits = nt(w_hi, h_hi) + nt(w_hi, h_lo) + nt(w_lo, h_hi)
    scores = jax.nn.sigmoid(logits)
    sel = scores + rb_ref[...]

    ne, ng, pg = N_EXPERTS, N_EXPERT_GROUPS, EXPERTS_PER_GROUP
    iota_g = lax.broadcasted_iota(jnp.int32, (pg, tm), 0)
    best_score = None
    best_group = None
    for g in range(ng):
        xg = sel[g * pg:(g + 1) * pg, :]
        top1, i1 = _first_index_of_max(xg, iota_g, pg)
        top2 = jnp.max(jnp.where(iota_g == i1, -jnp.inf, xg), axis=0, keepdims=True)
        gs = top1 + top2
        if g == 0:
            best_score, best_group = gs, jnp.zeros((1, tm), jnp.int32)
        else:
            better = gs > best_score
            best_score = jnp.where(better, gs, best_score)
            best_group = jnp.where(better, g, best_group)

    iota_e = lax.broadcasted_iota(jnp.int32, (ne, tm), 0)
    masked = jnp.where(iota_e // pg == best_group, sel, MASK_VALUE)
    _, e1 = _first_index_of_max(masked, iota_e, ne)
    oh1 = iota_e == e1
    _, e2 = _first_index_of_max(jnp.where(oh1, MASK_VALUE, masked), iota_e, ne)
    oh2 = iota_e == e2
    w1 = jnp.sum(jnp.where(oh1, scores, 0.0), axis=0, keepdims=True)
    w2 = jnp.sum(jnp.where(oh2, scores, 0.0), axis=0, keepdims=True)
    wsum = w1 + w2
    w1 = w1 / wsum
    w2 = w2 / wsum

    @pl.when(step == 0)
    def _():
        carry_ref[...] = jnp.zeros_like(carry_ref)

    f1 = jnp.where(oh1, 1.0, 0.0)
    f2 = jnp.where(oh2, 1.0, 0.0)
    both = f1 + f2
    r_i = lax.broadcasted_iota(jnp.int32, (tm, tm), 0)
    c_i = lax.broadcasted_iota(jnp.int32, (tm, tm), 1)
    before = jnp.where(r_i < c_i, 1.0, 0.0).astype(BF16)
    rank = jnp.dot(both.astype(BF16), before, preferred_element_type=F32) + carry_ref[:, 0:1]
    p1 = jnp.sum(f1 * rank, axis=0, keepdims=True)
    p2 = jnp.sum(f2 * rank, axis=0, keepdims=True)
    carry_new = carry_ref[...] + jnp.sum(both, axis=1, keepdims=True)
    carry_ref[...] = carry_new
    cnt_ref[...] = carry_new.astype(jnp.int32)

    idx_ref[...] = jnp.concatenate([e1, e2], axis=0)
    pos_ref[...] = jnp.concatenate([p1, p2], axis=0).astype(jnp.int32)
    wt_rows = jnp.concatenate([w1, w2, jnp.zeros((LANE - 2, tm), F32)], axis=0)
    wt_ref[...] = wt_rows.T


def merge_and_route(o_a, o_b, o_c, o_d, p2, x2, g1, w_merge_bf, w_out_bf, norm_w, sc2, sh2, w_router_t, router_bias, seq):
    t, d = x2.shape
    tm = min(MERGE_TM, seq)
    per_b = seq // tm
    gw = N_BRANCH * d
    tok = lambda w: pl.BlockSpec((tm, w), lambda i: (i, 0))
    bat = pl.BlockSpec((None, 1, d), lambda i: (i // per_b, 0, 0))
    ne = N_EXPERTS
    return pl.pallas_call(
        _merge_kernel,
        grid=(t // tm,),
        in_specs=[
            tok(BRANCH_W), tok(BRANCH_W), tok(BRANCH_W), tok(BRANCH_W),
            pl.BlockSpec((tm, gw), lambda i: (i, C_GATE * LANE // gw)),
            tok(d), bat,
            pl.BlockSpec((N_BRANCH, BRANCH_W, d), lambda i: (0, 0, 0)),
            pl.BlockSpec((d, d), lambda i: (0, 0)),
            pl.BlockSpec((1, d), lambda i: (0, 0)),
            bat, bat,
            pl.BlockSpec((ne, d), lambda i: (0, 0)),
            pl.BlockSpec((ne, 1), lambda i: (0, 0)),
        ],
        out_specs=[
            tok(d), pl.BlockSpec((tm * ROW_TILE, LANE), lambda i: (i, 0)),
            pl.BlockSpec((TOP_K, tm), lambda i: (0, i)),
            pl.BlockSpec((TOP_K, tm), lambda i: (0, i)),
            tok(LANE),
            pl.BlockSpec((ne, LANE), lambda i: (0, 0)),
        ],
        out_shape=[
            jax.ShapeDtypeStruct((t, d), F32),
            jax.ShapeDtypeStruct((t * ROW_TILE, LANE), F32),
            jax.ShapeDtypeStruct((TOP_K, t), jnp.int32),
            jax.ShapeDtypeStruct((TOP_K, t), jnp.int32),
            jax.ShapeDtypeStruct((t, LANE), F32),
            jax.ShapeDtypeStruct((ne, LANE), jnp.int32),
        ],
        scratch_shapes=[pltpu.VMEM((ne, LANE), F32)],
        compiler_params=_cparams(("arbitrary",)),
        name="merge_route",
    )(o_a, o_b, o_c, o_d, p2, x2, g1, w_merge_bf, w_out_bf, norm_w.reshape(1, d), sc2, sh2,
      w_router_t, router_bias.reshape(ne, 1))


MOE_BM = 256


def _plan_kernel(idx_ref, pos_ref, cnt_ref, dest_ref, blke_ref, nused_ref):
    bm = MOE_BM
    ne = N_EXPERTS
    cnt = cnt_ref[...].astype(F32)
    padded = jnp.floor((cnt + (bm - 1)) * (1.0 / bm)) * bm
    r = lax.broadcasted_iota(jnp.int32, (ne, ne), 0)
    c = lax.broadcasted_iota(jnp.int32, (ne, ne), 1)
    pstart = _exact_rows_dot(jnp.where(c < r, 1.0, 0.0), padded)
    pend = pstart + padded

    idx = idx_ref[...]
    base = jnp.zeros(idx.shape, F32)
    for e in range(ne):
        base = jnp.where(idx == e, pstart[e:e + 1, 0:1], base)
    dest_ref[...] = base.astype(jnp.int32) + pos_ref[...]

    nbp = blke_ref.shape[1]
    blk_start = (lax.broadcasted_iota(jnp.int32, (ne, nbp), 1) * bm).astype(F32)
    done = jnp.sum(jnp.where(pend[:, 0:1] <= blk_start, 1.0, 0.0), axis=0, keepdims=True)
    blke_ref[...] = jnp.minimum(done, ne - 1.0).astype(jnp.int32)
    nused_ref[...] = (pend[ne - 1:ne, :] * (1.0 / bm)).astype(jnp.int32)


def moe_plan(idx, pos, counts):
    n_tok = idx.shape[1]
    n_rows = TOP_K * n_tok + N_EXPERTS * MOE_BM
    n_blocks = n_rows // MOE_BM
    nbp = -(-n_blocks // LANE) * LANE
    dest, blk_e, n_used = pl.pallas_call(
        _plan_kernel,
        out_shape=[
            jax.ShapeDtypeStruct((TOP_K, n_tok), jnp.int32),
            jax.ShapeDtypeStruct((1, nbp), jnp.int32),
            jax.ShapeDtypeStruct((1, LANE), jnp.int32),
        ],
        compiler_params=pltpu.CompilerParams(vmem_limit_bytes=VMEM_LIMIT),
        name="moe_plan",
    )(idx, pos, counts)
    return dest.reshape(-1), blk_e[0, :n_blocks], n_used[0, :1]


ROW_TILE = 8
DISPATCH_TM = 512


def _to_row_tiles(ref, x):
    rows = x.shape[0]
    for j in range(ROW_TILE):
        ref[pl.ds(j, rows, stride=ROW_TILE), :] = x[:, j * LANE:(j + 1) * LANE]


def _from_row_tiles(ref, rows):
    return jnp.concatenate([ref[pl.ds(j, rows, stride=ROW_TILE), :] for j in range(ROW_TILE)], axis=1)


def _tile_rows(row):
    return pl.ds(pl.multiple_of(row * ROW_TILE, ROW_TILE), ROW_TILE)


def _dispatch_kernel(dest_ref, h_ref, xs_init_hbm, xs_hbm, sem_ref, *, n_tok):
    del xs_init_hbm
    i = pl.program_id(0)
    tm = h_ref.shape[0] // ROW_TILE

    def copy(r, k):
        row = dest_ref[k * n_tok + i * tm + r]
        return pltpu.make_async_copy(h_ref.at[pl.ds(r * ROW_TILE, ROW_TILE), :], xs_hbm.at[_tile_rows(row), :], sem_ref.at[k])

    for r in range(tm):
        for k in range(TOP_K):
            copy(r, k).start(priority=k)
    for r in range(tm):
        for k in range(TOP_K):
            copy(r, k).wait()


def moe_dispatch(h2_tiles, dest, xs_init):
    n_tok = dest.shape[0] // TOP_K
    tm = min(DISPATCH_TM, n_tok)
    grid_spec = pltpu.PrefetchScalarGridSpec(
        num_scalar_prefetch=1,
        grid=(n_tok // tm,),
        in_specs=[pl.BlockSpec((tm * ROW_TILE, LANE), lambda i, ds: (i, 0)), pl.BlockSpec(memory_space=pl.ANY)],
        out_specs=pl.BlockSpec(memory_space=pl.ANY),
        scratch_shapes=[pltpu.SemaphoreType.DMA((TOP_K,))],
    )
    return pl.pallas_call(
        functools.partial(_dispatch_kernel, n_tok=n_tok),
        grid_spec=grid_spec,
        out_shape=jax.ShapeDtypeStruct(xs_init.shape, F32),
        input_output_aliases={2: 0},
        compiler_params=_cparams(("arbitrary",)),
        name="moe_dispatch",
    )(dest, h2_tiles, xs_init)


def _expert_kernel(blke_ref, nused_ref, x_ref, wg_ref, wu_ref, wd_ref, y_ref, wgb_ref, wub_ref, wdb_ref):
    b = pl.program_id(0)
    bm = MOE_BM
    used = nused_ref[0]
    changed = jnp.logical_or(b == 0, blke_ref[b] != blke_ref[jnp.maximum(b - 1, 0)])

    @pl.when(jnp.logical_and(b < used, changed))
    def _():
        wgb_ref[...] = wg_ref[...].astype(BF16)
        wub_ref[...] = wu_ref[...].astype(BF16)
        wdb_ref[...] = wd_ref[...].astype(BF16)

    @pl.when(b < used)
    def _():
        xb = _from_row_tiles(x_ref, bm).astype(BF16)
        hid = _silu(jnp.dot(xb, wgb_ref[...], preferred_element_type=F32)) * jnp.dot(xb, wub_ref[...], preferred_element_type=F32)
        _to_row_tiles(y_ref, jnp.dot(hid.astype(BF16), wdb_ref[...], preferred_element_type=F32))

    @pl.when(b >= used)
    def _():
        y_ref[...] = jnp.zeros_like(y_ref)


def moe_experts(xs, blk_e, n_used, w_gate, w_up, w_down, layer):
    bm = MOE_BM
    n_rows = xs.shape[0] // ROW_TILE
    d, de = w_gate.shape[-2:]
    assert d == ROW_TILE * LANE
    wspec = lambda r, c: pl.BlockSpec((None, None, r, c), lambda b, be, nu: (layer, be[b], 0, 0))
    rows = pl.BlockSpec((bm * ROW_TILE, LANE), lambda b, be, nu: (b, 0))
    rows_in = pl.BlockSpec((bm * ROW_TILE, LANE), lambda b, be, nu: (jnp.minimum(b, nu[0]), 0))
    grid_spec = pltpu.PrefetchScalarGridSpec(
        num_scalar_prefetch=2,
        grid=(n_rows // bm,),
        in_specs=[rows_in, wspec(d, de), wspec(d, de), wspec(de, d)],
        out_specs=rows,
        scratch_shapes=[pltpu.VMEM((d, de), BF16), pltpu.VMEM((d, de), BF16), pltpu.VMEM((de, d), BF16)],
    )
    return pl.pallas_call(
        _expert_kernel,
        grid_spec=grid_spec,
        out_shape=jax.ShapeDtypeStruct(xs.shape, F32),
        compiler_params=_cparams(("arbitrary",)),
        name="moe_experts",
    )(blk_e, n_used, xs, w_gate, w_up, w_down)


COMB_TM = 256


def _tile_copy(src_hbm, row, dst, r, sem):
    return pltpu.make_async_copy(src_hbm.at[_tile_rows(row), :], dst.at[pl.ds(r * ROW_TILE, ROW_TILE), :], sem)


def _gather_start(src_hbm, row_of, dst, sem, n):
    for r in range(n):
        _tile_copy(src_hbm, row_of(r), dst, r, sem).start(priority=r % 2)


def _gather_wait(src_hbm, dst, sem, n):
    for r in range(n):
        _tile_copy(src_hbm, 0, dst, r, sem).wait()


def _combine_kernel(dest_ref, ys_hbm, x_ref, wt_ref, g2_ref, fw_ref, o_ref, buf_ref, sem_ref, *, n_tok, final_norm):
    i = pl.program_id(0)
    n = pl.num_programs(0)
    tm = x_ref.shape[0]
    slot = i % 2

    def start(tile, sl):
        for k in range(TOP_K):
            _gather_start(ys_hbm, lambda r: dest_ref[k * n_tok + tile * tm + r], buf_ref.at[sl, k], sem_ref.at[sl, k], tm)

    def wait(sl):
        for k in range(TOP_K):
            _gather_wait(ys_hbm, buf_ref.at[sl, k], sem_ref.at[sl, k], tm)

    @pl.when(i == 0)
    def _():
        start(0, 0)

    start(jnp.minimum(i + 1, n - 1), 1 - slot)
    wait(slot)
    wt = wt_ref[...]
    moe = (wt[:, 0:1] * _from_row_tiles(buf_ref.at[slot, 0], tm)
           + wt[:, 1:2] * _from_row_tiles(buf_ref.at[slot, 1], tm))
    out = x_ref[...] + g2_ref[...] * moe
    if final_norm:
        out = out * lax.rsqrt(jnp.mean(out * out, axis=-1, keepdims=True) + NORM_EPS) * fw_ref[...]
    o_ref[...] = out

    @pl.when(i == n - 1)
    def _():
        wait(1 - slot)


def moe_combine(ys, dest, x2, wts, g2, final_w, seq, final_norm):
    t, d = x2.shape
    tm = min(COMB_TM, seq)
    per_b = seq // tm
    grid_spec = pltpu.PrefetchScalarGridSpec(
        num_scalar_prefetch=1,
        grid=(t // tm,),
        in_specs=[
            pl.BlockSpec(memory_space=pl.ANY),
            pl.BlockSpec((tm, d), lambda i, ds: (i, 0)),
            pl.BlockSpec((tm, LANE), lambda i, ds: (i, 0)),
            pl.BlockSpec((None, 1, d), lambda i, ds: (i // per_b, 0, 0)),
            pl.BlockSpec((1, d), lambda i, ds: (0, 0)),
        ],
        out_specs=pl.BlockSpec((tm, d), lambda i, ds: (i, 0)),
        scratch_shapes=[pltpu.VMEM((2, TOP_K, tm * ROW_TILE, LANE), F32), pltpu.SemaphoreType.DMA((2, TOP_K))],
    )
    return pl.pallas_call(
        functools.partial(_combine_kernel, n_tok=t, final_norm=final_norm),
        grid_spec=grid_spec,
        out_shape=jax.ShapeDtypeStruct((t, d), F32),
        compiler_params=_cparams(("arbitrary",)),
        name="moe_combine",
    )(dest, ys, x2, wts, g2, final_w.reshape(1, d))


def kernel(x, c, w_ada, b_ada, norm_mix_w, norm_ffn_w, w_in, conv_w, conv_b, ssd_dt_bias, ssd_a_log, ssd_d, ssd_norm_w, diff_lambda, diff_subln_w, hgrn_lb_logits, hgrn_norm_w, w_merge, w_out, w_router, router_bias, w_expert_gate, w_expert_up, w_expert_down, final_norm_w):
    b, s, d = x.shape
    t = b * s
    depth = w_in.shape[0]
    mod = ada_modulation(c, w_ada, b_ada)
    lb_p = jax.nn.softmax(hgrn_lb_logits.astype(F32), axis=0)
    lower_bounds = jnp.cumsum(lb_p, axis=0) - lb_p[0]
    w_router_t = w_router.T
    x2 = x.reshape(t, d)
    ys = jnp.zeros(((TOP_K * t + N_EXPERTS * MOE_BM) * ROW_TILE, LANE), F32)
    for l in range(depth):
        sh1, sc1, g1, sh2, sc2, g2 = [mod[l, :, i * d:(i + 1) * d].reshape(b, 1, d) for i in range(6)]
        p2 = in_projection(x2, norm_mix_w[l], sc1, sh1, pad_w_in(w_in[l]), s)
        p3 = p2.reshape(b, s, -1)
        o_a = ssd_branch(p3, conv_w[l], conv_b[l], ssd_dt_bias[l], ssd_a_log[l], ssd_d[l], ssd_norm_w[l])
        o_b = dilated_branch(p3)
        o_c = diff_branch(p3, diff_lambda[l], diff_subln_w[l], l)
        o_d = hgrn_branch(p3, lower_bounds[l], hgrn_norm_w[l])
        flat = lambda o: o.reshape(t, BRANCH_W)
        x_mid, h2, idx, pos, wts, counts = merge_and_route(
            flat(o_a), flat(o_b), flat(o_c), flat(o_d), p2, x2, g1, w_merge[l].astype(BF16), w_out[l].astype(BF16),
            norm_ffn_w[l], sc2, sh2, w_router_t, router_bias, s)
        dest, blk_e, n_used = moe_plan(idx, pos, counts)
        xs = moe_dispatch(h2, dest, ys)
        ys = moe_experts(xs, blk_e, n_used, w_expert_gate, w_expert_up, w_expert_down, l)
        x2 = moe_combine(ys, dest, x_mid, wts, g2, final_norm_w, s, final_norm=(l == depth - 1))
    return x2.reshape(b, s, d)
```

```python
import functools
import math

import jax
import jax.numpy as jnp
from jax import lax
from jax.experimental import pallas as pl
from jax.experimental.pallas import tpu as pltpu

F32 = jnp.float32
BF16 = jnp.bfloat16

D_MODEL = 1024
DEPTH = 2
N_BRANCH = 4
BRANCH_W = 512

SSD_D_INNER = 512
SSD_HEAD_DIM = 64
SSD_HEADS = 8
SSD_GROUPS = 2
SSD_HPG = 4
SSD_STATE = 64
SSD_CONV = 4
SSD_CHUNK = 128
SSD_STEP_CHUNKS = 2
SSD_HALO = 16
SSD_CONV_CH = 768

DIL_PAIRS = ((128, 1), (512, 4), (2048, 16))
DIL_HPG = 4
DIL_HEAD_DIM = 128
DIL_HEADS = 12
DIL_BLOCK = 128

DIFF_HEADS = 4
DIFF_HEAD_DIM = 64

HG_HEADS = 8
HG_KEY_DIM = 64
HG_VAL_DIM = 64
HG_CHUNK = 64

N_EXPERTS = 64
N_EXPERT_GROUPS = 8
EXPERTS_PER_GROUP = 8
TOP_K = 2
D_EXPERT = 256

NORM_EPS = 1e-6
MASK_VALUE = -1e30

LANE = 128
VMEM_LIMIT = 48 * 1024 * 1024

C_GATE = 0
C_Z = 32
C_XBC = 36
C_DT = 42
C_QB = 44
C_KB = 56
C_VB = 68
C_QC = 80
C_KC = 84
C_VC = 88
C_QD = 92
C_FD = 96
C_ID = 100
C_GD = 104
N_COLB = 108
D_IN_PAD = N_COLB * LANE
DT_PAD = 2 * LANE


def _cparams(sem):
    return pltpu.CompilerParams(dimension_semantics=sem, vmem_limit_bytes=VMEM_LIMIT)


def _aligned(x, m):
    return x if isinstance(x, int) else pl.multiple_of(x, m)


def _sigmoid(v):
    return 0.5 * jnp.tanh(0.5 * v) + 0.5


def _silu(v):
    return v * _sigmoid(v)


def _bdot(a, b):
    return jnp.dot(a.astype(BF16), b.astype(BF16), preferred_element_type=F32)


def _bdot_nt(a, b):
    return lax.dot_general(a.astype(BF16), b.astype(BF16), (((1,), (1,)), ((), ())),
                           preferred_element_type=F32)


def _bdot_tn(a, b):
    return lax.dot_general(a.astype(BF16), b.astype(BF16), (((0,), (0,)), ((), ())),
                           preferred_element_type=F32)


def _dot_rhs01(x, m01):
    hi, lo = _split_bf16(x)
    return jnp.dot(hi, m01, preferred_element_type=F32) + jnp.dot(lo, m01, preferred_element_type=F32)


def _exact_rows_dot(m01, v):
    hi = v.astype(BF16)
    r1 = v - hi.astype(F32)
    mid = r1.astype(BF16)
    lo = (r1 - mid.astype(F32)).astype(BF16)
    m = m01.astype(BF16)
    return (jnp.dot(m, hi, preferred_element_type=F32) + jnp.dot(m, mid, preferred_element_type=F32)
            + jnp.dot(m, lo, preferred_element_type=F32))


def _ada_kernel(c_ref, w_ref, b_ref, o_ref):
    o_ref[...] = _bdot(_silu(c_ref[...]), w_ref[...]) + b_ref[...]


def ada_modulation(c, w_ada, b_ada):
    depth, d, n = w_ada.shape
    b = c.shape[0]
    bp = 8
    c_pad = jnp.zeros((bp, d), F32).at[:b].set(c)
    tn = 1536
    out = pl.pallas_call(
        _ada_kernel,
        grid=(depth, n // tn),
        in_specs=[
            pl.BlockSpec((bp, d), lambda l, j: (0, 0)),
            pl.BlockSpec((None, d, tn), lambda l, j: (l, 0, j)),
            pl.BlockSpec((None, 1, tn), lambda l, j: (l, 0, j)),
        ],
        out_specs=pl.BlockSpec((None, bp, tn), lambda l, j: (l, 0, j)),
        out_shape=jax.ShapeDtypeStruct((depth, bp, n), F32),
        compiler_params=_cparams(("arbitrary", "arbitrary")),
        name="ada_mod",
    )(c_pad, w_ada, b_ada.reshape(depth, 1, n))
    return out[:, :b]


INPROJ_TM = 2048
INPROJ_TN = 1536


def _inproj_kernel(x_ref, nw_ref, sc_ref, sh_ref, w_ref, o_ref, h_ref):
    @pl.when(pl.program_id(1) == 0)
    def _():
        x = x_ref[...]
        y = x * lax.rsqrt(jnp.mean(x * x, axis=-1, keepdims=True) + NORM_EPS) * nw_ref[...]
        h_ref[...] = (y * (1.0 + sc_ref[...]) + sh_ref[...]).astype(BF16)

    o_ref[...] = jnp.dot(h_ref[...], w_ref[...], preferred_element_type=F32).astype(o_ref.dtype)


def in_projection(x2, norm_w, scale, shift, w_pad, seq):
    t, d = x2.shape
    n = w_pad.shape[1]
    tm = min(INPROJ_TM, seq)
    tn = INPROJ_TN
    per_b = seq // tm
    return pl.pallas_call(
        _inproj_kernel,
        grid=(t // tm, n // tn),
        in_specs=[
            pl.BlockSpec((tm, d), lambda i, j: (i, 0)),
            pl.BlockSpec((1, d), lambda i, j: (0, 0)),
            pl.BlockSpec((None, 1, d), lambda i, j: (i // per_b, 0, 0)),
            pl.BlockSpec((None, 1, d), lambda i, j: (i // per_b, 0, 0)),
            pl.BlockSpec((d, tn), lambda i, j: (0, j)),
        ],
        out_specs=pl.BlockSpec((tm, tn), lambda i, j: (i, j)),
        out_shape=jax.ShapeDtypeStruct((t, n), BF16),
        scratch_shapes=[pltpu.VMEM((tm, d), BF16)],
        compiler_params=_cparams(("arbitrary", "arbitrary")),
        name="in_proj",
    )(x2, norm_w.reshape(1, d), scale, shift, w_pad)


def pad_w_in(w_in_l):
    d = w_in_l.shape[0]
    o_dt = SSD_D_INNER + SSD_CONV_CH
    o_gate = w_in_l.shape[1] - N_BRANCH * D_MODEL
    return jnp.concatenate(
        [w_in_l[:, o_gate:], w_in_l[:, :o_dt + SSD_HEADS], jnp.zeros((d, DT_PAD - SSD_HEADS), w_in_l.dtype),
         w_in_l[:, o_dt + SSD_HEADS:o_gate]], axis=1).astype(BF16)


def _ssd_kernel(z_ref, xbc_ref, dt_ref, cw_ref, cb_ref, dtb_ref, alog_ref, dsk_ref, nw_ref, exp_ref, o_ref,
                xpad_ref, state_ref, y_ref):
    c = pl.program_id(1)
    L = SSD_CHUNK
    halo = SSD_HALO

    @pl.when(c == 0)
    def _():
        xpad_ref[L:L + halo, :] = jnp.zeros((halo, SSD_CONV_CH), BF16)
        state_ref[...] = jnp.zeros_like(state_ref)

    for ci in range(SSD_STEP_CHUNKS):
        rows = slice(ci * L, (ci + 1) * L)
        _ssd_chunk(z_ref[rows, :], xbc_ref[rows, :], dt_ref[rows, :], cw_ref, cb_ref, dtb_ref, alog_ref, dsk_ref, nw_ref,
                   exp_ref, o_ref, xpad_ref, state_ref, y_ref, rows)


def _ssd_chunk(z_in, xbc_in, dt_in, cw_ref, cb_ref, dtb_ref, alog_ref, dsk_ref, nw_ref, exp_ref, o_ref, xpad_ref, state_ref,
               y_ref, rows):
    L = SSD_CHUNK
    halo = SSD_HALO
    xpad_ref[0:halo, :] = xpad_ref[L:L + halo, :]
    z = z_in.astype(F32)
    xpad_ref[halo:halo + L, :] = xbc_in

    xpad = xpad_ref[...]
    t_i = lax.broadcasted_iota(jnp.int32, (L, halo + L), 0)
    m_i = lax.broadcasted_iota(jnp.int32, (L, halo + L), 1)
    conv = cb_ref[...] + cw_ref[SSD_CONV - 1:SSD_CONV, :] * xbc_in.astype(F32)
    for j in range(SSD_CONV - 1):
        shift = jnp.where(m_i == t_i + (halo - (SSD_CONV - 1) + j), 1.0, 0.0).astype(BF16)
        conv = conv + cw_ref[j:j + 1, :] * jnp.dot(shift, xpad, preferred_element_type=F32)
    xc = _silu(conv)
    xs = xc[:, 0:SSD_D_INNER]
    ns = SSD_GROUPS * SSD_STATE
    bm = xc[:, SSD_D_INNER:SSD_D_INNER + ns]
    cm = xc[:, SSD_D_INNER + ns:SSD_D_INNER + 2 * ns]

    dt = jax.nn.softplus(dt_in[:, 0:LANE].astype(F32) + dtb_ref[...])
    a = -jnp.exp(alog_ref[...])
    adt = dt * a
    row = lax.broadcasted_iota(jnp.int32, (L, L), 0)
    col = lax.broadcasted_iota(jnp.int32, (L, L), 1)
    tril = row >= col
    tril_f = jnp.where(tril, 1.0, 0.0)
    a_cs = _exact_rows_dot(tril_f, adt)
    a_cs_t = a_cs.T

    expand = exp_ref[...]
    dt_x = _dot_rhs01(dt, expand)
    a_cs_x = _exact_rows_dot(tril_f, _dot_rhs01(adt, expand))
    tot_x = a_cs_x[L - 1:L, :]
    xdt = xs * dt_x
    xdt_b = xdt.astype(BF16)
    xdec_b = (xdt * jnp.exp(tot_x - a_cs_x)).astype(BF16)
    dec_out = jnp.exp(a_cs_x)
    dec_tot = jnp.exp(tot_x)

    npair = SSD_HEADS // 2
    grp_of_pair = [(2 * p) // SSD_HPG for p in range(npair)]
    lane = lax.broadcasted_iota(jnp.int32, (L, LANE), 1)
    first = lane < SSD_STATE
    bm_b = bm.astype(BF16)
    cm_b = cm.astype(BF16)
    cbs = [lax.dot_general(jnp.where(first if g == 0 else jnp.logical_not(first), cm, 0.0).astype(BF16), bm_b,
                           (((1,), (1,)), ((), ())), preferred_element_type=F32) for g in range(SSD_GROUPS)]
    lmats = [jnp.exp(jnp.where(tril, jnp.broadcast_to(a_cs[:, e:e + 1], (L, L)) - a_cs_t[e:e + 1, :], MASK_VALUE))
             for e in range(SSD_HEADS)]
    ms = [(cbs[e // SSD_HPG] * lmats[e]).astype(BF16) for e in range(SSD_HEADS)]
    sls = [slice(p * LANE, (p + 1) * LANE) for p in range(npair)]
    diag = [jnp.dot(ms[e], xdt_b[:, sls[e // 2]], preferred_element_type=F32) for e in range(SSD_HEADS)]
    sts = [state_ref[p] for p in range(npair)]
    offs = [jnp.dot(cm_b, sts[p].astype(BF16), preferred_element_type=F32) for p in range(npair)]
    bm_t = bm.T.astype(BF16)
    locs = [jnp.dot(bm_t, xdec_b[:, sls[p]], preferred_element_type=F32) for p in range(npair)]
    sub = lax.broadcasted_iota(jnp.int32, (LANE, LANE), 0)
    for p in range(npair):
        own_rows = (sub < SSD_STATE) if grp_of_pair[p] == 0 else (sub >= SSD_STATE)
        state_ref[p] = dec_tot[:, sls[p]] * sts[p] + jnp.where(own_rows, locs[p], 0.0)
    for p in range(npair):
        y = jnp.where(first, diag[2 * p], diag[2 * p + 1]) + offs[p] * dec_out[:, sls[p]] + dsk_ref[:, sls[p]] * xs[:, sls[p]]
        y_ref[:, sls[p]] = y

    yz = y_ref[...] * _silu(z)
    o_ref[rows, :] = yz * lax.rsqrt(jnp.mean(yz * yz, axis=-1, keepdims=True) + NORM_EPS) * nw_ref[...]


def _pad_lanes(v, n=LANE):
    return jnp.zeros((1, n), F32).at[0, :v.shape[0]].set(v.astype(F32))


def ssd_branch(p3, conv_w, conv_b, dt_bias, a_log, d_skip, norm_w):
    b, s, _ = p3.shape
    L = SSD_CHUNK * SSD_STEP_CHUNKS
    const = lambda shape: pl.BlockSpec(shape, lambda i, c: (0, 0))
    assert SSD_HEAD_DIM == SSD_STATE and LANE == 2 * SSD_HEAD_DIM and SSD_HPG % 2 == 0
    import numpy as np
    expand = (np.arange(LANE)[:, None] == np.arange(SSD_D_INNER)[None, :] // SSD_HEAD_DIM).astype(np.float32)
    return pl.pallas_call(
        _ssd_kernel,
        grid=(b, s // L),
        in_specs=[
            pl.BlockSpec((None, L, SSD_D_INNER), lambda i, c: (i, c, C_Z * LANE // SSD_D_INNER)),
            pl.BlockSpec((None, L, SSD_CONV_CH), lambda i, c: (i, c, C_XBC * LANE // SSD_CONV_CH)),
            pl.BlockSpec((None, L, DT_PAD), lambda i, c: (i, c, C_DT * LANE // DT_PAD)),
            const((SSD_CONV, SSD_CONV_CH)),
            const((1, SSD_CONV_CH)),
            const((1, LANE)),
            const((1, LANE)),
            const((1, SSD_D_INNER)),
            const((1, SSD_D_INNER)),
            const((LANE, SSD_D_INNER)),
        ],
        out_specs=pl.BlockSpec((None, L, SSD_D_INNER), lambda i, c: (i, c, 0)),
        out_shape=jax.ShapeDtypeStruct((b, s, SSD_D_INNER), F32),
        scratch_shapes=[
            pltpu.VMEM((SSD_CHUNK + SSD_HALO, SSD_CONV_CH), BF16),
            pltpu.VMEM((SSD_HEADS // 2, LANE, LANE), F32),
            pltpu.VMEM((SSD_CHUNK, SSD_D_INNER), F32),
        ],
        compiler_params=_cparams(("arbitrary", "arbitrary")),
        name="ssd",
    )(p3, p3, p3, conv_w, conv_b.reshape(1, -1), _pad_lanes(dt_bias), _pad_lanes(a_log),
      jnp.repeat(d_skip.astype(F32), SSD_HEAD_DIM).reshape(1, -1), norm_w.reshape(1, -1), jnp.asarray(expand, BF16))


DIL_ORDER = (2, 1, 0)
DIL_CHEAP_STRIDE = 4
DIL_UNROLL = 8


def _dil_kernel(q_ref, k_ref, v_ref, o_ref, m_ref, l_ref, tmp_ref, tmp2_ref, qd_ref, kd_ref, vd_ref):
    g = pl.program_id(2)
    s_len = q_ref.shape[0]
    blk = DIL_BLOCK
    scale = DIL_HEAD_DIM ** -0.5

    def run_group(dil, first_group):
        sub = s_len // dil
        nblk = sub // blk

        if dil == 1:
            qd_ref[...] = (q_ref[...].astype(F32) * scale).astype(BF16)
            kd, vd = k_ref, v_ref
        else:
            for src, dst, mul in ((q_ref, qd_ref, scale), (k_ref, kd_ref, None), (v_ref, vd_ref, None)):
                x = src[...].astype(F32)
                tmp_ref[...] = x if mul is None else x * mul

                if dil <= DIL_CHEAP_STRIDE:
                    for r in range(dil):
                        dst[r * sub:(r + 1) * sub, :] = tmp_ref[pl.ds(r, sub, stride=dil), :].astype(BF16)
                else:
                    inner = DIL_CHEAP_STRIDE
                    outer = dil // inner
                    part = s_len // inner
                    for r_lo in range(inner):
                        tmp2_ref[r_lo * part:(r_lo + 1) * part, :] = tmp_ref[pl.ds(r_lo, part, stride=inner), :]
                    for r_lo in range(inner):
                        for r_hi in range(outer):
                            r = r_lo + inner * r_hi
                            dst[r * sub:(r + 1) * sub, :] = tmp2_ref[pl.ds(r_lo * part + r_hi, sub, stride=outer), :].astype(BF16)
            kd, vd = kd_ref, vd_ref

        def block_softmax(r, b, nkey):
            row0 = _aligned(r * sub + b * blk, blk)
            ks = pl.ds(_aligned(row0 - (nkey - blk), blk), nkey)
            i = lax.broadcasted_iota(jnp.int32, (blk, nkey), 0) + (nkey - blk)
            j = lax.broadcasted_iota(jnp.int32, (blk, nkey), 1)
            ok = jnp.logical_and(j <= i, j >= i - blk)
            sc = lax.dot_general(qd_ref[pl.ds(row0, blk), :], kd[ks, :], (((1,), (1,)), ((), ())),
                                 preferred_element_type=F32)
            sc = jnp.where(ok, sc, MASK_VALUE)
            m_blk = jnp.max(sc, axis=-1, keepdims=True)
            p = jnp.exp(sc - m_blk).astype(BF16)
            v_ext = jnp.concatenate([vd[ks, :], jnp.ones((nkey, LANE), BF16)], axis=1)
            return m_blk, jnp.dot(p, v_ext, preferred_element_type=F32)

        def merge(r, b, m_blk, pv):
            base = r + dil * blk * b
            acc_rows = pl.ds(_aligned(base, blk), blk) if dil == 1 else pl.ds(base, blk, stride=dil)
            m_blk = jnp.broadcast_to(m_blk, (blk, LANE))
            if first_group:
                o_ref[acc_rows, :] = pv[:, :LANE]
                l_ref[acc_rows, :] = pv[:, LANE:]
                m_ref[acc_rows, :] = m_blk
                return
            m_old = m_ref[acc_rows, :]
            m_new = jnp.maximum(m_old, m_blk)
            a_old = jnp.exp(m_old - m_new)
            a_blk = jnp.exp(m_blk - m_new)
            o_ref[acc_rows, :] = a_old * o_ref[acc_rows, :] + a_blk * pv[:, :LANE]
            l_ref[acc_rows, :] = a_old * l_ref[acc_rows, :] + a_blk * pv[:, LANE:]
            m_ref[acc_rows, :] = m_new

        def run_units(n, where, nkey):
            def group(units):
                parts = [block_softmax(r, b, nkey) for r, b in units]
                for (r, b), (m_blk, pv) in zip(units, parts):
                    merge(r, b, m_blk, pv)

            def body(i, carry):
                group([where(i * DIL_UNROLL + j) for j in range(DIL_UNROLL)])
                return carry

            lax.fori_loop(0, n // DIL_UNROLL, body, 0)
            if n % DIL_UNROLL:
                group([where(n - n % DIL_UNROLL + j) for j in range(n % DIL_UNROLL)])

        run_units(dil, lambda u: (u, 0), blk)
        if nblk > 1:
            run_units(dil * (nblk - 1), lambda u: (u // (nblk - 1), 1 + u % (nblk - 1)), 2 * blk)

    for step, gi in enumerate(DIL_ORDER):
        window, dil = DIL_PAIRS[gi]
        assert window // dil == blk

        @pl.when(g == step)
        def _(dil=dil, step=step):
            run_group(dil, step == 0)

    @pl.when(g == len(DIL_PAIRS) - 1)
    def _():
        o_ref[...] = o_ref[...] / l_ref[...]


def dilated_branch(p3):
    b, s, _ = p3.shape
    ng = len(DIL_PAIRS)
    last = len(DIL_PAIRS) - 1
    assert DIL_ORDER == tuple(range(last, -1, -1))
    spec = lambda c0: pl.BlockSpec((None, s, DIL_HEAD_DIM), lambda i, j, g: (i, 0, c0 + (last - g) * DIL_HPG + j))
    return pl.pallas_call(
        _dil_kernel,
        grid=(b, DIL_HPG, ng),
        in_specs=[spec(C_QB), spec(C_KB), spec(C_VB)],
        out_specs=pl.BlockSpec((None, s, DIL_HEAD_DIM), lambda i, j, g: (i, 0, j)),
        out_shape=jax.ShapeDtypeStruct((b, s, BRANCH_W), F32),
        scratch_shapes=[pltpu.VMEM((s, LANE), F32), pltpu.VMEM((s, LANE), F32), pltpu.VMEM((s, LANE), F32),
                        pltpu.VMEM((s, LANE), F32), pltpu.VMEM((s, LANE), BF16), pltpu.VMEM((s, LANE), BF16), pltpu.VMEM((s, LANE), BF16)],
        compiler_params=_cparams(("arbitrary", "arbitrary", "arbitrary")),
        name="dilated",
    )(p3, p3, p3)


DIFF_TQ = 512
DIFF_TK = 1024
DIFF_DIAG = 512
LOG2E = 1.4426950408889634


DIFF_ONES = 16


def _diff_kernel(lam_ref, q_ref, k_ref, v_ref, nw_ref, o_ref, vt_ref, m_ref, acc_ref, *, lam_init):
    qi = pl.program_id(2)
    tq, tk, dh = DIFF_TQ, DIFF_TK, DIFF_HEAD_DIM
    w = 2 * dh
    s_len = k_ref.shape[0]

    @pl.when(qi == 0)
    def _():
        for c0 in range(0, s_len, tk):
            vt_ref[0:w, c0:c0 + tk] = v_ref[c0:c0 + tk, :].astype(F32).T.astype(BF16)
        vt_ref[w:w + DIFF_ONES, :] = jnp.ones((DIFF_ONES, s_len), BF16)

    q_t = (q_ref[...].astype(F32) * (dh ** -0.5 * LOG2E)).T
    half = lax.broadcasted_iota(jnp.int32, (w, tq), 0) < dh
    q_sel = [jnp.where(half, q_t, 0.0).astype(BF16), jnp.where(half, 0.0, q_t).astype(BF16)]
    for t in range(2):
        m_ref[t] = jnp.full((8, tq), MASK_VALUE, F32)
        acc_ref[t] = jnp.zeros((w + DIFF_ONES, tq), F32)

    def block(kstart, width, qlo=None):
        c0 = qlo or 0
        nq = tq - c0
        kb = k_ref[pl.ds(kstart, width), :]
        vb = vt_ref[:, pl.ds(kstart, width)]
        scores = [jnp.dot(kb, q_sel[t][:, c0:], preferred_element_type=F32) for t in range(2)]
        if qlo is not None:
            row = lax.broadcasted_iota(jnp.int32, (width, nq), 0)
            col = lax.broadcasted_iota(jnp.int32, (width, nq), 1)
            scores = [jnp.where(row <= col, sc, MASK_VALUE) for sc in scores]
        m_blks = [jnp.max(jnp.max(sc.reshape(width // 8, 8, nq), axis=0), axis=0, keepdims=True) for sc in scores]
        pvs = [jnp.dot(vb, jnp.exp2(sc - m_blk).astype(BF16), preferred_element_type=F32)
               for sc, m_blk in zip(scores, m_blks)]
        for t, (m_blk, pv) in enumerate(zip(m_blks, pvs)):
            m_old = m_ref[t, 0:1, c0:]
            m_new = jnp.maximum(m_old, m_blk)
            acc_ref[t, :, c0:] = acc_ref[t, :, c0:] * jnp.exp2(m_old - m_new) + pv * jnp.exp2(m_blk - m_new)
            m_ref[t, :, c0:] = jnp.broadcast_to(m_new, (8, nq))

    def main_step(j, carry):
        block(pl.multiple_of(j * tk, tk), tk)
        return carry

    per = tk // tq
    lax.fori_loop(0, qi // per, main_step, 0)
    for r in range(1, per):
        @pl.when(qi % per >= r)
        def _(r=r):
            block(pl.multiple_of((qi // per) * tk + (r - 1) * tq, tq), tq)
    for c0 in range(0, tq, DIFF_DIAG):
        block(pl.multiple_of(qi * tq + c0, DIFF_DIAG), DIFF_DIAG, qlo=c0)

    lam_p = lam_ref[...]
    lam = (jnp.exp(jnp.sum(lam_p[0:1] * lam_p[1:2], axis=-1, keepdims=True))
           - jnp.exp(jnp.sum(lam_p[2:3] * lam_p[3:4], axis=-1, keepdims=True)) + lam_init)
    a0, a1 = acc_ref[0], acc_ref[1]
    o_t = a0[0:w] / a0[w:w + 1] - lam * (a1[0:w] / a1[w:w + 1])
    o = o_t.T
    o = o * lax.rsqrt(jnp.mean(o * o, axis=-1, keepdims=True) + NORM_EPS) * nw_ref[...]
    o_ref[...] = o * (1.0 - lam_init)


def diff_branch(p3, diff_lambda, subln_w, layer):
    b, s, _ = p3.shape
    lam_init = 0.8 - 0.6 * math.exp(-0.3 * layer)
    w = 2 * DIFF_HEAD_DIM
    return pl.pallas_call(
        functools.partial(_diff_kernel, lam_init=lam_init),
        grid=(b, DIFF_HEADS, s // DIFF_TQ),
        in_specs=[
            pl.BlockSpec((4, DIFF_HEAD_DIM), lambda i, h, t: (0, 0)),
            pl.BlockSpec((None, DIFF_TQ, w), lambda i, h, t: (i, t, C_QC + h)),
            pl.BlockSpec((None, s, w), lambda i, h, t: (i, 0, C_KC + h)),
            pl.BlockSpec((None, s, w), lambda i, h, t: (i, 0, C_VC + h)),
            pl.BlockSpec((1, w), lambda i, h, t: (0, 0)),
        ],
        out_specs=pl.BlockSpec((None, DIFF_TQ, w), lambda i, h, t: (i, t, h)),
        out_shape=jax.ShapeDtypeStruct((b, s, BRANCH_W), F32),
        scratch_shapes=[pltpu.VMEM((w + DIFF_ONES, s), BF16), pltpu.VMEM((2, 8, DIFF_TQ), F32),
                        pltpu.VMEM((2, w + DIFF_ONES, DIFF_TQ), F32)],
        compiler_params=_cparams(("arbitrary", "arbitrary", "arbitrary")),
        name="diff_attn",
    )(diff_lambda, p3, p3, p3, subln_w.reshape(1, w))


HG_LEVELS = 6
HG_STEP_CHUNKS = 4


def _hgrn_tables():
    import numpy as np
    c = HG_CHUNK
    i = np.arange(c)[:, None]
    j = np.arange(c)[None, :]
    tril = (j <= i).astype(np.float32)
    masks = []
    for lv in range(HG_LEVELS):
        sz = 1 << lv
        m = ((i // (2 * sz) == j // (2 * sz)) & (i // sz == j // sz + 1)).astype(np.float32)
        masks.append(np.concatenate([m, m], axis=0))
    lane = np.arange(LANE)
    same_head = (lane[:, None] // HG_KEY_DIM == lane[None, :] // HG_KEY_DIM).astype(np.float32)
    return tril, np.stack(masks, axis=0), same_head


def _hgrn_kernel(q_ref, f_ref, i_ref, g_ref, lb_ref, tri_ref, msk_ref, sh_ref, nw_ref, o_ref, state_ref, gpad_ref):
    @pl.when(pl.program_id(1) == 0)
    def _():
        state_ref[...] = jnp.zeros_like(state_ref)

    for ci in range(HG_STEP_CHUNKS):
        rows = slice(ci * HG_CHUNK, (ci + 1) * HG_CHUNK)
        _hgrn_chunk(q_ref[rows, :], f_ref[rows, :], i_ref[rows, :], g_ref, lb_ref, tri_ref, msk_ref, sh_ref, nw_ref,
                    o_ref, state_ref, gpad_ref, rows)


def _hgrn_chunk(q_in, f_in, i_in, g_ref, lb_ref, tri_ref, msk_ref, sh_ref, nw_ref, o_ref, state_ref, gpad_ref, rows):
    cs = HG_CHUNK
    nl = HG_LEVELS
    w = q_in.shape[1]
    sub_rows = 8

    lb = lb_ref[...]
    f_gate = lb + (1.0 - lb) * jax.nn.sigmoid(f_in.astype(F32))
    log_f = jnp.log(f_gate)
    k_in = 1.0 - f_gate
    q = _silu(q_in.astype(F32))
    v = i_in.astype(F32)

    hi, lo = _split_bf16(log_f)
    tri = tri_ref[...]
    g = jnp.dot(tri, hi, preferred_element_type=F32) + jnp.dot(tri, lo, preferred_element_type=F32)
    gpad_ref[0:sub_rows, :] = jnp.zeros((sub_rows, w), F32)
    gpad_ref[sub_rows:sub_rows + cs, :] = g
    g_last = g[cs - 1:cs, :]

    def g_row(r):
        return jnp.broadcast_to(gpad_ref[sub_rows + r:sub_rows + r + 1, :], (sub_rows, w))

    sub = lax.broadcasted_iota(jnp.int32, (sub_rows, w), 0)
    tiles = [[] for _ in range(nl)]
    for t in range(cs // sub_rows):
        r0 = t * sub_rows
        gt = g[r0:r0 + sub_rows]
        prev = gpad_ref[r0 + sub_rows - 1:r0 + 2 * sub_rows - 1, :]
        tiles[0].append(jnp.where(sub % 2 == 1, gt - prev, 0.0))
        c1 = jnp.where(sub < 4, g_row(r0 + 1), g_row(r0 + 5))
        tiles[1].append(jnp.where((sub // 2) % 2 == 1, gt - c1, c1 - gt))
        c2 = g_row(r0 + 3)
        tiles[2].append(jnp.where(sub >= 4, gt - c2, c2 - gt))
        for lv in range(3, nl):
            span = (1 << lv) // sub_rows
            mid = (t // (2 * span)) * 2 * span + span
            cm = g_row(mid * sub_rows - 1)
            tiles[lv].append(gt - cm if (t // span) % 2 == 1 else cm - gt)
    dec = [jnp.exp(jnp.concatenate(tl, axis=0)) for tl in tiles]
    first_w = lax.broadcasted_iota(jnp.int32, (cs, w), 1) % LANE < HG_KEY_DIM
    q_even = jnp.where(first_w, q, 0.0).astype(BF16)
    q_odd = jnp.where(first_w, 0.0, q).astype(BF16)
    k_b = k_in.astype(BF16)
    dec_b = [d.astype(BF16) for d in dec]
    q_full = q * jnp.exp(g)
    k_full = k_in * jnp.exp(g_last - g)
    decay_row = jnp.exp(g_last)

    same_head = sh_ref[...]
    same_head_b = same_head.astype(BF16)
    first = lax.broadcasted_iota(jnp.int32, (cs, LANE), 1) < HG_KEY_DIM
    pairs = range(w // LANE)
    sls = [slice(p * LANE, (p + 1) * LANE) for p in pairs]
    attns = [jnp.zeros((2 * cs, cs), F32) for _ in pairs]
    for lv in range(nl):
        for p in pairs:
            db = dec_b[lv][:, sls[p]]
            lhs = jnp.concatenate([q_even[:, sls[p]] * db, q_odd[:, sls[p]] * db], axis=0)
            scores = lax.dot_general(lhs, k_b[:, sls[p]] * db, (((1,), (1,)), ((), ())), preferred_element_type=F32)
            attns[p] = attns[p] + msk_ref[lv] * scores
    vps = [v[:, sl] for sl in sls]
    vpbs = [vp.astype(BF16) for vp in vps]
    rs = [jnp.dot(attns[p].astype(BF16), vpbs[p], preferred_element_type=F32) for p in pairs]
    diags = [jnp.dot((q[:, sl] * k_in[:, sl]).astype(BF16), same_head_b, preferred_element_type=F32) for sl in sls]
    sts = [state_ref[p] for p in pairs]
    inters = [_bdot_nt(q_full[:, sls[p]], sts[p]) for p in pairs]
    upds = [_bdot_tn(vpbs[p], k_full[:, sls[p]]) for p in pairs]
    for p in pairs:
        state_ref[p] = decay_row[:, sls[p]] * sts[p] + same_head * upds[p]
    outs = [jnp.where(first, rs[p][0:cs], rs[p][cs:2 * cs]) + diags[p] * vps[p] + inters[p] for p in pairs]
    mss = [jnp.dot((o * o).astype(BF16), same_head_b, preferred_element_type=F32) * (1.0 / HG_VAL_DIM) for o in outs]
    for p in pairs:
        o = outs[p] * lax.rsqrt(mss[p] + NORM_EPS) * nw_ref[...]
        o_ref[rows, sls[p]] = o * _silu(g_ref[rows, sls[p]].astype(F32))


def hgrn_branch(p3, lower_bound, norm_w):
    b, s, _ = p3.shape
    cs = HG_CHUNK * HG_STEP_CHUNKS
    w = HG_HEADS * HG_KEY_DIM
    assert HG_KEY_DIM == HG_VAL_DIM and LANE == 2 * HG_KEY_DIM
    tril, masks, same_head = _hgrn_tables()
    seg = lambda c0: pl.BlockSpec((None, cs, w), lambda i, c: (i, c, c0 * LANE // w))
    return pl.pallas_call(
        _hgrn_kernel,
        grid=(b, s // cs),
        in_specs=[
            seg(C_QD), seg(C_FD), seg(C_ID), seg(C_GD),
            pl.BlockSpec((1, w), lambda i, c: (0, 0)),
            pl.BlockSpec(tril.shape, lambda i, c: (0, 0)),
            pl.BlockSpec(masks.shape, lambda i, c: (0, 0, 0)),
            pl.BlockSpec(same_head.shape, lambda i, c: (0, 0)),
            pl.BlockSpec((1, LANE), lambda i, c: (0, 0)),
        ],
        out_specs=pl.BlockSpec((None, cs, w), lambda i, c: (i, c, 0)),
        out_shape=jax.ShapeDtypeStruct((b, s, BRANCH_W), F32),
        scratch_shapes=[pltpu.VMEM((w // LANE, LANE, LANE), F32), pltpu.VMEM((HG_CHUNK + 8, w), F32)],
        compiler_params=_cparams(("arbitrary", "arbitrary")),
        name="hgrn2",
    )(p3, p3, p3, p3, lower_bound.reshape(1, w).astype(F32), jnp.asarray(tril, BF16), jnp.asarray(masks, F32),
      jnp.asarray(same_head, F32), jnp.tile(norm_w.astype(F32), LANE // HG_VAL_DIM).reshape(1, LANE))


MERGE_TM = 512


def _split_bf16(v):
    hi = v.astype(BF16)
    return hi, (v - hi.astype(F32)).astype(BF16)


def _first_index_of_max(vals, iota, n):
    top = jnp.max(vals, axis=0, keepdims=True)
    idx = jnp.min(jnp.where(vals == top, iota, n), axis=0, keepdims=True)
    return top, idx


def _merge_kernel(oa_ref, ob_ref, oc_ref, od_ref, gl_ref, x_ref, g1_ref, wm_ref, wo_ref,
                  nw_ref, sc_ref, sh_ref, wr_ref, rb_ref,
                  xo_ref, h2_ref, idx_ref, pos_ref, wt_ref, cnt_ref, carry_ref):
    step = pl.program_id(0)
    tm = x_ref.shape[0]
    d = D_MODEL

    projs = [jnp.dot(o_ref[...].astype(BF16), wm_ref[n], preferred_element_type=F32)
             for n, o_ref in enumerate((oa_ref, ob_ref, oc_ref, od_ref))]
    acc = jnp.zeros((tm, d), F32)
    for n, proj in enumerate(projs):
        acc = acc + _sigmoid(gl_ref[:, n * d:(n + 1) * d].astype(F32)) * proj
    mix = jnp.dot(acc.astype(BF16), wo_ref[...], preferred_element_type=F32)
    x_new = x_ref[...] + g1_ref[...] * mix
    xo_ref[...] = x_new

    y = x_new * lax.rsqrt(jnp.mean(x_new * x_new, axis=-1, keepdims=True) + NORM_EPS) * nw_ref[...]
    h2 = y * (1.0 + sc_ref[...]) + sh_ref[...]
    _to_row_tiles(h2_ref, h2)

    h_hi, h_lo = _split_bf16(h2)
    w_hi, w_lo = _split_bf16(wr_ref[...])
    nt = lambda a, b: lax.dot_general(a, b, (((1,), (1,)), ((), ())), preferred_element_type=F32)
    logits = nt(w_hi, h_hi) + nt(w_hi, h_lo) + nt(w_lo, h_hi)
    scores = jax.nn.sigmoid(logits)
    sel = scores + rb_ref[...]

    ne, ng, pg = N_EXPERTS, N_EXPERT_GROUPS, EXPERTS_PER_GROUP
    iota_g = lax.broadcasted_iota(jnp.int32, (pg, tm), 0)
    best_score = None
    best_group = None
    for g in range(ng):
        xg = sel[g * pg:(g + 1) * pg, :]
        top1, i1 = _first_index_of_max(xg, iota_g, pg)
        top2 = jnp.max(jnp.where(iota_g == i1, -jnp.inf, xg), axis=0, keepdims=True)
        gs = top1 + top2
        if g == 0:
            best_score, best_group = gs, jnp.zeros((1, tm), jnp.int32)
        else:
            better = gs > best_score
            best_score = jnp.where(better, gs, best_score)
            best_group = jnp.where(better, g, best_group)

    iota_e = lax.broadcasted_iota(jnp.int32, (ne, tm), 0)
    masked = jnp.where(iota_e // pg == best_group, sel, MASK_VALUE)
    _, e1 = _first_index_of_max(masked, iota_e, ne)
    oh1 = iota_e == e1
    _, e2 = _first_index_of_max(jnp.where(oh1, MASK_VALUE, masked), iota_e, ne)
    oh2 = iota_e == e2
    w1 = jnp.sum(jnp.where(oh1, scores, 0.0), axis=0, keepdims=True)
    w2 = jnp.sum(jnp.where(oh2, scores, 0.0), axis=0, keepdims=True)
    wsum = w1 + w2
    w1 = w1 / wsum
    w2 = w2 / wsum

    @pl.when(step == 0)
    def _():
        carry_ref[...] = jnp.zeros_like(carry_ref)

    f1 = jnp.where(oh1, 1.0, 0.0)
    f2 = jnp.where(oh2, 1.0, 0.0)
    both = f1 + f2
    r_i = lax.broadcasted_iota(jnp.int32, (tm, tm), 0)
    c_i = lax.broadcasted_iota(jnp.int32, (tm, tm), 1)
    before = jnp.where(r_i < c_i, 1.0, 0.0).astype(BF16)
    rank = jnp.dot(both.astype(BF16), before, preferred_element_type=F32) + carry_ref[:, 0:1]
    p1 = jnp.sum(f1 * rank, axis=0, keepdims=True)
    p2 = jnp.sum(f2 * rank, axis=0, keepdims=True)
    carry_new = carry_ref[...] + jnp.sum(both, axis=1, keepdims=True)
    carry_ref[...] = carry_new
    cnt_ref[...] = carry_new.astype(jnp.int32)

    idx_ref[...] = jnp.concatenate([e1, e2], axis=0)
    pos_ref[...] = jnp.concatenate([p1, p2], axis=0).astype(jnp.int32)
    wt_rows = jnp.concatenate([w1, w2, jnp.zeros((LANE - 2, tm), F32)], axis=0)
    wt_ref[...] = wt_rows.T


def merge_and_route(o_a, o_b, o_c, o_d, p2, x2, g1, w_merge_bf, w_out_bf, norm_w, sc2, sh2, w_router_t, router_bias, seq):
    t, d = x2.shape
    tm = min(MERGE_TM, seq)
    per_b = seq // tm
    gw = N_BRANCH * d
    tok = lambda w: pl.BlockSpec((tm, w), lambda i: (i, 0))
    bat = pl.BlockSpec((None, 1, d), lambda i: (i // per_b, 0, 0))
    ne = N_EXPERTS
    return pl.pallas_call(
        _merge_kernel,
        grid=(t // tm,),
        in_specs=[
            tok(BRANCH_W), tok(BRANCH_W), tok(BRANCH_W), tok(BRANCH_W),
            pl.BlockSpec((tm, gw), lambda i: (i, C_GATE * LANE // gw)),
            tok(d), bat,
            pl.BlockSpec((N_BRANCH, BRANCH_W, d), lambda i: (0, 0, 0)),
            pl.BlockSpec((d, d), lambda i: (0, 0)),
            pl.BlockSpec((1, d), lambda i: (0, 0)),
            bat, bat,
            pl.BlockSpec((ne, d), lambda i: (0, 0)),
            pl.BlockSpec((ne, 1), lambda i: (0, 0)),
        ],
        out_specs=[
            tok(d), pl.BlockSpec((tm * ROW_TILE, LANE), lambda i: (i, 0)),
            pl.BlockSpec((TOP_K, tm), lambda i: (0, i)),
            pl.BlockSpec((TOP_K, tm), lambda i: (0, i)),
            tok(LANE),
            pl.BlockSpec((ne, LANE), lambda i: (0, 0)),
        ],
        out_shape=[
            jax.ShapeDtypeStruct((t, d), F32),
            jax.ShapeDtypeStruct((t * ROW_TILE, LANE), F32),
            jax.ShapeDtypeStruct((TOP_K, t), jnp.int32),
            jax.ShapeDtypeStruct((TOP_K, t), jnp.int32),
            jax.ShapeDtypeStruct((t, LANE), F32),
            jax.ShapeDtypeStruct((ne, LANE), jnp.int32),
        ],
        scratch_shapes=[pltpu.VMEM((ne, LANE), F32)],
        compiler_params=_cparams(("arbitrary",)),
        name="merge_route",
    )(o_a, o_b, o_c, o_d, p2, x2, g1, w_merge_bf, w_out_bf, norm_w.reshape(1, d), sc2, sh2,
      w_router_t, router_bias.reshape(ne, 1))


MOE_BM = 256


def _plan_kernel(idx_ref, pos_ref, cnt_ref, dest_ref, blke_ref, nused_ref):
    bm = MOE_BM
    ne = N_EXPERTS
    cnt = cnt_ref[...].astype(F32)
    padded = jnp.floor((cnt + (bm - 1)) * (1.0 / bm)) * bm
    r = lax.broadcasted_iota(jnp.int32, (ne, ne), 0)
    c = lax.broadcasted_iota(jnp.int32, (ne, ne), 1)
    pstart = _exact_rows_dot(jnp.where(c < r, 1.0, 0.0), padded)
    pend = pstart + padded

    idx = idx_ref[...]
    base = jnp.zeros(idx.shape, F32)
    for e in range(ne):
        base = jnp.where(idx == e, pstart[e:e + 1, 0:1], base)
    dest_ref[...] = base.astype(jnp.int32) + pos_ref[...]

    nbp = blke_ref.shape[1]
    blk_start = (lax.broadcasted_iota(jnp.int32, (ne, nbp), 1) * bm).astype(F32)
    done = jnp.sum(jnp.where(pend[:, 0:1] <= blk_start, 1.0, 0.0), axis=0, keepdims=True)
    blke_ref[...] = jnp.minimum(done, ne - 1.0).astype(jnp.int32)
    nused_ref[...] = (pend[ne - 1:ne, :] * (1.0 / bm)).astype(jnp.int32)


def moe_plan(idx, pos, counts):
    n_tok = idx.shape[1]
    n_rows = TOP_K * n_tok + N_EXPERTS * MOE_BM
    n_blocks = n_rows // MOE_BM
    nbp = -(-n_blocks // LANE) * LANE
    dest, blk_e, n_used = pl.pallas_call(
        _plan_kernel,
        out_shape=[
            jax.ShapeDtypeStruct((TOP_K, n_tok), jnp.int32),
            jax.ShapeDtypeStruct((1, nbp), jnp.int32),
            jax.ShapeDtypeStruct((1, LANE), jnp.int32),
        ],
        compiler_params=pltpu.CompilerParams(vmem_limit_bytes=VMEM_LIMIT),
        name="moe_plan",
    )(idx, pos, counts)
    return dest.reshape(-1), blk_e[0, :n_blocks], n_used[0, :1]


ROW_TILE = 8
DISPATCH_TM = 512


def _to_row_tiles(ref, x):
    rows = x.shape[0]
    for j in range(ROW_TILE):
        ref[pl.ds(j, rows, stride=ROW_TILE), :] = x[:, j * LANE:(j + 1) * LANE]


def _from_row_tiles(ref, rows):
    return jnp.concatenate([ref[pl.ds(j, rows, stride=ROW_TILE), :] for j in range(ROW_TILE)], axis=1)


def _tile_rows(row):
    return pl.ds(_aligned(row * ROW_TILE, ROW_TILE), ROW_TILE)


def _dispatch_kernel(dest_ref, h_ref, xs_init_hbm, xs_hbm, sem_ref, *, n_tok):
    del xs_init_hbm
    i = pl.program_id(0)
    tm = h_ref.shape[0] // ROW_TILE

    def copy(r, k, row):
        return pltpu.make_async_copy(h_ref.at[pl.ds(r * ROW_TILE, ROW_TILE), :], xs_hbm.at[_tile_rows(row), :], sem_ref.at[k])

    for r in range(tm):
        for k in range(TOP_K):
            copy(r, k, dest_ref[k * n_tok + i * tm + r]).start(priority=k)
    for k in range(TOP_K):
        for r in range(tm):
            copy(r, k, 0).wait()


def moe_dispatch(h2_tiles, dest, xs_init):
    n_tok = dest.shape[0] // TOP_K
    tm = min(DISPATCH_TM, n_tok)
    grid_spec = pltpu.PrefetchScalarGridSpec(
        num_scalar_prefetch=1,
        grid=(n_tok // tm,),
        in_specs=[pl.BlockSpec((tm * ROW_TILE, LANE), lambda i, ds: (i, 0)), pl.BlockSpec(memory_space=pl.ANY)],
        out_specs=pl.BlockSpec(memory_space=pl.ANY),
        scratch_shapes=[pltpu.SemaphoreType.DMA((TOP_K,))],
    )
    return pl.pallas_call(
        functools.partial(_dispatch_kernel, n_tok=n_tok),
        grid_spec=grid_spec,
        out_shape=jax.ShapeDtypeStruct(xs_init.shape, F32),
        input_output_aliases={2: 0},
        compiler_params=_cparams(("arbitrary",)),
        name="moe_dispatch",
    )(dest, h2_tiles, xs_init)


def _expert_kernel(blke_ref, nused_ref, x_ref, wg_ref, wu_ref, wd_ref, y_ref, wgb_ref, wub_ref, wdb_ref):
    b = pl.program_id(0)
    bm = MOE_BM
    used = nused_ref[0]
    changed = jnp.logical_or(b == 0, blke_ref[b] != blke_ref[jnp.maximum(b - 1, 0)])

    @pl.when(jnp.logical_and(b < used, changed))
    def _():
        wgb_ref[...] = wg_ref[...].astype(BF16)
        wub_ref[...] = wu_ref[...].astype(BF16)
        wdb_ref[...] = wd_ref[...].astype(BF16)

    @pl.when(b < used)
    def _():
        xb = _from_row_tiles(x_ref, bm).astype(BF16)
        hid = _silu(jnp.dot(xb, wgb_ref[...], preferred_element_type=F32)) * jnp.dot(xb, wub_ref[...], preferred_element_type=F32)
        _to_row_tiles(y_ref, jnp.dot(hid.astype(BF16), wdb_ref[...], preferred_element_type=F32))

    @pl.when(b >= used)
    def _():
        y_ref[...] = jnp.zeros_like(y_ref)


def moe_experts(xs, blk_e, n_used, w_gate, w_up, w_down, layer):
    bm = MOE_BM
    n_rows = xs.shape[0] // ROW_TILE
    d, de = w_gate.shape[-2:]
    assert d == ROW_TILE * LANE
    wspec = lambda r, c: pl.BlockSpec((None, None, r, c), lambda b, be, nu: (layer, be[b], 0, 0))
    rows = pl.BlockSpec((bm * ROW_TILE, LANE), lambda b, be, nu: (b, 0))
    rows_in = pl.BlockSpec((bm * ROW_TILE, LANE), lambda b, be, nu: (jnp.minimum(b, nu[0]), 0))
    grid_spec = pltpu.PrefetchScalarGridSpec(
        num_scalar_prefetch=2,
        grid=(n_rows // bm,),
        in_specs=[rows_in, wspec(d, de), wspec(d, de), wspec(de, d)],
        out_specs=rows,
        scratch_shapes=[pltpu.VMEM((d, de), BF16), pltpu.VMEM((d, de), BF16), pltpu.VMEM((de, d), BF16)],
    )
    return pl.pallas_call(
        _expert_kernel,
        grid_spec=grid_spec,
        out_shape=jax.ShapeDtypeStruct(xs.shape, F32),
        compiler_params=_cparams(("arbitrary",)),
        name="moe_experts",
    )(blk_e, n_used, xs, w_gate, w_up, w_down)


COMB_TM = 256


def _tile_copy(src_hbm, row, dst, r, sem):
    return pltpu.make_async_copy(src_hbm.at[_tile_rows(row), :], dst.at[pl.ds(r * ROW_TILE, ROW_TILE), :], sem)


def _gather_start(src_hbm, row_of, dst, sem, n):
    for r in range(n):
        _tile_copy(src_hbm, row_of(r), dst, r, sem).start(priority=r % 2)


def _gather_wait(src_hbm, dst, sem, n):
    for r in range(n):
        _tile_copy(src_hbm, 0, dst, r, sem).wait()


def _combine_kernel(dest_ref, ys_hbm, x_ref, wt_ref, g2_ref, fw_ref, o_ref, buf_ref, sem_ref, *, n_tok, final_norm):
    i = pl.program_id(0)
    n = pl.num_programs(0)
    tm = x_ref.shape[0]
    slot = i % 2

    def start(tile, sl):
        for k in range(TOP_K):
            _gather_start(ys_hbm, lambda r: dest_ref[k * n_tok + tile * tm + r], buf_ref.at[sl, k], sem_ref.at[sl, k], tm)

    def wait(sl):
        for k in range(TOP_K):
            _gather_wait(ys_hbm, buf_ref.at[sl, k], sem_ref.at[sl, k], tm)

    @pl.when(i == 0)
    def _():
        start(0, 0)

    start(jnp.minimum(i + 1, n - 1), 1 - slot)
    wait(slot)
    wt = wt_ref[...]
    moe = (wt[:, 0:1] * _from_row_tiles(buf_ref.at[slot, 0], tm)
           + wt[:, 1:2] * _from_row_tiles(buf_ref.at[slot, 1], tm))
    out = x_ref[...] + g2_ref[...] * moe
    if final_norm:
        out = out * lax.rsqrt(jnp.mean(out * out, axis=-1, keepdims=True) + NORM_EPS) * fw_ref[...]
    o_ref[...] = out

    @pl.when(i == n - 1)
    def _():
        wait(1 - slot)


def moe_combine(ys, dest, x2, wts, g2, final_w, seq, final_norm):
    t, d = x2.shape
    tm = min(COMB_TM, seq)
    per_b = seq // tm
    grid_spec = pltpu.PrefetchScalarGridSpec(
        num_scalar_prefetch=1,
        grid=(t // tm,),
        in_specs=[
            pl.BlockSpec(memory_space=pl.ANY),
            pl.BlockSpec((tm, d), lambda i, ds: (i, 0)),
            pl.BlockSpec((tm, LANE), lambda i, ds: (i, 0)),
            pl.BlockSpec((None, 1, d), lambda i, ds: (i // per_b, 0, 0)),
            pl.BlockSpec((1, d), lambda i, ds: (0, 0)),
        ],
        out_specs=pl.BlockSpec((tm, d), lambda i, ds: (i, 0)),
        scratch_shapes=[pltpu.VMEM((2, TOP_K, tm * ROW_TILE, LANE), F32), pltpu.SemaphoreType.DMA((2, TOP_K))],
    )
    return pl.pallas_call(
        functools.partial(_combine_kernel, n_tok=t, final_norm=final_norm),
        grid_spec=grid_spec,
        out_shape=jax.ShapeDtypeStruct((t, d), F32),
        compiler_params=_cparams(("arbitrary",)),
        name="moe_combine",
    )(dest, ys, x2, wts, g2, final_w.reshape(1, d))


def kernel(x, c, w_ada, b_ada, norm_mix_w, norm_ffn_w, w_in, conv_w, conv_b, ssd_dt_bias, ssd_a_log, ssd_d, ssd_norm_w, diff_lambda, diff_subln_w, hgrn_lb_logits, hgrn_norm_w, w_merge, w_out, w_router, router_bias, w_expert_gate, w_expert_up, w_expert_down, final_norm_w):
    b, s, d = x.shape
    t = b * s
    depth = w_in.shape[0]
    mod = ada_modulation(c, w_ada, b_ada)
    lb_p = jax.nn.softmax(hgrn_lb_logits.astype(F32), axis=0)
    lower_bounds = jnp.cumsum(lb_p, axis=0) - lb_p[0]
    w_router_t = w_router.T
    x2 = x.reshape(t, d)
    ys = jnp.zeros(((TOP_K * t + N_EXPERTS * MOE_BM) * ROW_TILE, LANE), F32)
    for l in range(depth):
        sh1, sc1, g1, sh2, sc2, g2 = [mod[l, :, i * d:(i + 1) * d].reshape(b, 1, d) for i in range(6)]
        p2 = in_projection(x2, norm_mix_w[l], sc1, sh1, pad_w_in(w_in[l]), s)
        p3 = p2.reshape(b, s, -1)
        o_a = ssd_branch(p3, conv_w[l], conv_b[l], ssd_dt_bias[l], ssd_a_log[l], ssd_d[l], ssd_norm_w[l])
        o_b = dilated_branch(p3)
        o_c = diff_branch(p3, diff_lambda[l], diff_subln_w[l], l)
        o_d = hgrn_branch(p3, lower_bounds[l], hgrn_norm_w[l])
        flat = lambda o: o.reshape(t, BRANCH_W)
        x_mid, h2, idx, pos, wts, counts = merge_and_route(
            flat(o_a), flat(o_b), flat(o_c), flat(o_d), p2, x2, g1, w_merge[l].astype(BF16), w_out[l].astype(BF16),
            norm_ffn_w[l], sc2, sh2, w_router_t, router_bias, s)
        dest, blk_e, n_used = moe_plan(idx, pos, counts)
        xs = moe_dispatch(h2, dest, ys)
        ys = moe_experts(xs, blk_e, n_used, w_expert_gate, w_expert_up, w_expert_down, l)
        x2 = moe_combine(ys, dest, x_mid, wts, g2, final_norm_w, s, final_norm=(l == depth - 1))
    return x2.reshape(b, s, d)
```

```python
import functools
import math

import jax
import jax.numpy as jnp
from jax import lax
from jax.experimental import pallas as pl
from jax.experimental.pallas import tpu as pltpu

F32 = jnp.float32
BF16 = jnp.bfloat16

D_MODEL = 1024
DEPTH = 2
N_BRANCH = 4
BRANCH_W = 512

SSD_D_INNER = 512
SSD_HEAD_DIM = 64
SSD_HEADS = 8
SSD_GROUPS = 2
SSD_HPG = 4
SSD_STATE = 64
SSD_CONV = 4
SSD_CHUNK = 128
SSD_STEP_CHUNKS = 2
SSD_HALO = 16
SSD_CONV_CH = 768

DIL_PAIRS = ((128, 1), (512, 4), (2048, 16))
DIL_HPG = 4
DIL_HEAD_DIM = 128
DIL_HEADS = 12
DIL_BLOCK = 128

DIFF_HEADS = 4
DIFF_HEAD_DIM = 64

HG_HEADS = 8
HG_KEY_DIM = 64
HG_VAL_DIM = 64
HG_CHUNK = 64

N_EXPERTS = 64
N_EXPERT_GROUPS = 8
EXPERTS_PER_GROUP = 8
TOP_K = 2
D_EXPERT = 256

NORM_EPS = 1e-6
MASK_VALUE = -1e30

LANE = 128
VMEM_LIMIT = 48 * 1024 * 1024

C_GATE = 0
C_Z = 32
C_XBC = 36
C_DT = 42
C_QB = 44
C_KB = 56
C_VB = 68
C_QC = 80
C_KC = 84
C_VC = 88
C_QD = 92
C_FD = 96
C_ID = 100
C_GD = 104
N_COLB = 108
D_IN_PAD = N_COLB * LANE
DT_PAD = 2 * LANE


def _cparams(sem):
    return pltpu.CompilerParams(dimension_semantics=sem, vmem_limit_bytes=VMEM_LIMIT)


def _aligned(x, m):
    return x if isinstance(x, int) else pl.multiple_of(x, m)


def _sigmoid(v):
    return 0.5 * jnp.tanh(0.5 * v) + 0.5


def _silu(v):
    return v * _sigmoid(v)


def _bdot(a, b):
    return jnp.dot(a.astype(BF16), b.astype(BF16), preferred_element_type=F32)


def _bdot_nt(a, b):
    return lax.dot_general(a.astype(BF16), b.astype(BF16), (((1,), (1,)), ((), ())),
                           preferred_element_type=F32)


def _bdot_tn(a, b):
    return lax.dot_general(a.astype(BF16), b.astype(BF16), (((0,), (0,)), ((), ())),
                           preferred_element_type=F32)


def _dot_rhs01(x, m01):
    hi, lo = _split_bf16(x)
    return jnp.dot(hi, m01, preferred_element_type=F32) + jnp.dot(lo, m01, preferred_element_type=F32)


def _exact_rows_dot(m01, v):
    hi = v.astype(BF16)
    r1 = v - hi.astype(F32)
    mid = r1.astype(BF16)
    lo = (r1 - mid.astype(F32)).astype(BF16)
    m = m01.astype(BF16)
    return (jnp.dot(m, hi, preferred_element_type=F32) + jnp.dot(m, mid, preferred_element_type=F32)
            + jnp.dot(m, lo, preferred_element_type=F32))


def _ada_kernel(c_ref, w_ref, b_ref, o_ref):
    o_ref[...] = _bdot(_silu(c_ref[...]), w_ref[...]) + b_ref[...]


def ada_modulation(c, w_ada, b_ada):
    depth, d, n = w_ada.shape
    b = c.shape[0]
    bp = 8
    c_pad = jnp.zeros((bp, d), F32).at[:b].set(c)
    tn = 1536
    out = pl.pallas_call(
        _ada_kernel,
        grid=(depth, n // tn),
        in_specs=[
            pl.BlockSpec((bp, d), lambda l, j: (0, 0)),
            pl.BlockSpec((None, d, tn), lambda l, j: (l, 0, j)),
            pl.BlockSpec((None, 1, tn), lambda l, j: (l, 0, j)),
        ],
        out_specs=pl.BlockSpec((None, bp, tn), lambda l, j: (l, 0, j)),
        out_shape=jax.ShapeDtypeStruct((depth, bp, n), F32),
        compiler_params=_cparams(("arbitrary", "arbitrary")),
        name="ada_mod",
    )(c_pad, w_ada, b_ada.reshape(depth, 1, n))
    return out[:, :b]


INPROJ_TM = 2048
INPROJ_TN = 1536


def _inproj_kernel(x_ref, nw_ref, sc_ref, sh_ref, w_ref, o_ref, h_ref):
    @pl.when(pl.program_id(1) == 0)
    def _():
        x = x_ref[...]
        y = x * lax.rsqrt(jnp.mean(x * x, axis=-1, keepdims=True) + NORM_EPS) * nw_ref[...]
        h_ref[...] = (y * (1.0 + sc_ref[...]) + sh_ref[...]).astype(BF16)

    o_ref[...] = jnp.dot(h_ref[...], w_ref[...], preferred_element_type=F32).astype(o_ref.dtype)


def in_projection(x2, norm_w, scale, shift, w_pad, seq):
    t, d = x2.shape
    n = w_pad.shape[1]
    tm = min(INPROJ_TM, seq)
    tn = INPROJ_TN
    per_b = seq // tm
    return pl.pallas_call(
        _inproj_kernel,
        grid=(t // tm, n // tn),
        in_specs=[
            pl.BlockSpec((tm, d), lambda i, j: (i, 0)),
            pl.BlockSpec((1, d), lambda i, j: (0, 0)),
            pl.BlockSpec((None, 1, d), lambda i, j: (i // per_b, 0, 0)),
            pl.BlockSpec((None, 1, d), lambda i, j: (i // per_b, 0, 0)),
            pl.BlockSpec((d, tn), lambda i, j: (0, j)),
        ],
        out_specs=pl.BlockSpec((tm, tn), lambda i, j: (i, j)),
        out_shape=jax.ShapeDtypeStruct((t, n), BF16),
        scratch_shapes=[pltpu.VMEM((tm, d), BF16)],
        compiler_params=_cparams(("arbitrary", "arbitrary")),
        name="in_proj",
    )(x2, norm_w.reshape(1, d), scale, shift, w_pad)


def pad_w_in(w_in_l):
    d = w_in_l.shape[0]
    o_dt = SSD_D_INNER + SSD_CONV_CH
    o_gate = w_in_l.shape[1] - N_BRANCH * D_MODEL
    return jnp.concatenate(
        [w_in_l[:, o_gate:], w_in_l[:, :o_dt + SSD_HEADS], jnp.zeros((d, DT_PAD - SSD_HEADS), w_in_l.dtype),
         w_in_l[:, o_dt + SSD_HEADS:o_gate]], axis=1).astype(BF16)


def _ssd_kernel(z_ref, xbc_ref, dt_ref, cw_ref, cb_ref, dtb_ref, alog_ref, dsk_ref, nw_ref, exp_ref, o_ref,
                xpad_ref, state_ref, y_ref):
    c = pl.program_id(1)
    L = SSD_CHUNK
    halo = SSD_HALO

    @pl.when(c == 0)
    def _():
        xpad_ref[L:L + halo, :] = jnp.zeros((halo, SSD_CONV_CH), BF16)
        state_ref[...] = jnp.zeros_like(state_ref)

    for ci in range(SSD_STEP_CHUNKS):
        rows = slice(ci * L, (ci + 1) * L)
        _ssd_chunk(z_ref[rows, :], xbc_ref[rows, :], dt_ref[rows, :], cw_ref, cb_ref, dtb_ref, alog_ref, dsk_ref, nw_ref,
                   exp_ref, o_ref, xpad_ref, state_ref, y_ref, rows)


def _ssd_chunk(z_in, xbc_in, dt_in, cw_ref, cb_ref, dtb_ref, alog_ref, dsk_ref, nw_ref, exp_ref, o_ref, xpad_ref, state_ref,
               y_ref, rows):
    L = SSD_CHUNK
    halo = SSD_HALO
    xpad_ref[0:halo, :] = xpad_ref[L:L + halo, :]
    z = z_in.astype(F32)
    xpad_ref[halo:halo + L, :] = xbc_in

    xpad = xpad_ref[...]
    t_i = lax.broadcasted_iota(jnp.int32, (L, halo + L), 0)
    m_i = lax.broadcasted_iota(jnp.int32, (L, halo + L), 1)
    conv = cb_ref[...] + cw_ref[SSD_CONV - 1:SSD_CONV, :] * xbc_in.astype(F32)
    for j in range(SSD_CONV - 1):
        shift = jnp.where(m_i == t_i + (halo - (SSD_CONV - 1) + j), 1.0, 0.0).astype(BF16)
        conv = conv + cw_ref[j:j + 1, :] * jnp.dot(shift, xpad, preferred_element_type=F32)
    xc = _silu(conv)
    xs = xc[:, 0:SSD_D_INNER]
    ns = SSD_GROUPS * SSD_STATE
    bm = xc[:, SSD_D_INNER:SSD_D_INNER + ns]
    cm = xc[:, SSD_D_INNER + ns:SSD_D_INNER + 2 * ns]

    dt = jax.nn.softplus(dt_in[:, 0:LANE].astype(F32) + dtb_ref[...])
    a = -jnp.exp(alog_ref[...])
    adt = dt * a
    row = lax.broadcasted_iota(jnp.int32, (L, L), 0)
    col = lax.broadcasted_iota(jnp.int32, (L, L), 1)
    tril = row >= col
    tril_f = jnp.where(tril, 1.0, 0.0)
    a_cs = _exact_rows_dot(tril_f, adt)
    a_cs_t = a_cs.T

    expand = exp_ref[...]
    dt_x = _dot_rhs01(dt, expand)
    a_cs_x = _exact_rows_dot(tril_f, _dot_rhs01(adt, expand))
    tot_x = a_cs_x[L - 1:L, :]
    xdt = xs * dt_x
    xdt_b = xdt.astype(BF16)
    xdec_b = (xdt * jnp.exp(tot_x - a_cs_x)).astype(BF16)
    dec_out = jnp.exp(a_cs_x)
    dec_tot = jnp.exp(tot_x)

    npair = SSD_HEADS // 2
    grp_of_pair = [(2 * p) // SSD_HPG for p in range(npair)]
    lane = lax.broadcasted_iota(jnp.int32, (L, LANE), 1)
    first = lane < SSD_STATE
    bm_b = bm.astype(BF16)
    cm_b = cm.astype(BF16)
    cbs = [lax.dot_general(jnp.where(first if g == 0 else jnp.logical_not(first), cm, 0.0).astype(BF16), bm_b,
                           (((1,), (1,)), ((), ())), preferred_element_type=F32) for g in range(SSD_GROUPS)]
    lmats = [jnp.exp(jnp.where(tril, jnp.broadcast_to(a_cs[:, e:e + 1], (L, L)) - a_cs_t[e:e + 1, :], MASK_VALUE))
             for e in range(SSD_HEADS)]
    ms = [(cbs[e // SSD_HPG] * lmats[e]).astype(BF16) for e in range(SSD_HEADS)]
    sls = [slice(p * LANE, (p + 1) * LANE) for p in range(npair)]
    diag = [jnp.dot(ms[e], xdt_b[:, sls[e // 2]], preferred_element_type=F32) for e in range(SSD_HEADS)]
    sts = [state_ref[p] for p in range(npair)]
    offs = [jnp.dot(cm_b, sts[p].astype(BF16), preferred_element_type=F32) for p in range(npair)]
    bm_t = bm.T.astype(BF16)
    locs = [jnp.dot(bm_t, xdec_b[:, sls[p]], preferred_element_type=F32) for p in range(npair)]
    sub = lax.broadcasted_iota(jnp.int32, (LANE, LANE), 0)
    for p in range(npair):
        own_rows = (sub < SSD_STATE) if grp_of_pair[p] == 0 else (sub >= SSD_STATE)
        state_ref[p] = dec_tot[:, sls[p]] * sts[p] + jnp.where(own_rows, locs[p], 0.0)
    for p in range(npair):
        y = jnp.where(first, diag[2 * p], diag[2 * p + 1]) + offs[p] * dec_out[:, sls[p]] + dsk_ref[:, sls[p]] * xs[:, sls[p]]
        y_ref[:, sls[p]] = y

    yz = y_ref[...] * _silu(z)
    o_ref[rows, :] = yz * lax.rsqrt(jnp.mean(yz * yz, axis=-1, keepdims=True) + NORM_EPS) * nw_ref[...]


def _pad_lanes(v, n=LANE):
    return jnp.zeros((1, n), F32).at[0, :v.shape[0]].set(v.astype(F32))


def ssd_branch(p3, conv_w, conv_b, dt_bias, a_log, d_skip, norm_w):
    b, s, _ = p3.shape
    L = SSD_CHUNK * SSD_STEP_CHUNKS
    const = lambda shape: pl.BlockSpec(shape, lambda i, c: (0, 0))
    assert SSD_HEAD_DIM == SSD_STATE and LANE == 2 * SSD_HEAD_DIM and SSD_HPG % 2 == 0
    import numpy as np
    expand = (np.arange(LANE)[:, None] == np.arange(SSD_D_INNER)[None, :] // SSD_HEAD_DIM).astype(np.float32)
    return pl.pallas_call(
        _ssd_kernel,
        grid=(b, s // L),
        in_specs=[
            pl.BlockSpec((None, L, SSD_D_INNER), lambda i, c: (i, c, C_Z * LANE // SSD_D_INNER)),
            pl.BlockSpec((None, L, SSD_CONV_CH), lambda i, c: (i, c, C_XBC * LANE // SSD_CONV_CH)),
            pl.BlockSpec((None, L, DT_PAD), lambda i, c: (i, c, C_DT * LANE // DT_PAD)),
            const((SSD_CONV, SSD_CONV_CH)),
            const((1, SSD_CONV_CH)),
            const((1, LANE)),
            const((1, LANE)),
            const((1, SSD_D_INNER)),
            const((1, SSD_D_INNER)),
            const((LANE, SSD_D_INNER)),
        ],
        out_specs=pl.BlockSpec((None, L, SSD_D_INNER), lambda i, c: (i, c, 0)),
        out_shape=jax.ShapeDtypeStruct((b, s, SSD_D_INNER), F32),
        scratch_shapes=[
            pltpu.VMEM((SSD_CHUNK + SSD_HALO, SSD_CONV_CH), BF16),
            pltpu.VMEM((SSD_HEADS // 2, LANE, LANE), F32),
            pltpu.VMEM((SSD_CHUNK, SSD_D_INNER), F32),
        ],
        compiler_params=_cparams(("arbitrary", "arbitrary")),
        name="ssd",
    )(p3, p3, p3, conv_w, conv_b.reshape(1, -1), _pad_lanes(dt_bias), _pad_lanes(a_log),
      jnp.repeat(d_skip.astype(F32), SSD_HEAD_DIM).reshape(1, -1), norm_w.reshape(1, -1), jnp.asarray(expand, BF16))


DIL_ORDER = (2, 1, 0)
DIL_CHEAP_STRIDE = 4
DIL_UNROLL = 8


def _dil_kernel(q_ref, k_ref, v_ref, o_ref, m_ref, l_ref, tmp_ref, tmp2_ref, qd_ref, kd_ref, vd_ref):
    g = pl.program_id(2)
    s_len = q_ref.shape[0]
    blk = DIL_BLOCK
    scale = DIL_HEAD_DIM ** -0.5

    def run_group(dil, first_group):
        sub = s_len // dil
        nblk = sub // blk

        if dil == 1:
            qd_ref[...] = (q_ref[...].astype(F32) * scale).astype(BF16)
            kd, vd = k_ref, v_ref
        else:
            for src, dst, mul in ((q_ref, qd_ref, scale), (k_ref, kd_ref, None), (v_ref, vd_ref, None)):
                x = src[...].astype(F32)
                tmp_ref[...] = x if mul is None else x * mul

                if dil <= DIL_CHEAP_STRIDE:
                    for r in range(dil):
                        dst[r * sub:(r + 1) * sub, :] = tmp_ref[pl.ds(r, sub, stride=dil), :].astype(BF16)
                else:
                    inner = DIL_CHEAP_STRIDE
                    outer = dil // inner
                    part = s_len // inner
                    for r_lo in range(inner):
                        tmp2_ref[r_lo * part:(r_lo + 1) * part, :] = tmp_ref[pl.ds(r_lo, part, stride=inner), :]
                    for r_lo in range(inner):
                        for r_hi in range(outer):
                            r = r_lo + inner * r_hi
                            dst[r * sub:(r + 1) * sub, :] = tmp2_ref[pl.ds(r_lo * part + r_hi, sub, stride=outer), :].astype(BF16)
            kd, vd = kd_ref, vd_ref

        def block_softmax(r, b, nkey):
            row0 = _aligned(r * sub + b * blk, blk)
            ks = pl.ds(_aligned(row0 - (nkey - blk), blk), nkey)
            i = lax.broadcasted_iota(jnp.int32, (blk, nkey), 0) + (nkey - blk)
            j = lax.broadcasted_iota(jnp.int32, (blk, nkey), 1)
            ok = jnp.logical_and(j <= i, j >= i - blk)
            sc = lax.dot_general(qd_ref[pl.ds(row0, blk), :], kd[ks, :], (((1,), (1,)), ((), ())),
                                 preferred_element_type=F32)
            sc = jnp.where(ok, sc, MASK_VALUE)
            m_blk = jnp.max(sc, axis=-1, keepdims=True)
            p = jnp.exp(sc - m_blk).astype(BF16)
            v_ext = jnp.concatenate([vd[ks, :], jnp.ones((nkey, LANE), BF16)], axis=1)
            return m_blk, jnp.dot(p, v_ext, preferred_element_type=F32)

        def merge(r, b, m_blk, pv):
            base = r + dil * blk * b
            acc_rows = pl.ds(_aligned(base, blk), blk) if dil == 1 else pl.ds(base, blk, stride=dil)
            m_blk = jnp.broadcast_to(m_blk, (blk, LANE))
            if first_group:
                o_ref[acc_rows, :] = pv[:, :LANE]
                l_ref[acc_rows, :] = pv[:, LANE:]
                m_ref[acc_rows, :] = m_blk
                return
            m_old = m_ref[acc_rows, :]
            m_new = jnp.maximum(m_old, m_blk)
            a_old = jnp.exp(m_old - m_new)
            a_blk = jnp.exp(m_blk - m_new)
            o_ref[acc_rows, :] = a_old * o_ref[acc_rows, :] + a_blk * pv[:, :LANE]
            l_ref[acc_rows, :] = a_old * l_ref[acc_rows, :] + a_blk * pv[:, LANE:]
            m_ref[acc_rows, :] = m_new

        def run_units(n, where, nkey):
            def group(units):
                parts = [block_softmax(r, b, nkey) for r, b in units]
                for (r, b), (m_blk, pv) in zip(units, parts):
                    merge(r, b, m_blk, pv)

            def body(i, carry):
                group([where(i * DIL_UNROLL + j) for j in range(DIL_UNROLL)])
                return carry

            lax.fori_loop(0, n // DIL_UNROLL, body, 0)
            if n % DIL_UNROLL:
                group([where(n - n % DIL_UNROLL + j) for j in range(n % DIL_UNROLL)])

        run_units(dil, lambda u: (u, 0), blk)
        if nblk > 1:
            run_units(dil * (nblk - 1), lambda u: (u // (nblk - 1), 1 + u % (nblk - 1)), 2 * blk)

    for step, gi in enumerate(DIL_ORDER):
        window, dil = DIL_PAIRS[gi]
        assert window // dil == blk

        @pl.when(g == step)
        def _(dil=dil, step=step):
            run_group(dil, step == 0)

    @pl.when(g == len(DIL_PAIRS) - 1)
    def _():
        o_ref[...] = o_ref[...] / l_ref[...]


def dilated_branch(p3):
    b, s, _ = p3.shape
    ng = len(DIL_PAIRS)
    last = len(DIL_PAIRS) - 1
    assert DIL_ORDER == tuple(range(last, -1, -1))
    spec = lambda c0: pl.BlockSpec((None, s, DIL_HEAD_DIM), lambda i, j, g: (i, 0, c0 + (last - g) * DIL_HPG + j))
    return pl.pallas_call(
        _dil_kernel,
        grid=(b, DIL_HPG, ng),
        in_specs=[spec(C_QB), spec(C_KB), spec(C_VB)],
        out_specs=pl.BlockSpec((None, s, DIL_HEAD_DIM), lambda i, j, g: (i, 0, j)),
        out_shape=jax.ShapeDtypeStruct((b, s, BRANCH_W), F32),
        scratch_shapes=[pltpu.VMEM((s, LANE), F32), pltpu.VMEM((s, LANE), F32), pltpu.VMEM((s, LANE), F32),
                        pltpu.VMEM((s, LANE), F32), pltpu.VMEM((s, LANE), BF16), pltpu.VMEM((s, LANE), BF16), pltpu.VMEM((s, LANE), BF16)],
        compiler_params=_cparams(("arbitrary", "arbitrary", "arbitrary")),
        name="dilated",
    )(p3, p3, p3)


DIFF_TQ = 512
DIFF_TK = 1024
DIFF_DIAG = 512
LOG2E = 1.4426950408889634


DIFF_ONES = 16


def _diff_kernel(lam_ref, q_ref, k_ref, v_ref, nw_ref, o_ref, vt_ref, m_ref, acc_ref, *, lam_init):
    qi = pl.program_id(2)
    tq, tk, dh = DIFF_TQ, DIFF_TK, DIFF_HEAD_DIM
    w = 2 * dh
    s_len = k_ref.shape[0]

    @pl.when(qi == 0)
    def _():
        for c0 in range(0, s_len, tk):
            vt_ref[0:w, c0:c0 + tk] = v_ref[c0:c0 + tk, :].astype(F32).T.astype(BF16)
        vt_ref[w:w + DIFF_ONES, :] = jnp.ones((DIFF_ONES, s_len), BF16)

    q_t = (q_ref[...].astype(F32) * (dh ** -0.5 * LOG2E)).T
    half = lax.broadcasted_iota(jnp.int32, (w, tq), 0) < dh
    q_sel = [jnp.where(half, q_t, 0.0).astype(BF16), jnp.where(half, 0.0, q_t).astype(BF16)]
    for t in range(2):
        m_ref[t] = jnp.full((8, tq), MASK_VALUE, F32)
        acc_ref[t] = jnp.zeros((w + DIFF_ONES, tq), F32)

    def blocks(items):
        work = []
        for kstart, width, qlo in items:
            c0 = qlo or 0
            nq = tq - c0
            kb = k_ref[pl.ds(kstart, width), :]
            vb = vt_ref[:, pl.ds(kstart, width)]
            for t in range(2):
                sc = jnp.dot(kb, q_sel[t][:, c0:], preferred_element_type=F32)
                if qlo is not None:
                    row = lax.broadcasted_iota(jnp.int32, (width, nq), 0)
                    col = lax.broadcasted_iota(jnp.int32, (width, nq), 1)
                    sc = jnp.where(row <= col, sc, MASK_VALUE)
                work.append((t, c0, nq, width, vb, sc))
        m_blks = [jnp.max(jnp.max(sc.reshape(width // 8, 8, nq), axis=0), axis=0, keepdims=True)
                  for (t, c0, nq, width, vb, sc) in work]
        pvs = [jnp.dot(vb, jnp.exp2(sc - m_blk).astype(BF16), preferred_element_type=F32)
               for (t, c0, nq, width, vb, sc), m_blk in zip(work, m_blks)]
        for (t, c0, nq, width, vb, sc), m_blk, pv in zip(work, m_blks, pvs):
            m_old = m_ref[t, 0:1, c0:]
            m_new = jnp.maximum(m_old, m_blk)
            acc_ref[t, :, c0:] = acc_ref[t, :, c0:] * jnp.exp2(m_old - m_new) + pv * jnp.exp2(m_blk - m_new)
            m_ref[t, :, c0:] = jnp.broadcast_to(m_new, (8, nq))

    def main_step(j, carry):
        blocks([(pl.multiple_of(j * tk, tk), tk, None)])
        return carry

    per = tk // tq
    lax.fori_loop(0, qi // per, main_step, 0)
    for r in range(per):
        @pl.when(qi % per == r)
        def _(r=r):
            left = [(pl.multiple_of((qi // per) * tk + i * tq, tq), tq, None) for i in range(r)]
            diag = [(pl.multiple_of(qi * tq + c0, DIFF_DIAG), DIFF_DIAG, c0) for c0 in range(0, tq, DIFF_DIAG)]
            blocks(left + diag)

    lam_p = lam_ref[...]
    lam = (jnp.exp(jnp.sum(lam_p[0:1] * lam_p[1:2], axis=-1, keepdims=True))
           - jnp.exp(jnp.sum(lam_p[2:3] * lam_p[3:4], axis=-1, keepdims=True)) + lam_init)
    a0, a1 = acc_ref[0], acc_ref[1]
    o_t = a0[0:w] / a0[w:w + 1] - lam * (a1[0:w] / a1[w:w + 1])
    o = o_t.T
    o = o * lax.rsqrt(jnp.mean(o * o, axis=-1, keepdims=True) + NORM_EPS) * nw_ref[...]
    o_ref[...] = o * (1.0 - lam_init)


def diff_branch(p3, diff_lambda, subln_w, layer):
    b, s, _ = p3.shape
    lam_init = 0.8 - 0.6 * math.exp(-0.3 * layer)
    w = 2 * DIFF_HEAD_DIM
    return pl.pallas_call(
        functools.partial(_diff_kernel, lam_init=lam_init),
        grid=(b, DIFF_HEADS, s // DIFF_TQ),
        in_specs=[
            pl.BlockSpec((4, DIFF_HEAD_DIM), lambda i, h, t: (0, 0)),
            pl.BlockSpec((None, DIFF_TQ, w), lambda i, h, t: (i, t, C_QC + h)),
            pl.BlockSpec((None, s, w), lambda i, h, t: (i, 0, C_KC + h)),
            pl.BlockSpec((None, s, w), lambda i, h, t: (i, 0, C_VC + h)),
            pl.BlockSpec((1, w), lambda i, h, t: (0, 0)),
        ],
        out_specs=pl.BlockSpec((None, DIFF_TQ, w), lambda i, h, t: (i, t, h)),
        out_shape=jax.ShapeDtypeStruct((b, s, BRANCH_W), F32),
        scratch_shapes=[pltpu.VMEM((w + DIFF_ONES, s), BF16), pltpu.VMEM((2, 8, DIFF_TQ), F32),
                        pltpu.VMEM((2, w + DIFF_ONES, DIFF_TQ), F32)],
        compiler_params=_cparams(("arbitrary", "arbitrary", "arbitrary")),
        name="diff_attn",
    )(diff_lambda, p3, p3, p3, subln_w.reshape(1, w))


HG_LEVELS = 6
HG_STEP_CHUNKS = 4


def _hgrn_tables():
    import numpy as np
    c = HG_CHUNK
    i = np.arange(c)[:, None]
    j = np.arange(c)[None, :]
    tril = (j <= i).astype(np.float32)
    masks = []
    for lv in range(HG_LEVELS):
        sz = 1 << lv
        m = ((i // (2 * sz) == j // (2 * sz)) & (i // sz == j // sz + 1)).astype(np.float32)
        masks.append(np.concatenate([m, m], axis=0))
    lane = np.arange(LANE)
    same_head = (lane[:, None] // HG_KEY_DIM == lane[None, :] // HG_KEY_DIM).astype(np.float32)
    return tril, np.stack(masks, axis=0), same_head


def _hgrn_kernel(q_ref, f_ref, i_ref, g_ref, lb_ref, tri_ref, msk_ref, sh_ref, nw_ref, o_ref, state_ref, gpad_ref):
    @pl.when(pl.program_id(1) == 0)
    def _():
        state_ref[...] = jnp.zeros_like(state_ref)

    for ci in range(HG_STEP_CHUNKS):
        rows = slice(ci * HG_CHUNK, (ci + 1) * HG_CHUNK)
        _hgrn_chunk(q_ref[rows, :], f_ref[rows, :], i_ref[rows, :], g_ref, lb_ref, tri_ref, msk_ref, sh_ref, nw_ref,
                    o_ref, state_ref, gpad_ref, rows)


def _hgrn_chunk(q_in, f_in, i_in, g_ref, lb_ref, tri_ref, msk_ref, sh_ref, nw_ref, o_ref, state_ref, gpad_ref, rows):
    cs = HG_CHUNK
    nl = HG_LEVELS
    w = q_in.shape[1]
    sub_rows = 8

    lb = lb_ref[...]
    f_gate = lb + (1.0 - lb) * jax.nn.sigmoid(f_in.astype(F32))
    log_f = jnp.log(f_gate)
    k_in = 1.0 - f_gate
    q = _silu(q_in.astype(F32))
    v = i_in.astype(F32)

    hi, lo = _split_bf16(log_f)
    tri = tri_ref[...]
    g = jnp.dot(tri, hi, preferred_element_type=F32) + jnp.dot(tri, lo, preferred_element_type=F32)
    gpad_ref[0:sub_rows, :] = jnp.zeros((sub_rows, w), F32)
    gpad_ref[sub_rows:sub_rows + cs, :] = g
    g_last = g[cs - 1:cs, :]

    def g_row(r):
        return jnp.broadcast_to(gpad_ref[sub_rows + r:sub_rows + r + 1, :], (sub_rows, w))

    sub = lax.broadcasted_iota(jnp.int32, (sub_rows, w), 0)
    tiles = [[] for _ in range(nl)]
    for t in range(cs // sub_rows):
        r0 = t * sub_rows
        gt = g[r0:r0 + sub_rows]
        prev = gpad_ref[r0 + sub_rows - 1:r0 + 2 * sub_rows - 1, :]
        tiles[0].append(jnp.where(sub % 2 == 1, gt - prev, 0.0))
        c1 = jnp.where(sub < 4, g_row(r0 + 1), g_row(r0 + 5))
        tiles[1].append(jnp.where((sub // 2) % 2 == 1, gt - c1, c1 - gt))
        c2 = g_row(r0 + 3)
        tiles[2].append(jnp.where(sub >= 4, gt - c2, c2 - gt))
        for lv in range(3, nl):
            span = (1 << lv) // sub_rows
            mid = (t // (2 * span)) * 2 * span + span
            cm = g_row(mid * sub_rows - 1)
            tiles[lv].append(gt - cm if (t // span) % 2 == 1 else cm - gt)
    dec = [jnp.exp(jnp.concatenate(tl, axis=0)) for tl in tiles]
    first_w = lax.broadcasted_iota(jnp.int32, (cs, w), 1) % LANE < HG_KEY_DIM
    q_even = jnp.where(first_w, q, 0.0).astype(BF16)
    q_odd = jnp.where(first_w, 0.0, q).astype(BF16)
    k_b = k_in.astype(BF16)
    dec_b = [d.astype(BF16) for d in dec]
    q_full = q * jnp.exp(g)
    k_full = k_in * jnp.exp(g_last - g)
    decay_row = jnp.exp(g_last)

    same_head = sh_ref[...]
    same_head_b = same_head.astype(BF16)
    first = lax.broadcasted_iota(jnp.int32, (cs, LANE), 1) < HG_KEY_DIM
    pairs = range(w // LANE)
    sls = [slice(p * LANE, (p + 1) * LANE) for p in pairs]
    attns = [jnp.zeros((2 * cs, cs), F32) for _ in pairs]
    for lv in range(nl):
        for p in pairs:
            db = dec_b[lv][:, sls[p]]
            lhs = jnp.concatenate([q_even[:, sls[p]] * db, q_odd[:, sls[p]] * db], axis=0)
            scores = lax.dot_general(lhs, k_b[:, sls[p]] * db, (((1,), (1,)), ((), ())), preferred_element_type=F32)
            attns[p] = attns[p] + msk_ref[lv] * scores
    vps = [v[:, sl] for sl in sls]
    vpbs = [vp.astype(BF16) for vp in vps]
    rs = [jnp.dot(attns[p].astype(BF16), vpbs[p], preferred_element_type=F32) for p in pairs]
    diags = [jnp.dot((q[:, sl] * k_in[:, sl]).astype(BF16), same_head_b, preferred_element_type=F32) for sl in sls]
    sts = [state_ref[p] for p in pairs]
    inters = [_bdot_nt(q_full[:, sls[p]], sts[p]) for p in pairs]
    upds = [_bdot_tn(vpbs[p], k_full[:, sls[p]]) for p in pairs]
    for p in pairs:
        state_ref[p] = decay_row[:, sls[p]] * sts[p] + same_head * upds[p]
    outs = [jnp.where(first, rs[p][0:cs], rs[p][cs:2 * cs]) + diags[p] * vps[p] + inters[p] for p in pairs]
    mss = [jnp.dot((o * o).astype(BF16), same_head_b, preferred_element_type=F32) * (1.0 / HG_VAL_DIM) for o in outs]
    for p in pairs:
        o = outs[p] * lax.rsqrt(mss[p] + NORM_EPS) * nw_ref[...]
        o_ref[rows, sls[p]] = o * _silu(g_ref[rows, sls[p]].astype(F32))


def hgrn_branch(p3, lower_bound, norm_w):
    b, s, _ = p3.shape
    cs = HG_CHUNK * HG_STEP_CHUNKS
    w = HG_HEADS * HG_KEY_DIM
    assert HG_KEY_DIM == HG_VAL_DIM and LANE == 2 * HG_KEY_DIM
    tril, masks, same_head = _hgrn_tables()
    seg = lambda c0: pl.BlockSpec((None, cs, w), lambda i, c: (i, c, c0 * LANE // w))
    return pl.pallas_call(
        _hgrn_kernel,
        grid=(b, s // cs),
        in_specs=[
            seg(C_QD), seg(C_FD), seg(C_ID), seg(C_GD),
            pl.BlockSpec((1, w), lambda i, c: (0, 0)),
            pl.BlockSpec(tril.shape, lambda i, c: (0, 0)),
            pl.BlockSpec(masks.shape, lambda i, c: (0, 0, 0)),
            pl.BlockSpec(same_head.shape, lambda i, c: (0, 0)),
            pl.BlockSpec((1, LANE), lambda i, c: (0, 0)),
        ],
        out_specs=pl.BlockSpec((None, cs, w), lambda i, c: (i, c, 0)),
        out_shape=jax.ShapeDtypeStruct((b, s, BRANCH_W), F32),
        scratch_shapes=[pltpu.VMEM((w // LANE, LANE, LANE), F32), pltpu.VMEM((HG_CHUNK + 8, w), F32)],
        compiler_params=_cparams(("arbitrary", "arbitrary")),
        name="hgrn2",
    )(p3, p3, p3, p3, lower_bound.reshape(1, w).astype(F32), jnp.asarray(tril, BF16), jnp.asarray(masks, F32),
      jnp.asarray(same_head, F32), jnp.tile(norm_w.astype(F32), LANE // HG_VAL_DIM).reshape(1, LANE))


MERGE_TM = 512


def _split_bf16(v):
    hi = v.astype(BF16)
    return hi, (v - hi.astype(F32)).astype(BF16)


def _first_index_of_max(vals, iota, n):
    top = jnp.max(vals, axis=0, keepdims=True)
    idx = jnp.min(jnp.where(vals == top, iota, n), axis=0, keepdims=True)
    return top, idx


def _merge_kernel(oa_ref, ob_ref, oc_ref, od_ref, gl_ref, x_ref, g1_ref, wm_ref, wo_ref,
                  nw_ref, sc_ref, sh_ref, wr_ref, rb_ref,
                  xo_ref, h2_ref, idx_ref, pos_ref, wt_ref, cnt_ref, carry_ref):
    step = pl.program_id(0)
    tm = x_ref.shape[0]
    d = D_MODEL

    projs = [jnp.dot(o_ref[...].astype(BF16), wm_ref[n], preferred_element_type=F32)
             for n, o_ref in enumerate((oa_ref, ob_ref, oc_ref, od_ref))]
    acc = jnp.zeros((tm, d), F32)
    for n, proj in enumerate(projs):
        acc = acc + _sigmoid(gl_ref[:, n * d:(n + 1) * d].astype(F32)) * proj
    mix = jnp.dot(acc.astype(BF16), wo_ref[...], preferred_element_type=F32)
    x_new = x_ref[...] + g1_ref[...] * mix
    xo_ref[...] = x_new

    y = x_new * lax.rsqrt(jnp.mean(x_new * x_new, axis=-1, keepdims=True) + NORM_EPS) * nw_ref[...]
    h2 = y * (1.0 + sc_ref[...]) + sh_ref[...]
    _to_row_tiles(h2_ref, h2)

    h_hi, h_lo = _split_bf16(h2)
    w_hi, w_lo = _split_bf16(wr_ref[...])
    nt = lambda a, b: lax.dot_general(a, b, (((1,), (1,)), ((), ())), preferred_element_type=F32)
    logits = nt(w_hi, h_hi) + nt(w_hi, h_lo) + nt(w_lo, h_hi)
    scores = jax.nn.sigmoid(logits)
    sel = scores + rb_ref[...]

    ne, ng, pg = N_EXPERTS, N_EXPERT_GROUPS, EXPERTS_PER_GROUP
    iota_g = lax.broadcasted_iota(jnp.int32, (pg, tm), 0)
    best_score = None
    best_group = None
    for g in range(ng):
        xg = sel[g * pg:(g + 1) * pg, :]
        top1, i1 = _first_index_of_max(xg, iota_g, pg)
        top2 = jnp.max(jnp.where(iota_g == i1, -jnp.inf, xg), axis=0, keepdims=True)
        gs = top1 + top2
        if g == 0:
            best_score, best_group = gs, jnp.zeros((1, tm), jnp.int32)
        else:
            better = gs > best_score
            best_score = jnp.where(better, gs, best_score)
            best_group = jnp.where(better, g, best_group)

    iota_e = lax.broadcasted_iota(jnp.int32, (ne, tm), 0)
    masked = jnp.where(iota_e // pg == best_group, sel, MASK_VALUE)
    _, e1 = _first_index_of_max(masked, iota_e, ne)
    oh1 = iota_e == e1
    _, e2 = _first_index_of_max(jnp.where(oh1, MASK_VALUE, masked), iota_e, ne)
    oh2 = iota_e == e2
    w1 = jnp.sum(jnp.where(oh1, scores, 0.0), axis=0, keepdims=True)
    w2 = jnp.sum(jnp.where(oh2, scores, 0.0), axis=0, keepdims=True)
    wsum = w1 + w2
    w1 = w1 / wsum
    w2 = w2 / wsum

    @pl.when(step == 0)
    def _():
        carry_ref[...] = jnp.zeros_like(carry_ref)

    f1 = jnp.where(oh1, 1.0, 0.0)
    f2 = jnp.where(oh2, 1.0, 0.0)
    both = f1 + f2
    r_i = lax.broadcasted_iota(jnp.int32, (tm, tm), 0)
    c_i = lax.broadcasted_iota(jnp.int32, (tm, tm), 1)
    before = jnp.where(r_i < c_i, 1.0, 0.0).astype(BF16)
    rank = jnp.dot(both.astype(BF16), before, preferred_element_type=F32) + carry_ref[:, 0:1]
    p1 = jnp.sum(f1 * rank, axis=0, keepdims=True)
    p2 = jnp.sum(f2 * rank, axis=0, keepdims=True)
    carry_new = carry_ref[...] + jnp.sum(both, axis=1, keepdims=True)
    carry_ref[...] = carry_new
    cnt_ref[...] = carry_new.astype(jnp.int32)

    idx_ref[...] = jnp.concatenate([e1, e2], axis=0)
    pos_ref[...] = jnp.concatenate([p1, p2], axis=0).astype(jnp.int32)
    wt_rows = jnp.concatenate([w1, w2, jnp.zeros((LANE - 2, tm), F32)], axis=0)
    wt_ref[...] = wt_rows.T


def merge_and_route(o_a, o_b, o_c, o_d, p2, x2, g1, w_merge_bf, w_out_bf, norm_w, sc2, sh2, w_router_t, router_bias, seq):
    t, d = x2.shape
    tm = min(MERGE_TM, seq)
    per_b = seq // tm
    gw = N_BRANCH * d
    tok = lambda w: pl.BlockSpec((tm, w), lambda i: (i, 0))
    bat = pl.BlockSpec((None, 1, d), lambda i: (i // per_b, 0, 0))
    ne = N_EXPERTS
    return pl.pallas_call(
        _merge_kernel,
        grid=(t // tm,),
        in_specs=[
            tok(BRANCH_W), tok(BRANCH_W), tok(BRANCH_W), tok(BRANCH_W),
            pl.BlockSpec((tm, gw), lambda i: (i, C_GATE * LANE // gw)),
            tok(d), bat,
            pl.BlockSpec((N_BRANCH, BRANCH_W, d), lambda i: (0, 0, 0)),
            pl.BlockSpec((d, d), lambda i: (0, 0)),
            pl.BlockSpec((1, d), lambda i: (0, 0)),
            bat, bat,
            pl.BlockSpec((ne, d), lambda i: (0, 0)),
            pl.BlockSpec((ne, 1), lambda i: (0, 0)),
        ],
        out_specs=[
            tok(d), pl.BlockSpec((tm * ROW_TILE, LANE), lambda i: (i, 0)),
            pl.BlockSpec((TOP_K, tm), lambda i: (0, i)),
            pl.BlockSpec((TOP_K, tm), lambda i: (0, i)),
            tok(LANE),
            pl.BlockSpec((ne, LANE), lambda i: (0, 0)),
        ],
        out_shape=[
            jax.ShapeDtypeStruct((t, d), F32),
            jax.ShapeDtypeStruct((t * ROW_TILE, LANE), F32),
            jax.ShapeDtypeStruct((TOP_K, t), jnp.int32),
            jax.ShapeDtypeStruct((TOP_K, t), jnp.int32),
            jax.ShapeDtypeStruct((t, LANE), F32),
            jax.ShapeDtypeStruct((ne, LANE), jnp.int32),
        ],
        scratch_shapes=[pltpu.VMEM((ne, LANE), F32)],
        compiler_params=_cparams(("arbitrary",)),
        name="merge_route",
    )(o_a, o_b, o_c, o_d, p2, x2, g1, w_merge_bf, w_out_bf, norm_w.reshape(1, d), sc2, sh2,
      w_router_t, router_bias.reshape(ne, 1))


MOE_BM = 256


def _plan_kernel(idx_ref, pos_ref, cnt_ref, dest_ref, blke_ref, nused_ref):
    bm = MOE_BM
    ne = N_EXPERTS
    cnt = cnt_ref[...].astype(F32)
    padded = jnp.floor((cnt + (bm - 1)) * (1.0 / bm)) * bm
    r = lax.broadcasted_iota(jnp.int32, (ne, ne), 0)
    c = lax.broadcasted_iota(jnp.int32, (ne, ne), 1)
    pstart = _exact_rows_dot(jnp.where(c < r, 1.0, 0.0), padded)
    pend = pstart + padded

    idx = idx_ref[...]
    base = jnp.zeros(idx.shape, F32)
    for e in range(ne):
        base = jnp.where(idx == e, pstart[e:e + 1, 0:1], base)
    dest_ref[...] = base.astype(jnp.int32) + pos_ref[...]

    nbp = blke_ref.shape[1]
    blk_start = (lax.broadcasted_iota(jnp.int32, (ne, nbp), 1) * bm).astype(F32)
    done = jnp.sum(jnp.where(pend[:, 0:1] <= blk_start, 1.0, 0.0), axis=0, keepdims=True)
    blke_ref[...] = jnp.minimum(done, ne - 1.0).astype(jnp.int32)
    nused_ref[...] = (pend[ne - 1:ne, :] * (1.0 / bm)).astype(jnp.int32)


def moe_plan(idx, pos, counts):
    n_tok = idx.shape[1]
    n_rows = TOP_K * n_tok + N_EXPERTS * MOE_BM
    n_blocks = n_rows // MOE_BM
    nbp = -(-n_blocks // LANE) * LANE
    dest, blk_e, n_used = pl.pallas_call(
        _plan_kernel,
        out_shape=[
            jax.ShapeDtypeStruct((TOP_K, n_tok), jnp.int32),
            jax.ShapeDtypeStruct((1, nbp), jnp.int32),
            jax.ShapeDtypeStruct((1, LANE), jnp.int32),
        ],
        compiler_params=pltpu.CompilerParams(vmem_limit_bytes=VMEM_LIMIT),
        name="moe_plan",
    )(idx, pos, counts)
    return dest.reshape(-1), blk_e[0, :n_blocks], n_used[0, :1]


ROW_TILE = 8
DISPATCH_TM = 512


def _to_row_tiles(ref, x):
    rows = x.shape[0]
    for j in range(ROW_TILE):
        ref[pl.ds(j, rows, stride=ROW_TILE), :] = x[:, j * LANE:(j + 1) * LANE]


def _from_row_tiles(ref, rows):
    return jnp.concatenate([ref[pl.ds(j, rows, stride=ROW_TILE), :] for j in range(ROW_TILE)], axis=1)


def _tile_rows(row):
    return pl.ds(_aligned(row * ROW_TILE, ROW_TILE), ROW_TILE)


def _dispatch_kernel(dest_ref, h_ref, xs_init_hbm, xs_hbm, sem_ref, *, n_tok):
    del xs_init_hbm
    i = pl.program_id(0)
    tm = h_ref.shape[0] // ROW_TILE

    def copy(r, k, row):
        return pltpu.make_async_copy(h_ref.at[pl.ds(r * ROW_TILE, ROW_TILE), :], xs_hbm.at[_tile_rows(row), :], sem_ref.at[k])

    for r in range(tm):
        for k in range(TOP_K):
            copy(r, k, dest_ref[k * n_tok + i * tm + r]).start(priority=k)
    for k in range(TOP_K):
        for r in range(tm):
            copy(r, k, 0).wait()


def moe_dispatch(h2_tiles, dest, xs_init):
    n_tok = dest.shape[0] // TOP_K
    tm = min(DISPATCH_TM, n_tok)
    grid_spec = pltpu.PrefetchScalarGridSpec(
        num_scalar_prefetch=1,
        grid=(n_tok // tm,),
        in_specs=[pl.BlockSpec((tm * ROW_TILE, LANE), lambda i, ds: (i, 0)), pl.BlockSpec(memory_space=pl.ANY)],
        out_specs=pl.BlockSpec(memory_space=pl.ANY),
        scratch_shapes=[pltpu.SemaphoreType.DMA((TOP_K,))],
    )
    return pl.pallas_call(
        functools.partial(_dispatch_kernel, n_tok=n_tok),
        grid_spec=grid_spec,
        out_shape=jax.ShapeDtypeStruct(xs_init.shape, F32),
        input_output_aliases={2: 0},
        compiler_params=_cparams(("arbitrary",)),
        name="moe_dispatch",
    )(dest, h2_tiles, xs_init)


def _expert_kernel(blke_ref, nused_ref, x_ref, wg_ref, wu_ref, wd_ref, y_ref, wgb_ref, wub_ref, wdb_ref):
    b = pl.program_id(0)
    bm = MOE_BM
    used = nused_ref[0]
    changed = jnp.logical_or(b == 0, blke_ref[b] != blke_ref[jnp.maximum(b - 1, 0)])

    @pl.when(jnp.logical_and(b < used, changed))
    def _():
        wgb_ref[...] = wg_ref[...].astype(BF16)
        wub_ref[...] = wu_ref[...].astype(BF16)
        wdb_ref[...] = wd_ref[...].astype(BF16)

    @pl.when(b < used)
    def _():
        xb = _from_row_tiles(x_ref, bm).astype(BF16)
        hid = _silu(jnp.dot(xb, wgb_ref[...], preferred_element_type=F32)) * jnp.dot(xb, wub_ref[...], preferred_element_type=F32)
        _to_row_tiles(y_ref, jnp.dot(hid.astype(BF16), wdb_ref[...], preferred_element_type=F32))

    @pl.when(b >= used)
    def _():
        y_ref[...] = jnp.zeros_like(y_ref)


def moe_experts(xs, blk_e, n_used, w_gate, w_up, w_down, layer):
    bm = MOE_BM
    n_rows = xs.shape[0] // ROW_TILE
    d, de = w_gate.shape[-2:]
    assert d == ROW_TILE * LANE
    wspec = lambda r, c: pl.BlockSpec((None, None, r, c), lambda b, be, nu: (layer, be[b], 0, 0))
    rows = pl.BlockSpec((bm * ROW_TILE, LANE), lambda b, be, nu: (b, 0))
    rows_in = pl.BlockSpec((bm * ROW_TILE, LANE), lambda b, be, nu: (jnp.minimum(b, nu[0]), 0))
    grid_spec = pltpu.PrefetchScalarGridSpec(
        num_scalar_prefetch=2,
        grid=(n_rows // bm,),
        in_specs=[rows_in, wspec(d, de), wspec(d, de), wspec(de, d)],
        out_specs=rows,
        scratch_shapes=[pltpu.VMEM((d, de), BF16), pltpu.VMEM((d, de), BF16), pltpu.VMEM((de, d), BF16)],
    )
    return pl.pallas_call(
        _expert_kernel,
        grid_spec=grid_spec,
        out_shape=jax.ShapeDtypeStruct(xs.shape, F32),
        compiler_params=_cparams(("arbitrary",)),
        name="moe_experts",
    )(blk_e, n_used, xs, w_gate, w_up, w_down)


COMB_TM = 256


def _tile_copy(src_hbm, row, dst, r, sem):
    return pltpu.make_async_copy(src_hbm.at[_tile_rows(row), :], dst.at[pl.ds(r * ROW_TILE, ROW_TILE), :], sem)


def _gather_start(src_hbm, row_of, dst, sem, n):
    for r in range(n):
        _tile_copy(src_hbm, row_of(r), dst, r, sem).start(priority=r % 2)


def _gather_wait(src_hbm, dst, sem, n):
    for r in range(n):
        _tile_copy(src_hbm, 0, dst, r, sem).wait()


def _combine_kernel(dest_ref, ys_hbm, x_ref, wt_ref, g2_ref, fw_ref, o_ref, buf_ref, sem_ref, *, n_tok, final_norm):
    i = pl.program_id(0)
    n = pl.num_programs(0)
    tm = x_ref.shape[0]
    slot = i % 2

    def start(tile, sl):
        for k in range(TOP_K):
            _gather_start(ys_hbm, lambda r: dest_ref[k * n_tok + tile * tm + r], buf_ref.at[sl, k], sem_ref.at[sl, k], tm)

    def wait(sl):
        for k in range(TOP_K):
            _gather_wait(ys_hbm, buf_ref.at[sl, k], sem_ref.at[sl, k], tm)

    @pl.when(i == 0)
    def _():
        start(0, 0)

    start(jnp.minimum(i + 1, n - 1), 1 - slot)
    wait(slot)
    wt = wt_ref[...]
    moe = (wt[:, 0:1] * _from_row_tiles(buf_ref.at[slot, 0], tm)
           + wt[:, 1:2] * _from_row_tiles(buf_ref.at[slot, 1], tm))
    out = x_ref[...] + g2_ref[...] * moe
    if final_norm:
        out = out * lax.rsqrt(jnp.mean(out * out, axis=-1, keepdims=True) + NORM_EPS) * fw_ref[...]
    o_ref[...] = out

    @pl.when(i == n - 1)
    def _():
        wait(1 - slot)


def moe_combine(ys, dest, x2, wts, g2, final_w, seq, final_norm):
    t, d = x2.shape
    tm = min(COMB_TM, seq)
    per_b = seq // tm
    grid_spec = pltpu.PrefetchScalarGridSpec(
        num_scalar_prefetch=1,
        grid=(t // tm,),
        in_specs=[
            pl.BlockSpec(memory_space=pl.ANY),
            pl.BlockSpec((tm, d), lambda i, ds: (i, 0)),
            pl.BlockSpec((tm, LANE), lambda i, ds: (i, 0)),
            pl.BlockSpec((None, 1, d), lambda i, ds: (i // per_b, 0, 0)),
            pl.BlockSpec((1, d), lambda i, ds: (0, 0)),
        ],
        out_specs=pl.BlockSpec((tm, d), lambda i, ds: (i, 0)),
        scratch_shapes=[pltpu.VMEM((2, TOP_K, tm * ROW_TILE, LANE), F32), pltpu.SemaphoreType.DMA((2, TOP_K))],
    )
    return pl.pallas_call(
        functools.partial(_combine_kernel, n_tok=t, final_norm=final_norm),
        grid_spec=grid_spec,
        out_shape=jax.ShapeDtypeStruct((t, d), F32),
        compiler_params=_cparams(("arbitrary",)),
        name="moe_combine",
    )(dest, ys, x2, wts, g2, final_w.reshape(1, d))


def kernel(x, c, w_ada, b_ada, norm_mix_w, norm_ffn_w, w_in, conv_w, conv_b, ssd_dt_bias, ssd_a_log, ssd_d, ssd_norm_w, diff_lambda, diff_subln_w, hgrn_lb_logits, hgrn_norm_w, w_merge, w_out, w_router, router_bias, w_expert_gate, w_expert_up, w_expert_down, final_norm_w):
    b, s, d = x.shape
    t = b * s
    depth = w_in.shape[0]
    mod = ada_modulation(c, w_ada, b_ada)
    lb_p = jax.nn.softmax(hgrn_lb_logits.astype(F32), axis=0)
    lower_bounds = jnp.cumsum(lb_p, axis=0) - lb_p[0]
    w_router_t = w_router.T
    x2 = x.reshape(t, d)
    ys = jnp.zeros(((TOP_K * t + N_EXPERTS * MOE_BM) * ROW_TILE, LANE), F32)
    for l in range(depth):
        sh1, sc1, g1, sh2, sc2, g2 = [mod[l, :, i * d:(i + 1) * d].reshape(b, 1, d) for i in range(6)]
        p2 = in_projection(x2, norm_mix_w[l], sc1, sh1, pad_w_in(w_in[l]), s)
        p3 = p2.reshape(b, s, -1)
        o_a = ssd_branch(p3, conv_w[l], conv_b[l], ssd_dt_bias[l], ssd_a_log[l], ssd_d[l], ssd_norm_w[l])
        o_b = dilated_branch(p3)
        o_c = diff_branch(p3, diff_lambda[l], diff_subln_w[l], l)
        o_d = hgrn_branch(p3, lower_bounds[l], hgrn_norm_w[l])
        flat = lambda o: o.reshape(t, BRANCH_W)
        x_mid, h2, idx, pos, wts, counts = merge_and_route(
            flat(o_a), flat(o_b), flat(o_c), flat(o_d), p2, x2, g1, w_merge[l].astype(BF16), w_out[l].astype(BF16),
            norm_ffn_w[l], sc2, sh2, w_router_t, router_bias, s)
        dest, blk_e, n_used = moe_plan(idx, pos, counts)
        xs = moe_dispatch(h2, dest, ys)
        ys = moe_experts(xs, blk_e, n_used, w_expert_gate, w_expert_up, w_expert_down, l)
        x2 = moe_combine(ys, dest, x_mid, wts, g2, final_norm_w, s, final_norm=(l == depth - 1))
    return x2.reshape(b, s, d)
```

```python
import functools
import math

import jax
import jax.numpy as jnp
from jax import lax
from jax.experimental import pallas as pl
from jax.experimental.pallas import tpu as pltpu

F32 = jnp.float32
BF16 = jnp.bfloat16

D_MODEL = 1024
DEPTH = 2
N_BRANCH = 4
BRANCH_W = 512

SSD_D_INNER = 512
SSD_HEAD_DIM = 64
SSD_HEADS = 8
SSD_GROUPS = 2
SSD_HPG = 4
SSD_STATE = 64
SSD_CONV = 4
SSD_CHUNK = 128
SSD_STEP_CHUNKS = 2
SSD_HALO = 16
SSD_CONV_CH = 768

DIL_PAIRS = ((128, 1), (512, 4), (2048, 16))
DIL_HPG = 4
DIL_HEAD_DIM = 128
DIL_HEADS = 12
DIL_BLOCK = 128

DIFF_HEADS = 4
DIFF_HEAD_DIM = 64

HG_HEADS = 8
HG_KEY_DIM = 64
HG_VAL_DIM = 64
HG_CHUNK = 64

N_EXPERTS = 64
N_EXPERT_GROUPS = 8
EXPERTS_PER_GROUP = 8
TOP_K = 2
D_EXPERT = 256

NORM_EPS = 1e-6
MASK_VALUE = -1e30

LANE = 128
VMEM_LIMIT = 48 * 1024 * 1024

C_GATE = 0
C_Z = 32
C_XBC = 36
C_DT = 42
C_QB = 44
C_KB = 56
C_VB = 68
C_QC = 80
C_KC = 84
C_VC = 88
C_QD = 92
C_FD = 96
C_ID = 100
C_GD = 104
N_COLB = 108
D_IN_PAD = N_COLB * LANE
DT_PAD = 2 * LANE


def _cparams(sem):
    return pltpu.CompilerParams(dimension_semantics=sem, vmem_limit_bytes=VMEM_LIMIT)


def _aligned(x, m):
    return x if isinstance(x, int) else pl.multiple_of(x, m)


def _sigmoid(v):
    return 0.5 * jnp.tanh(0.5 * v) + 0.5


def _silu(v):
    return v * _sigmoid(v)


def _bdot(a, b):
    return jnp.dot(a.astype(BF16), b.astype(BF16), preferred_element_type=F32)


def _bdot_nt(a, b):
    return lax.dot_general(a.astype(BF16), b.astype(BF16), (((1,), (1,)), ((), ())),
                           preferred_element_type=F32)


def _bdot_tn(a, b):
    return lax.dot_general(a.astype(BF16), b.astype(BF16), (((0,), (0,)), ((), ())),
                           preferred_element_type=F32)


def _dot_rhs01(x, m01):
    hi, lo = _split_bf16(x)
    return jnp.dot(hi, m01, preferred_element_type=F32) + jnp.dot(lo, m01, preferred_element_type=F32)


def _exact_rows_dot(m01, v):
    hi = v.astype(BF16)
    r1 = v - hi.astype(F32)
    mid = r1.astype(BF16)
    lo = (r1 - mid.astype(F32)).astype(BF16)
    m = m01.astype(BF16)
    return (jnp.dot(m, hi, preferred_element_type=F32) + jnp.dot(m, mid, preferred_element_type=F32)
            + jnp.dot(m, lo, preferred_element_type=F32))


def _ada_kernel(c_ref, w_ref, b_ref, o_ref):
    o_ref[...] = _bdot(_silu(c_ref[...]), w_ref[...]) + b_ref[...]


def ada_modulation(c, w_ada, b_ada):
    depth, d, n = w_ada.shape
    b = c.shape[0]
    bp = 8
    c_pad = jnp.zeros((bp, d), F32).at[:b].set(c)
    tn = 1536
    out = pl.pallas_call(
        _ada_kernel,
        grid=(depth, n // tn),
        in_specs=[
            pl.BlockSpec((bp, d), lambda l, j: (0, 0)),
            pl.BlockSpec((None, d, tn), lambda l, j: (l, 0, j)),
            pl.BlockSpec((None, 1, tn), lambda l, j: (l, 0, j)),
        ],
        out_specs=pl.BlockSpec((None, bp, tn), lambda l, j: (l, 0, j)),
        out_shape=jax.ShapeDtypeStruct((depth, bp, n), F32),
        compiler_params=_cparams(("arbitrary", "arbitrary")),
        name="ada_mod",
    )(c_pad, w_ada, b_ada.reshape(depth, 1, n))
    return out[:, :b]


INPROJ_TM = 2048
INPROJ_TN = 1536


def _inproj_kernel(x_ref, nw_ref, sc_ref, sh_ref, w_ref, o_ref, h_ref):
    @pl.when(pl.program_id(1) == 0)
    def _():
        x = x_ref[...]
        y = x * lax.rsqrt(jnp.mean(x * x, axis=-1, keepdims=True) + NORM_EPS) * nw_ref[...]
        h_ref[...] = (y * (1.0 + sc_ref[...]) + sh_ref[...]).astype(BF16)

    o_ref[...] = jnp.dot(h_ref[...], w_ref[...], preferred_element_type=F32).astype(o_ref.dtype)


def in_projection(x2, norm_w, scale, shift, w_pad, seq):
    t, d = x2.shape
    n = w_pad.shape[1]
    tm = min(INPROJ_TM, seq)
    tn = INPROJ_TN
    per_b = seq // tm
    return pl.pallas_call(
        _inproj_kernel,
        grid=(t // tm, n // tn),
        in_specs=[
            pl.BlockSpec((tm, d), lambda i, j: (i, 0)),
            pl.BlockSpec((1, d), lambda i, j: (0, 0)),
            pl.BlockSpec((None, 1, d), lambda i, j: (i // per_b, 0, 0)),
            pl.BlockSpec((None, 1, d), lambda i, j: (i // per_b, 0, 0)),
            pl.BlockSpec((d, tn), lambda i, j: (0, j)),
        ],
        out_specs=pl.BlockSpec((tm, tn), lambda i, j: (i, j)),
        out_shape=jax.ShapeDtypeStruct((t, n), BF16),
        scratch_shapes=[pltpu.VMEM((tm, d), BF16)],
        compiler_params=_cparams(("arbitrary", "arbitrary")),
        name="in_proj",
    )(x2, norm_w.reshape(1, d), scale, shift, w_pad)


def pad_w_in(w_in_l):
    d = w_in_l.shape[0]
    o_dt = SSD_D_INNER + SSD_CONV_CH
    o_gate = w_in_l.shape[1] - N_BRANCH * D_MODEL
    return jnp.concatenate(
        [w_in_l[:, o_gate:], w_in_l[:, :o_dt + SSD_HEADS], jnp.zeros((d, DT_PAD - SSD_HEADS), w_in_l.dtype),
         w_in_l[:, o_dt + SSD_HEADS:o_gate]], axis=1).astype(BF16)


def _ssd_kernel(z_ref, xbc_ref, dt_ref, cw_ref, cb_ref, dtb_ref, alog_ref, dsk_ref, nw_ref, exp_ref, o_ref,
                xpad_ref, state_ref, y_ref):
    c = pl.program_id(1)
    L = SSD_CHUNK
    halo = SSD_HALO

    @pl.when(c == 0)
    def _():
        xpad_ref[L:L + halo, :] = jnp.zeros((halo, SSD_CONV_CH), BF16)
        state_ref[...] = jnp.zeros_like(state_ref)

    for ci in range(SSD_STEP_CHUNKS):
        rows = slice(ci * L, (ci + 1) * L)
        _ssd_chunk(z_ref[rows, :], xbc_ref[rows, :], dt_ref[rows, :], cw_ref, cb_ref, dtb_ref, alog_ref, dsk_ref, nw_ref,
                   exp_ref, o_ref, xpad_ref, state_ref, y_ref, rows)


def _ssd_chunk(z_in, xbc_in, dt_in, cw_ref, cb_ref, dtb_ref, alog_ref, dsk_ref, nw_ref, exp_ref, o_ref, xpad_ref, state_ref,
               y_ref, rows):
    L = SSD_CHUNK
    halo = SSD_HALO
    xpad_ref[0:halo, :] = xpad_ref[L:L + halo, :]
    z = z_in.astype(F32)
    xpad_ref[halo:halo + L, :] = xbc_in

    xpad = xpad_ref[...]
    t_i = lax.broadcasted_iota(jnp.int32, (L, halo + L), 0)
    m_i = lax.broadcasted_iota(jnp.int32, (L, halo + L), 1)
    conv = cb_ref[...] + cw_ref[SSD_CONV - 1:SSD_CONV, :] * xbc_in.astype(F32)
    for j in range(SSD_CONV - 1):
        shift = jnp.where(m_i == t_i + (halo - (SSD_CONV - 1) + j), 1.0, 0.0).astype(BF16)
        conv = conv + cw_ref[j:j + 1, :] * jnp.dot(shift, xpad, preferred_element_type=F32)
    xc = _silu(conv)
    xs = xc[:, 0:SSD_D_INNER]
    ns = SSD_GROUPS * SSD_STATE
    bm = xc[:, SSD_D_INNER:SSD_D_INNER + ns]
    cm = xc[:, SSD_D_INNER + ns:SSD_D_INNER + 2 * ns]

    dt = jax.nn.softplus(dt_in[:, 0:LANE].astype(F32) + dtb_ref[...])
    a = -jnp.exp(alog_ref[...])
    adt = dt * a
    row = lax.broadcasted_iota(jnp.int32, (L, L), 0)
    col = lax.broadcasted_iota(jnp.int32, (L, L), 1)
    tril = row >= col
    tril_f = jnp.where(tril, 1.0, 0.0)
    a_cs = _exact_rows_dot(tril_f, adt)
    a_cs_t = a_cs.T

    expand = exp_ref[...]
    dt_x = _dot_rhs01(dt, expand)
    a_cs_x = _exact_rows_dot(tril_f, _dot_rhs01(adt, expand))
    tot_x = a_cs_x[L - 1:L, :]
    xdt = xs * dt_x
    xdt_b = xdt.astype(BF16)
    xdec_b = (xdt * jnp.exp(tot_x - a_cs_x)).astype(BF16)
    dec_out = jnp.exp(a_cs_x)
    dec_tot = jnp.exp(tot_x)

    npair = SSD_HEADS // 2
    grp_of_pair = [(2 * p) // SSD_HPG for p in range(npair)]
    lane = lax.broadcasted_iota(jnp.int32, (L, LANE), 1)
    first = lane < SSD_STATE
    bm_b = bm.astype(BF16)
    cm_b = cm.astype(BF16)
    cbs = [lax.dot_general(jnp.where(first if g == 0 else jnp.logical_not(first), cm, 0.0).astype(BF16), bm_b,
                           (((1,), (1,)), ((), ())), preferred_element_type=F32) for g in range(SSD_GROUPS)]
    lmats = [jnp.exp(jnp.where(tril, jnp.broadcast_to(a_cs[:, e:e + 1], (L, L)) - a_cs_t[e:e + 1, :], MASK_VALUE))
             for e in range(SSD_HEADS)]
    ms = [(cbs[e // SSD_HPG] * lmats[e]).astype(BF16) for e in range(SSD_HEADS)]
    sls = [slice(p * LANE, (p + 1) * LANE) for p in range(npair)]
    diag = [jnp.dot(ms[e], xdt_b[:, sls[e // 2]], preferred_element_type=F32) for e in range(SSD_HEADS)]
    sts = [state_ref[p] for p in range(npair)]
    offs = [jnp.dot(cm_b, sts[p].astype(BF16), preferred_element_type=F32) for p in range(npair)]
    bm_t = bm.T.astype(BF16)
    locs = [jnp.dot(bm_t, xdec_b[:, sls[p]], preferred_element_type=F32) for p in range(npair)]
    sub = lax.broadcasted_iota(jnp.int32, (LANE, LANE), 0)
    for p in range(npair):
        own_rows = (sub < SSD_STATE) if grp_of_pair[p] == 0 else (sub >= SSD_STATE)
        state_ref[p] = dec_tot[:, sls[p]] * sts[p] + jnp.where(own_rows, locs[p], 0.0)
    for p in range(npair):
        y = jnp.where(first, diag[2 * p], diag[2 * p + 1]) + offs[p] * dec_out[:, sls[p]] + dsk_ref[:, sls[p]] * xs[:, sls[p]]
        y_ref[:, sls[p]] = y

    yz = y_ref[...] * _silu(z)
    o_ref[rows, :] = yz * lax.rsqrt(jnp.mean(yz * yz, axis=-1, keepdims=True) + NORM_EPS) * nw_ref[...]


def _pad_lanes(v, n=LANE):
    return jnp.zeros((1, n), F32).at[0, :v.shape[0]].set(v.astype(F32))


def ssd_branch(p3, conv_w, conv_b, dt_bias, a_log, d_skip, norm_w):
    b, s, _ = p3.shape
    L = SSD_CHUNK * SSD_STEP_CHUNKS
    const = lambda shape: pl.BlockSpec(shape, lambda i, c: (0, 0))
    assert SSD_HEAD_DIM == SSD_STATE and LANE == 2 * SSD_HEAD_DIM and SSD_HPG % 2 == 0
    import numpy as np
    expand = (np.arange(LANE)[:, None] == np.arange(SSD_D_INNER)[None, :] // SSD_HEAD_DIM).astype(np.float32)
    return pl.pallas_call(
        _ssd_kernel,
        grid=(b, s // L),
        in_specs=[
            pl.BlockSpec((None, L, SSD_D_INNER), lambda i, c: (i, c, C_Z * LANE // SSD_D_INNER)),
            pl.BlockSpec((None, L, SSD_CONV_CH), lambda i, c: (i, c, C_XBC * LANE // SSD_CONV_CH)),
            pl.BlockSpec((None, L, DT_PAD), lambda i, c: (i, c, C_DT * LANE // DT_PAD)),
            const((SSD_CONV, SSD_CONV_CH)),
            const((1, SSD_CONV_CH)),
            const((1, LANE)),
            const((1, LANE)),
            const((1, SSD_D_INNER)),
            const((1, SSD_D_INNER)),
            const((LANE, SSD_D_INNER)),
        ],
        out_specs=pl.BlockSpec((None, L, SSD_D_INNER), lambda i, c: (i, c, 0)),
        out_shape=jax.ShapeDtypeStruct((b, s, SSD_D_INNER), F32),
        scratch_shapes=[
            pltpu.VMEM((SSD_CHUNK + SSD_HALO, SSD_CONV_CH), BF16),
            pltpu.VMEM((SSD_HEADS // 2, LANE, LANE), F32),
            pltpu.VMEM((SSD_CHUNK, SSD_D_INNER), F32),
        ],
        compiler_params=_cparams(("arbitrary", "arbitrary")),
        name="ssd",
    )(p3, p3, p3, conv_w, conv_b.reshape(1, -1), _pad_lanes(dt_bias), _pad_lanes(a_log),
      jnp.repeat(d_skip.astype(F32), SSD_HEAD_DIM).reshape(1, -1), norm_w.reshape(1, -1), jnp.asarray(expand, BF16))


DIL_ORDER = (2, 1, 0)
DIL_CHEAP_STRIDE = 4
DIL_UNROLL = 8


def _dil_kernel(q_ref, k_ref, v_ref, o_ref, m_ref, l_ref, tmp_ref, tmp2_ref, qd_ref, kd_ref, vd_ref):
    g = pl.program_id(2)
    s_len = q_ref.shape[0]
    blk = DIL_BLOCK
    scale = DIL_HEAD_DIM ** -0.5

    def run_group(dil, first_group):
        sub = s_len // dil
        nblk = sub // blk

        if dil == 1:
            qd_ref[...] = (q_ref[...].astype(F32) * scale).astype(BF16)
            kd, vd = k_ref, v_ref
        else:
            for src, dst, mul in ((q_ref, qd_ref, scale), (k_ref, kd_ref, None), (v_ref, vd_ref, None)):
                x = src[...].astype(F32)
                tmp_ref[...] = x if mul is None else x * mul

                if dil <= DIL_CHEAP_STRIDE:
                    for r in range(dil):
                        dst[r * sub:(r + 1) * sub, :] = tmp_ref[pl.ds(r, sub, stride=dil), :].astype(BF16)
                else:
                    inner = DIL_CHEAP_STRIDE
                    outer = dil // inner
                    part = s_len // inner
                    for r_lo in range(inner):
                        tmp2_ref[r_lo * part:(r_lo + 1) * part, :] = tmp_ref[pl.ds(r_lo, part, stride=inner), :]
                    for r_lo in range(inner):
                        for r_hi in range(outer):
                            r = r_lo + inner * r_hi
                            dst[r * sub:(r + 1) * sub, :] = tmp2_ref[pl.ds(r_lo * part + r_hi, sub, stride=outer), :].astype(BF16)
            kd, vd = kd_ref, vd_ref

        def block_softmax(r, b, nkey):
            row0 = _aligned(r * sub + b * blk, blk)
            ks = pl.ds(_aligned(row0 - (nkey - blk), blk), nkey)
            i = lax.broadcasted_iota(jnp.int32, (blk, nkey), 0) + (nkey - blk)
            j = lax.broadcasted_iota(jnp.int32, (blk, nkey), 1)
            ok = jnp.logical_and(j <= i, j >= i - blk)
            sc = lax.dot_general(qd_ref[pl.ds(row0, blk), :], kd[ks, :], (((1,), (1,)), ((), ())),
                                 preferred_element_type=F32)
            sc = jnp.where(ok, sc, MASK_VALUE)
            m_blk = jnp.max(sc, axis=-1, keepdims=True)
            p = jnp.exp(sc - m_blk).astype(BF16)
            v_ext = jnp.concatenate([vd[ks, :], jnp.ones((nkey, LANE), BF16)], axis=1)
            return m_blk, jnp.dot(p, v_ext, preferred_element_type=F32)

        def merge(r, b, m_blk, pv):
            base = r + dil * blk * b
            acc_rows = pl.ds(_aligned(base, blk), blk) if dil == 1 else pl.ds(base, blk, stride=dil)
            m_blk = jnp.broadcast_to(m_blk, (blk, LANE))
            if first_group:
                o_ref[acc_rows, :] = pv[:, :LANE]
                l_ref[acc_rows, :] = pv[:, LANE:]
                m_ref[acc_rows, :] = m_blk
                return
            m_old = m_ref[acc_rows, :]
            m_new = jnp.maximum(m_old, m_blk)
            a_old = jnp.exp(m_old - m_new)
            a_blk = jnp.exp(m_blk - m_new)
            o_ref[acc_rows, :] = a_old * o_ref[acc_rows, :] + a_blk * pv[:, :LANE]
            l_ref[acc_rows, :] = a_old * l_ref[acc_rows, :] + a_blk * pv[:, LANE:]
            m_ref[acc_rows, :] = m_new

        def run_units(n, where, nkey):
            def group(units):
                parts = [block_softmax(r, b, nkey) for r, b in units]
                for (r, b), (m_blk, pv) in zip(units, parts):
                    merge(r, b, m_blk, pv)

            def body(i, carry):
                group([where(i * DIL_UNROLL + j) for j in range(DIL_UNROLL)])
                return carry

            lax.fori_loop(0, n // DIL_UNROLL, body, 0)
            if n % DIL_UNROLL:
                group([where(n - n % DIL_UNROLL + j) for j in range(n % DIL_UNROLL)])

        run_units(dil, lambda u: (u, 0), blk)
        if nblk > 1:
            run_units(dil * (nblk - 1), lambda u: (u // (nblk - 1), 1 + u % (nblk - 1)), 2 * blk)

    for step, gi in enumerate(DIL_ORDER):
        window, dil = DIL_PAIRS[gi]
        assert window // dil == blk

        @pl.when(g == step)
        def _(dil=dil, step=step):
            run_group(dil, step == 0)

    @pl.when(g == len(DIL_PAIRS) - 1)
    def _():
        o_ref[...] = o_ref[...] / l_ref[...]


def dilated_branch(p3):
    b, s, _ = p3.shape
    ng = len(DIL_PAIRS)
    last = len(DIL_PAIRS) - 1
    assert DIL_ORDER == tuple(range(last, -1, -1))
    spec = lambda c0: pl.BlockSpec((None, s, DIL_HEAD_DIM), lambda i, j, g: (i, 0, c0 + (last - g) * DIL_HPG + j))
    return pl.pallas_call(
        _dil_kernel,
        grid=(b, DIL_HPG, ng),
        in_specs=[spec(C_QB), spec(C_KB), spec(C_VB)],
        out_specs=pl.BlockSpec((None, s, DIL_HEAD_DIM), lambda i, j, g: (i, 0, j)),
        out_shape=jax.ShapeDtypeStruct((b, s, BRANCH_W), F32),
        scratch_shapes=[pltpu.VMEM((s, LANE), F32), pltpu.VMEM((s, LANE), F32), pltpu.VMEM((s, LANE), F32),
                        pltpu.VMEM((s, LANE), F32), pltpu.VMEM((s, LANE), BF16), pltpu.VMEM((s, LANE), BF16), pltpu.VMEM((s, LANE), BF16)],
        compiler_params=_cparams(("arbitrary", "arbitrary", "arbitrary")),
        name="dilated",
    )(p3, p3, p3)


DIFF_TQ = 512
DIFF_TK = 1024
DIFF_DIAG = 512
LOG2E = 1.4426950408889634


DIFF_ONES = 16


def _diff_kernel(lam_ref, q_ref, k_ref, v_ref, nw_ref, o_ref, vt_ref, m_ref, acc_ref, *, lam_init):
    qi = pl.program_id(2)
    tq, tk, dh = DIFF_TQ, DIFF_TK, DIFF_HEAD_DIM
    w = 2 * dh
    s_len = k_ref.shape[0]

    @pl.when(qi == 0)
    def _():
        for c0 in range(0, s_len, tk):
            vt_ref[0:w, c0:c0 + tk] = v_ref[c0:c0 + tk, :].astype(F32).T.astype(BF16)
        vt_ref[w:w + DIFF_ONES, :] = jnp.ones((DIFF_ONES, s_len), BF16)

    q_t = (q_ref[...].astype(F32) * (dh ** -0.5 * LOG2E)).T
    half = lax.broadcasted_iota(jnp.int32, (w, tq), 0) < dh
    q_sel = [jnp.where(half, q_t, 0.0).astype(BF16), jnp.where(half, 0.0, q_t).astype(BF16)]
    for t in range(2):
        m_ref[t] = jnp.full((8, tq), MASK_VALUE, F32)
        acc_ref[t] = jnp.zeros((w + DIFF_ONES, tq), F32)

    def blocks(items):
        work = []
        for kstart, width, qlo in items:
            c0 = qlo or 0
            nq = tq - c0
            kb = k_ref[pl.ds(kstart, width), :]
            vb = vt_ref[:, pl.ds(kstart, width)]
            for t in range(2):
                sc = jnp.dot(kb, q_sel[t][:, c0:], preferred_element_type=F32)
                if qlo is not None:
                    row = lax.broadcasted_iota(jnp.int32, (width, nq), 0)
                    col = lax.broadcasted_iota(jnp.int32, (width, nq), 1)
                    sc = jnp.where(row <= col, sc, MASK_VALUE)
                work.append((t, c0, nq, width, vb, sc))
        m_blks = [jnp.max(jnp.max(sc.reshape(width // 8, 8, nq), axis=0), axis=0, keepdims=True)
                  for (t, c0, nq, width, vb, sc) in work]
        pvs = [jnp.dot(vb, jnp.exp2(sc - m_blk).astype(BF16), preferred_element_type=F32)
               for (t, c0, nq, width, vb, sc), m_blk in zip(work, m_blks)]
        for (t, c0, nq, width, vb, sc), m_blk, pv in zip(work, m_blks, pvs):
            m_old = m_ref[t, 0:1, c0:]
            m_new = jnp.maximum(m_old, m_blk)
            acc_ref[t, :, c0:] = acc_ref[t, :, c0:] * jnp.exp2(m_old - m_new) + pv * jnp.exp2(m_blk - m_new)
            m_ref[t, :, c0:] = jnp.broadcast_to(m_new, (8, nq))

    def main_step(j, carry):
        blocks([(pl.multiple_of(j * tk, tk), tk, None)])
        return carry

    per = tk // tq
    lax.fori_loop(0, qi // per, main_step, 0)
    for r in range(per):
        @pl.when(qi % per == r)
        def _(r=r):
            left = [(pl.multiple_of((qi // per) * tk + i * tq, tq), tq, None) for i in range(r)]
            diag = [(pl.multiple_of(qi * tq + c0, DIFF_DIAG), DIFF_DIAG, c0) for c0 in range(0, tq, DIFF_DIAG)]
            blocks(left + diag)

    lam_p = lam_ref[...]
    lam = (jnp.exp(jnp.sum(lam_p[0:1] * lam_p[1:2], axis=-1, keepdims=True))
           - jnp.exp(jnp.sum(lam_p[2:3] * lam_p[3:4], axis=-1, keepdims=True)) + lam_init)
    a0, a1 = acc_ref[0], acc_ref[1]
    o_t = a0[0:w] / a0[w:w + 1] - lam * (a1[0:w] / a1[w:w + 1])
    o = o_t.T
    o = o * lax.rsqrt(jnp.mean(o * o, axis=-1, keepdims=True) + NORM_EPS) * nw_ref[...]
    o_ref[...] = o * (1.0 - lam_init)


def diff_branch(p3, diff_lambda, subln_w, layer):
    b, s, _ = p3.shape
    lam_init = 0.8 - 0.6 * math.exp(-0.3 * layer)
    w = 2 * DIFF_HEAD_DIM
    return pl.pallas_call(
        functools.partial(_diff_kernel, lam_init=lam_init),
        grid=(b, DIFF_HEADS, s // DIFF_TQ),
        in_specs=[
            pl.BlockSpec((4, DIFF_HEAD_DIM), lambda i, h, t: (0, 0)),
            pl.BlockSpec((None, DIFF_TQ, w), lambda i, h, t: (i, t, C_QC + h)),
            pl.BlockSpec((None, s, w), lambda i, h, t: (i, 0, C_KC + h)),
            pl.BlockSpec((None, s, w), lambda i, h, t: (i, 0, C_VC + h)),
            pl.BlockSpec((1, w), lambda i, h, t: (0, 0)),
        ],
        out_specs=pl.BlockSpec((None, DIFF_TQ, w), lambda i, h, t: (i, t, h)),
        out_shape=jax.ShapeDtypeStruct((b, s, BRANCH_W), F32),
        scratch_shapes=[pltpu.VMEM((w + DIFF_ONES, s), BF16), pltpu.VMEM((2, 8, DIFF_TQ), F32),
                        pltpu.VMEM((2, w + DIFF_ONES, DIFF_TQ), F32)],
        compiler_params=_cparams(("arbitrary", "arbitrary", "arbitrary")),
        name="diff_attn",
    )(diff_lambda, p3, p3, p3, subln_w.reshape(1, w))


HG_LEVELS = 6
HG_STEP_CHUNKS = 4


def _hgrn_tables():
    import numpy as np
    c = HG_CHUNK
    i = np.arange(c)[:, None]
    j = np.arange(c)[None, :]
    tril = (j <= i).astype(np.float32)
    masks = []
    for lv in range(HG_LEVELS):
        sz = 1 << lv
        m = ((i // (2 * sz) == j // (2 * sz)) & (i // sz == j // sz + 1)).astype(np.float32)
        masks.append(np.concatenate([m, m], axis=0))
    lane = np.arange(LANE)
    same_head = (lane[:, None] // HG_KEY_DIM == lane[None, :] // HG_KEY_DIM).astype(np.float32)
    return tril, np.stack(masks, axis=0), same_head


def _hgrn_kernel(q_ref, f_ref, i_ref, g_ref, lb_ref, tri_ref, msk_ref, sh_ref, nw_ref, o_ref, state_ref, gpad_ref):
    @pl.when(pl.program_id(1) == 0)
    def _():
        state_ref[...] = jnp.zeros_like(state_ref)

    for ci in range(HG_STEP_CHUNKS):
        rows = slice(ci * HG_CHUNK, (ci + 1) * HG_CHUNK)
        _hgrn_chunk(q_ref[rows, :], f_ref[rows, :], i_ref[rows, :], g_ref, lb_ref, tri_ref, msk_ref, sh_ref, nw_ref,
                    o_ref, state_ref, gpad_ref, rows)


def _hgrn_chunk(q_in, f_in, i_in, g_ref, lb_ref, tri_ref, msk_ref, sh_ref, nw_ref, o_ref, state_ref, gpad_ref, rows):
    cs = HG_CHUNK
    nl = HG_LEVELS
    w = q_in.shape[1]
    sub_rows = 8

    lb = lb_ref[...]
    f_gate = lb + (1.0 - lb) * jax.nn.sigmoid(f_in.astype(F32))
    log_f = jnp.log(f_gate)
    k_in = 1.0 - f_gate
    q = _silu(q_in.astype(F32))
    v = i_in.astype(F32)

    hi, lo = _split_bf16(log_f)
    tri = tri_ref[...]
    g = jnp.dot(tri, hi, preferred_element_type=F32) + jnp.dot(tri, lo, preferred_element_type=F32)
    gpad_ref[0:sub_rows, :] = jnp.zeros((sub_rows, w), F32)
    gpad_ref[sub_rows:sub_rows + cs, :] = g
    g_last = g[cs - 1:cs, :]

    def g_row(r):
        return jnp.broadcast_to(gpad_ref[sub_rows + r:sub_rows + r + 1, :], (sub_rows, w))

    sub = lax.broadcasted_iota(jnp.int32, (sub_rows, w), 0)
    tiles = [[] for _ in range(nl)]
    for t in range(cs // sub_rows):
        r0 = t * sub_rows
        gt = g[r0:r0 + sub_rows]
        prev = gpad_ref[r0 + sub_rows - 1:r0 + 2 * sub_rows - 1, :]
        tiles[0].append(jnp.where(sub % 2 == 1, gt - prev, 0.0))
        c1 = jnp.where(sub < 4, g_row(r0 + 1), g_row(r0 + 5))
        tiles[1].append(jnp.where((sub // 2) % 2 == 1, gt - c1, c1 - gt))
        c2 = g_row(r0 + 3)
        tiles[2].append(jnp.where(sub >= 4, gt - c2, c2 - gt))
        for lv in range(3, nl):
            span = (1 << lv) // sub_rows
            mid = (t // (2 * span)) * 2 * span + span
            cm = g_row(mid * sub_rows - 1)
            tiles[lv].append(gt - cm if (t // span) % 2 == 1 else cm - gt)
    dec = [jnp.exp(jnp.concatenate(tl, axis=0)) for tl in tiles]
    first_w = lax.broadcasted_iota(jnp.int32, (cs, w), 1) % LANE < HG_KEY_DIM
    q_even = jnp.where(first_w, q, 0.0).astype(BF16)
    q_odd = jnp.where(first_w, 0.0, q).astype(BF16)
    k_b = k_in.astype(BF16)
    dec_b = [d.astype(BF16) for d in dec]
    q_full = q * jnp.exp(g)
    k_full = k_in * jnp.exp(g_last - g)
    decay_row = jnp.exp(g_last)

    same_head = sh_ref[...]
    same_head_b = same_head.astype(BF16)
    first = lax.broadcasted_iota(jnp.int32, (cs, LANE), 1) < HG_KEY_DIM
    pairs = range(w // LANE)
    sls = [slice(p * LANE, (p + 1) * LANE) for p in pairs]
    attns = [jnp.zeros((2 * cs, cs), F32) for _ in pairs]
    for lv in range(nl):
        for p in pairs:
            db = dec_b[lv][:, sls[p]]
            lhs = jnp.concatenate([q_even[:, sls[p]] * db, q_odd[:, sls[p]] * db], axis=0)
            scores = lax.dot_general(lhs, k_b[:, sls[p]] * db, (((1,), (1,)), ((), ())), preferred_element_type=F32)
            attns[p] = attns[p] + msk_ref[lv] * scores
    vps = [v[:, sl] for sl in sls]
    vpbs = [vp.astype(BF16) for vp in vps]
    rs = [jnp.dot(attns[p].astype(BF16), vpbs[p], preferred_element_type=F32) for p in pairs]
    diags = [jnp.dot((q[:, sl] * k_in[:, sl]).astype(BF16), same_head_b, preferred_element_type=F32) for sl in sls]
    sts = [state_ref[p] for p in pairs]
    inters = [_bdot_nt(q_full[:, sls[p]], sts[p]) for p in pairs]
    upds = [_bdot_tn(vpbs[p], k_full[:, sls[p]]) for p in pairs]
    for p in pairs:
        state_ref[p] = decay_row[:, sls[p]] * sts[p] + same_head * upds[p]
    outs = [jnp.where(first, rs[p][0:cs], rs[p][cs:2 * cs]) + diags[p] * vps[p] + inters[p] for p in pairs]
    mss = [jnp.dot((o * o).astype(BF16), same_head_b, preferred_element_type=F32) * (1.0 / HG_VAL_DIM) for o in outs]
    for p in pairs:
        o = outs[p] * lax.rsqrt(mss[p] + NORM_EPS) * nw_ref[...]
        o_ref[rows, sls[p]] = o * _silu(g_ref[rows, sls[p]].astype(F32))


def hgrn_branch(p3, lower_bound, norm_w):
    b, s, _ = p3.shape
    cs = HG_CHUNK * HG_STEP_CHUNKS
    w = HG_HEADS * HG_KEY_DIM
    assert HG_KEY_DIM == HG_VAL_DIM and LANE == 2 * HG_KEY_DIM
    tril, masks, same_head = _hgrn_tables()
    seg = lambda c0: pl.BlockSpec((None, cs, w), lambda i, c: (i, c, c0 * LANE // w))
    return pl.pallas_call(
        _hgrn_kernel,
        grid=(b, s // cs),
        in_specs=[
            seg(C_QD), seg(C_FD), seg(C_ID), seg(C_GD),
            pl.BlockSpec((1, w), lambda i, c: (0, 0)),
            pl.BlockSpec(tril.shape, lambda i, c: (0, 0)),
            pl.BlockSpec(masks.shape, lambda i, c: (0, 0, 0)),
            pl.BlockSpec(same_head.shape, lambda i, c: (0, 0)),
            pl.BlockSpec((1, LANE), lambda i, c: (0, 0)),
        ],
        out_specs=pl.BlockSpec((None, cs, w), lambda i, c: (i, c, 0)),
        out_shape=jax.ShapeDtypeStruct((b, s, BRANCH_W), F32),
        scratch_shapes=[pltpu.VMEM((w // LANE, LANE, LANE), F32), pltpu.VMEM((HG_CHUNK + 8, w), F32)],
        compiler_params=_cparams(("arbitrary", "arbitrary")),
        name="hgrn2",
    )(p3, p3, p3, p3, lower_bound.reshape(1, w).astype(F32), jnp.asarray(tril, BF16), jnp.asarray(masks, F32),
      jnp.asarray(same_head, F32), jnp.tile(norm_w.astype(F32), LANE // HG_VAL_DIM).reshape(1, LANE))


MERGE_TM = 512


def _split_bf16(v):
    hi = v.astype(BF16)
    return hi, (v - hi.astype(F32)).astype(BF16)


def _first_index_of_max(vals, iota, n):
    top = jnp.max(vals, axis=0, keepdims=True)
    idx = jnp.min(jnp.where(vals == top, iota, n), axis=0, keepdims=True)
    return top, idx


def _merge_kernel(oa_ref, ob_ref, oc_ref, od_ref, gl_ref, x_ref, g1_ref, wm_ref, wo_ref,
                  nw_ref, sc_ref, sh_ref, wr_ref, rb_ref,
                  xo_ref, h2_ref, idx_ref, pos_ref, wt_ref, cnt_ref, carry_ref):
    step = pl.program_id(0)
    tm = x_ref.shape[0]
    d = D_MODEL

    projs = [jnp.dot(o_ref[...].astype(BF16), wm_ref[n], preferred_element_type=F32)
             for n, o_ref in enumerate((oa_ref, ob_ref, oc_ref, od_ref))]
    acc = jnp.zeros((tm, d), F32)
    for n, proj in enumerate(projs):
        acc = acc + _sigmoid(gl_ref[:, n * d:(n + 1) * d].astype(F32)) * proj
    mix = jnp.dot(acc.astype(BF16), wo_ref[...], preferred_element_type=F32)
    x_new = x_ref[...] + g1_ref[...] * mix
    xo_ref[...] = x_new

    y = x_new * lax.rsqrt(jnp.mean(x_new * x_new, axis=-1, keepdims=True) + NORM_EPS) * nw_ref[...]
    h2 = y * (1.0 + sc_ref[...]) + sh_ref[...]
    _to_row_tiles(h2_ref, h2)

    h_hi, h_lo = _split_bf16(h2)
    w_hi, w_lo = _split_bf16(wr_ref[...])
    nt = lambda a, b: lax.dot_general(a, b, (((1,), (1,)), ((), ())), preferred_element_type=F32)
    logits = nt(w_hi, h_hi) + nt(w_hi, h_lo) + nt(w_lo, h_hi)
    scores = jax.nn.sigmoid(logits)
    sel = scores + rb_ref[...]

    ne, ng, pg = N_EXPERTS, N_EXPERT_GROUPS, EXPERTS_PER_GROUP
    iota_g = lax.broadcasted_iota(jnp.int32, (pg, tm), 0)
    best_score = None
    best_group = None
    for g in range(ng):
        xg = sel[g * pg:(g + 1) * pg, :]
        top1, i1 = _first_index_of_max(xg, iota_g, pg)
        top2 = jnp.max(jnp.where(iota_g == i1, -jnp.inf, xg), axis=0, keepdims=True)
        gs = top1 + top2
        if g == 0:
            best_score, best_group = gs, jnp.zeros((1, tm), jnp.int32)
        else:
            better = gs > best_score
            best_score = jnp.where(better, gs, best_score)
            best_group = jnp.where(better, g, best_group)

    iota_e = lax.broadcasted_iota(jnp.int32, (ne, tm), 0)
    masked = jnp.where(iota_e // pg == best_group, sel, MASK_VALUE)
    _, e1 = _first_index_of_max(masked, iota_e, ne)
    oh1 = iota_e == e1
    _, e2 = _first_index_of_max(jnp.where(oh1, MASK_VALUE, masked), iota_e, ne)
    oh2 = iota_e == e2
    w1 = jnp.sum(jnp.where(oh1, scores, 0.0), axis=0, keepdims=True)
    w2 = jnp.sum(jnp.where(oh2, scores, 0.0), axis=0, keepdims=True)
    wsum = w1 + w2
    w1 = w1 / wsum
    w2 = w2 / wsum

    @pl.when(step == 0)
    def _():
        carry_ref[...] = jnp.zeros_like(carry_ref)

    f1 = jnp.where(oh1, 1.0, 0.0)
    f2 = jnp.where(oh2, 1.0, 0.0)
    both = f1 + f2
    r_i = lax.broadcasted_iota(jnp.int32, (tm, tm), 0)
    c_i = lax.broadcasted_iota(jnp.int32, (tm, tm), 1)
    before = jnp.where(r_i < c_i, 1.0, 0.0).astype(BF16)
    rank = jnp.dot(both.astype(BF16), before, preferred_element_type=F32) + carry_ref[:, 0:1]
    p1 = jnp.sum(f1 * rank, axis=0, keepdims=True)
    p2 = jnp.sum(f2 * rank, axis=0, keepdims=True)
    carry_new = carry_ref[...] + jnp.sum(both, axis=1, keepdims=True)
    carry_ref[...] = carry_new
    cnt_ref[...] = carry_new.astype(jnp.int32)

    idx_ref[...] = jnp.concatenate([e1, e2], axis=0)
    pos_ref[...] = jnp.concatenate([p1, p2], axis=0).astype(jnp.int32)
    wt_rows = jnp.concatenate([w1, w2, jnp.zeros((LANE - 2, tm), F32)], axis=0)
    wt_ref[...] = wt_rows.T


def merge_and_route(o_a, o_b, o_c, o_d, p2, x2, g1, w_merge_bf, w_out_bf, norm_w, sc2, sh2, w_router_t, router_bias, seq):
    t, d = x2.shape
    tm = min(MERGE_TM, seq)
    per_b = seq // tm
    gw = N_BRANCH * d
    tok = lambda w: pl.BlockSpec((tm, w), lambda i: (i, 0))
    bat = pl.BlockSpec((None, 1, d), lambda i: (i // per_b, 0, 0))
    ne = N_EXPERTS
    return pl.pallas_call(
        _merge_kernel,
        grid=(t // tm,),
        in_specs=[
            tok(BRANCH_W), tok(BRANCH_W), tok(BRANCH_W), tok(BRANCH_W),
            pl.BlockSpec((tm, gw), lambda i: (i, C_GATE * LANE // gw)),
            tok(d), bat,
            pl.BlockSpec((N_BRANCH, BRANCH_W, d), lambda i: (0, 0, 0)),
            pl.BlockSpec((d, d), lambda i: (0, 0)),
            pl.BlockSpec((1, d), lambda i: (0, 0)),
            bat, bat,
            pl.BlockSpec((ne, d), lambda i: (0, 0)),
            pl.BlockSpec((ne, 1), lambda i: (0, 0)),
        ],
        out_specs=[
            tok(d), pl.BlockSpec((tm * ROW_TILE, LANE), lambda i: (i, 0)),
            pl.BlockSpec((TOP_K, tm), lambda i: (0, i)),
            pl.BlockSpec((TOP_K, tm), lambda i: (0, i)),
            tok(LANE),
            pl.BlockSpec((ne, LANE), lambda i: (0, 0)),
        ],
        out_shape=[
            jax.ShapeDtypeStruct((t, d), F32),
            jax.ShapeDtypeStruct((t * ROW_TILE, LANE), F32),
            jax.ShapeDtypeStruct((TOP_K, t), jnp.int32),
            jax.ShapeDtypeStruct((TOP_K, t), jnp.int32),
            jax.ShapeDtypeStruct((t, LANE), F32),
            jax.ShapeDtypeStruct((ne, LANE), jnp.int32),
        ],
        scratch_shapes=[pltpu.VMEM((ne, LANE), F32)],
        compiler_params=_cparams(("arbitrary",)),
        name="merge_route",
    )(o_a, o_b, o_c, o_d, p2, x2, g1, w_merge_bf, w_out_bf, norm_w.reshape(1, d), sc2, sh2,
      w_router_t, router_bias.reshape(ne, 1))


MOE_BM = 256


def _plan_kernel(idx_ref, pos_ref, cnt_ref, dest_ref, blke_ref, nused_ref):
    bm = MOE_BM
    ne = N_EXPERTS
    cnt = cnt_ref[...].astype(F32)
    padded = jnp.floor((cnt + (bm - 1)) * (1.0 / bm)) * bm
    r = lax.broadcasted_iota(jnp.int32, (ne, ne), 0)
    c = lax.broadcasted_iota(jnp.int32, (ne, ne), 1)
    pstart = _exact_rows_dot(jnp.where(c < r, 1.0, 0.0), padded)
    pend = pstart + padded

    idx = idx_ref[...]
    base = jnp.zeros(idx.shape, F32)
    for e in range(ne):
        base = jnp.where(idx == e, pstart[e:e + 1, 0:1], base)
    dest_ref[...] = base.astype(jnp.int32) + pos_ref[...]

    nbp = blke_ref.shape[1]
    blk_start = (lax.broadcasted_iota(jnp.int32, (ne, nbp), 1) * bm).astype(F32)
    done = jnp.sum(jnp.where(pend[:, 0:1] <= blk_start, 1.0, 0.0), axis=0, keepdims=True)
    blke_ref[...] = jnp.minimum(done, ne - 1.0).astype(jnp.int32)
    nused_ref[...] = (pend[ne - 1:ne, :] * (1.0 / bm)).astype(jnp.int32)


def moe_plan(idx, pos, counts):
    n_tok = idx.shape[1]
    n_rows = TOP_K * n_tok + N_EXPERTS * MOE_BM
    n_blocks = n_rows // MOE_BM
    nbp = -(-n_blocks // LANE) * LANE
    dest, blk_e, n_used = pl.pallas_call(
        _plan_kernel,
        out_shape=[
            jax.ShapeDtypeStruct((TOP_K, n_tok), jnp.int32),
            jax.ShapeDtypeStruct((1, nbp), jnp.int32),
            jax.ShapeDtypeStruct((1, LANE), jnp.int32),
        ],
        compiler_params=pltpu.CompilerParams(vmem_limit_bytes=VMEM_LIMIT),
        name="moe_plan",
    )(idx, pos, counts)
    return dest.reshape(-1), blk_e[0, :n_blocks], n_used[0, :1]


ROW_TILE = 8
DISPATCH_TM = 512


def _to_row_tiles(ref, x):
    rows = x.shape[0]
    for j in range(ROW_TILE):
        ref[pl.ds(j, rows, stride=ROW_TILE), :] = x[:, j * LANE:(j + 1) * LANE]


def _from_row_tiles(ref, rows):
    return jnp.concatenate([ref[pl.ds(j, rows, stride=ROW_TILE), :] for j in range(ROW_TILE)], axis=1)


def _tile_rows(row):
    return pl.ds(_aligned(row * ROW_TILE, ROW_TILE), ROW_TILE)


def _dispatch_kernel(dest_ref, h_ref, xs_init_hbm, xs_hbm, sem_ref, *, n_tok):
    del xs_init_hbm
    i = pl.program_id(0)
    tm = h_ref.shape[0] // ROW_TILE

    def copy(r, k, row):
        return pltpu.make_async_copy(h_ref.at[pl.ds(r * ROW_TILE, ROW_TILE), :], xs_hbm.at[_tile_rows(row), :], sem_ref.at[k])

    for r in range(tm):
        for k in range(TOP_K):
            copy(r, k, dest_ref[k * n_tok + i * tm + r]).start(priority=k)
    for k in range(TOP_K):
        for r in range(tm):
            copy(r, k, 0).wait()


def moe_dispatch(h2_tiles, dest, xs_init):
    n_tok = dest.shape[0] // TOP_K
    tm = min(DISPATCH_TM, n_tok)
    grid_spec = pltpu.PrefetchScalarGridSpec(
        num_scalar_prefetch=1,
        grid=(n_tok // tm,),
        in_specs=[pl.BlockSpec((tm * ROW_TILE, LANE), lambda i, ds: (i, 0)), pl.BlockSpec(memory_space=pl.ANY)],
        out_specs=pl.BlockSpec(memory_space=pl.ANY),
        scratch_shapes=[pltpu.SemaphoreType.DMA((TOP_K,))],
    )
    return pl.pallas_call(
        functools.partial(_dispatch_kernel, n_tok=n_tok),
        grid_spec=grid_spec,
        out_shape=jax.ShapeDtypeStruct(xs_init.shape, F32),
        input_output_aliases={2: 0},
        compiler_params=_cparams(("arbitrary",)),
        name="moe_dispatch",
    )(dest, h2_tiles, xs_init)


def _expert_kernel(blke_ref, nused_ref, x_ref, wg_ref, wu_ref, wd_ref, y_ref, wgb_ref, wub_ref, wdb_ref):
    b = pl.program_id(0)
    bm = MOE_BM
    used = nused_ref[0]
    changed = jnp.logical_or(b == 0, blke_ref[b] != blke_ref[jnp.maximum(b - 1, 0)])

    @pl.when(jnp.logical_and(b < used, changed))
    def _():
        wgb_ref[...] = wg_ref[...].astype(BF16)
        wub_ref[...] = wu_ref[...].astype(BF16)
        wdb_ref[...] = wd_ref[...].astype(BF16)

    @pl.when(b < used)
    def _():
        xb = _from_row_tiles(x_ref, bm).astype(BF16)
        hid = _silu(jnp.dot(xb, wgb_ref[...], preferred_element_type=F32)) * jnp.dot(xb, wub_ref[...], preferred_element_type=F32)
        _to_row_tiles(y_ref, jnp.dot(hid.astype(BF16), wdb_ref[...], preferred_element_type=F32))

    @pl.when(b >= used)
    def _():
        y_ref[...] = jnp.zeros_like(y_ref)


def moe_experts(xs, blk_e, n_used, w_gate, w_up, w_down, layer):
    bm = MOE_BM
    n_rows = xs.shape[0] // ROW_TILE
    d, de = w_gate.shape[-2:]
    assert d == ROW_TILE * LANE
    wspec = lambda r, c: pl.BlockSpec((None, None, r, c), lambda b, be, nu: (layer, be[b], 0, 0))
    rows = pl.BlockSpec((bm * ROW_TILE, LANE), lambda b, be, nu: (b, 0))
    rows_in = pl.BlockSpec((bm * ROW_TILE, LANE), lambda b, be, nu: (jnp.minimum(b, nu[0]), 0))
    grid_spec = pltpu.PrefetchScalarGridSpec(
        num_scalar_prefetch=2,
        grid=(n_rows // bm,),
        in_specs=[rows_in, wspec(d, de), wspec(d, de), wspec(de, d)],
        out_specs=rows,
        scratch_shapes=[pltpu.VMEM((d, de), BF16), pltpu.VMEM((d, de), BF16), pltpu.VMEM((de, d), BF16)],
    )
    return pl.pallas_call(
        _expert_kernel,
        grid_spec=grid_spec,
        out_shape=jax.ShapeDtypeStruct(xs.shape, F32),
        compiler_params=_cparams(("arbitrary",)),
        name="moe_experts",
    )(blk_e, n_used, xs, w_gate, w_up, w_down)


COMB_TM = 512


def _tile_copy(src_hbm, row, dst, r, sem):
    return pltpu.make_async_copy(src_hbm.at[_tile_rows(row), :], dst.at[pl.ds(r * ROW_TILE, ROW_TILE), :], sem)


def _gather_start(src_hbm, row_of, dst, sem, n):
    for r in range(n):
        _tile_copy(src_hbm, row_of(r), dst, r, sem).start(priority=r % 2)


def _gather_wait(src_hbm, dst, sem, n):
    for r in range(n):
        _tile_copy(src_hbm, 0, dst, r, sem).wait()


def _combine_kernel(dest_ref, ys_hbm, x_ref, wt_ref, g2_ref, fw_ref, o_ref, buf_ref, sem_ref, *, n_tok, final_norm):
    i = pl.program_id(0)
    n = pl.num_programs(0)
    tm = x_ref.shape[0]
    slot = i % 2

    def start(tile, sl):
        for k in range(TOP_K):
            _gather_start(ys_hbm, lambda r: dest_ref[k * n_tok + tile * tm + r], buf_ref.at[sl, k], sem_ref.at[sl, k], tm)

    def wait(sl):
        for k in range(TOP_K):
            _gather_wait(ys_hbm, buf_ref.at[sl, k], sem_ref.at[sl, k], tm)

    @pl.when(i == 0)
    def _():
        start(0, 0)

    start(jnp.minimum(i + 1, n - 1), 1 - slot)
    wait(slot)
    wt = wt_ref[...]
    moe = (wt[:, 0:1] * _from_row_tiles(buf_ref.at[slot, 0], tm)
           + wt[:, 1:2] * _from_row_tiles(buf_ref.at[slot, 1], tm))
    out = x_ref[...] + g2_ref[...] * moe
    if final_norm:
        out = out * lax.rsqrt(jnp.mean(out * out, axis=-1, keepdims=True) + NORM_EPS) * fw_ref[...]
    o_ref[...] = out

    @pl.when(i == n - 1)
    def _():
        wait(1 - slot)


def moe_combine(ys, dest, x2, wts, g2, final_w, seq, final_norm):
    t, d = x2.shape
    tm = min(COMB_TM, seq)
    per_b = seq // tm
    grid_spec = pltpu.PrefetchScalarGridSpec(
        num_scalar_prefetch=1,
        grid=(t // tm,),
        in_specs=[
            pl.BlockSpec(memory_space=pl.ANY),
            pl.BlockSpec((tm, d), lambda i, ds: (i, 0)),
            pl.BlockSpec((tm, LANE), lambda i, ds: (i, 0)),
            pl.BlockSpec((None, 1, d), lambda i, ds: (i // per_b, 0, 0)),
            pl.BlockSpec((1, d), lambda i, ds: (0, 0)),
        ],
        out_specs=pl.BlockSpec((tm, d), lambda i, ds: (i, 0)),
        scratch_shapes=[pltpu.VMEM((2, TOP_K, tm * ROW_TILE, LANE), F32), pltpu.SemaphoreType.DMA((2, TOP_K))],
    )
    return pl.pallas_call(
        functools.partial(_combine_kernel, n_tok=t, final_norm=final_norm),
        grid_spec=grid_spec,
        out_shape=jax.ShapeDtypeStruct((t, d), F32),
        compiler_params=_cparams(("arbitrary",)),
        name="moe_combine",
    )(dest, ys, x2, wts, g2, final_w.reshape(1, d))


def kernel(x, c, w_ada, b_ada, norm_mix_w, norm_ffn_w, w_in, conv_w, conv_b, ssd_dt_bias, ssd_a_log, ssd_d, ssd_norm_w, diff_lambda, diff_subln_w, hgrn_lb_logits, hgrn_norm_w, w_merge, w_out, w_router, router_bias, w_expert_gate, w_expert_up, w_expert_down, final_norm_w):
    b, s, d = x.shape
    t = b * s
    depth = w_in.shape[0]
    mod = ada_modulation(c, w_ada, b_ada)
    lb_p = jax.nn.softmax(hgrn_lb_logits.astype(F32), axis=0)
    lower_bounds = jnp.cumsum(lb_p, axis=0) - lb_p[0]
    w_router_t = w_router.T
    x2 = x.reshape(t, d)
    ys = jnp.zeros(((TOP_K * t + N_EXPERTS * MOE_BM) * ROW_TILE, LANE), F32)
    for l in range(depth):
        sh1, sc1, g1, sh2, sc2, g2 = [mod[l, :, i * d:(i + 1) * d].reshape(b, 1, d) for i in range(6)]
        p2 = in_projection(x2, norm_mix_w[l], sc1, sh1, pad_w_in(w_in[l]), s)
        p3 = p2.reshape(b, s, -1)
        o_a = ssd_branch(p3, conv_w[l], conv_b[l], ssd_dt_bias[l], ssd_a_log[l], ssd_d[l], ssd_norm_w[l])
        o_b = dilated_branch(p3)
        o_c = diff_branch(p3, diff_lambda[l], diff_subln_w[l], l)
        o_d = hgrn_branch(p3, lower_bounds[l], hgrn_norm_w[l])
        flat = lambda o: o.reshape(t, BRANCH_W)
        x_mid, h2, idx, pos, wts, counts = merge_and_route(
            flat(o_a), flat(o_b), flat(o_c), flat(o_d), p2, x2, g1, w_merge[l].astype(BF16), w_out[l].astype(BF16),
            norm_ffn_w[l], sc2, sh2, w_router_t, router_bias, s)
        dest, blk_e, n_used = moe_plan(idx, pos, counts)
        xs = moe_dispatch(h2, dest, ys)
        ys = moe_experts(xs, blk_e, n_used, w_expert_gate, w_expert_up, w_expert_down, l)
        x2 = moe_combine(ys, dest, x_mid, wts, g2, final_norm_w, s, final_norm=(l == depth - 1))
    return x2.reshape(b, s, d)
```
